```python
import math
import jax, jax.numpy as jnp
from jax import lax
import numpy as np

D_MODEL = 1024
BATCH = 32
SEQ = 256
DEPTH = 2
DEC_BATCH = 2
DEC_SEQ = 2048
PAST_LEN = 512

GRID_W = 64
D_MIX = D_MODEL
POOL_W = D_MIX // 2
POOL_WINDOWS = (2, 4, 8, 16)
N_POOL_GROUPS = 4
POOL_GROUP_DIM = POOL_W // N_POOL_GROUPS
MLSTM_W = D_MIX - POOL_W
MLSTM_HEADS = 4
MLSTM_HEAD_DIM = MLSTM_W // MLSTM_HEADS
QK_CONV_W = 3
CHUNK = 64
D_FF = 2816
N_GATE_COLS = 4 * MLSTM_HEADS
IN_COLS = POOL_W + 4 * MLSTM_W + N_GATE_COLS
ALPHA = (2.0 * DEPTH) ** 0.25
BETA = (8.0 * DEPTH) ** -0.25
LN_EPS = 1e-5

kernel_name = 'hybrid_pool_mlstm_diffusion_step'


def layer_norm(x, g=None, b=None):
    xf = x.astype(jnp.float32)
    mu = xf.mean(-1, keepdims=True)
    var = jnp.square(xf - mu).mean(-1, keepdims=True)
    y = (xf - mu) * lax.rsqrt(var + LN_EPS)
    if g is not None:
        y = y * g + b
    return y.astype(x.dtype)


def modulate(x, shift, scale):
    return x * (1.0 + scale[:, None, :]) + shift[:, None, :]


def window_mean(x, w, axis):
    L = x.shape[axis]
    cs = jnp.cumsum(x.astype(jnp.float32), axis=axis)
    zero = jnp.zeros_like(lax.slice_in_dim(cs, 0, 1, axis=axis))
    S = jnp.concatenate([zero, cs], axis=axis)
    t = jnp.arange(L)
    hi = jnp.minimum(t + w // 2, L)
    lo = jnp.maximum(t - w // 2, 0)
    s = jnp.take(S, hi, axis=axis) - jnp.take(S, lo, axis=axis)
    shape = [1] * x.ndim
    shape[axis] = L
    return s / (hi - lo).astype(jnp.float32).reshape(shape)


def pool_mixer(xp, w_pool, pool_scale):
    B, R, W, _ = xp.shape
    grp = xp.reshape(B, R, W, N_POOL_GROUPS, POOL_GROUP_DIM)
    outs = []
    for gi, win in enumerate(POOL_WINDOWS):
        xg = grp[..., gi, :]
        outs.append(window_mean(window_mean(xg, win, 1), win, 2) - xg)
    y = jnp.stack(outs, axis=-2)
    y = jnp.einsum('brwgc,gcd->brwgd', y, w_pool)
    return y.reshape(B, R * W, POOL_W) * pool_scale


def dwconv1d_centred(x, w):
    L = x.shape[1]
    xp = jnp.pad(x, ((0, 0), (1, 1), (0, 0)))
    return xp[:, :L] * w[0] + xp[:, 1:L + 1] * w[1] + xp[:, 2:] * w[2]


def mlstm_chunkwise(q, k, v, i_pre, f_pre, C0, n0, m0):
    B, H, L, DH = q.shape
    nc = L // CHUNK
    f32 = jnp.float32

    def chunks(a):
        a = a.astype(f32).reshape((B, H, nc, CHUNK) + a.shape[3:])
        return jnp.moveaxis(a, 2, 0)

    xs = (chunks(q), chunks(k), chunks(v), chunks(i_pre), chunks(jax.nn.log_sigmoid(f_pre.astype(f32))))
    tri = jnp.tril(jnp.ones((CHUNK, CHUNK), dtype=bool))

    def step(carry, inp):
        C, n, m = carry
        qb, kb, vb, li, lf = inp
        b = jnp.cumsum(lf, axis=-1)
        log_d = jnp.where(tri, b[..., :, None] - b[..., None, :] + li[..., None, :], -jnp.inf)
        inter = b + m[..., None]
        m_t = jnp.maximum(inter, log_d.max(-1))
        d = jnp.exp(log_d - m_t[..., None])
        a_inter = jnp.exp(inter - m_t)
        s = jnp.einsum('bhtd,bhsd->bhts', qb, kb) * d
        num = a_inter[..., None] * jnp.einsum('bhtd,bhde->bhte', qb, C) + jnp.einsum('bhts,bhse->bhte', s, vb)
        den = a_inter * jnp.einsum('bhtd,bhd->bht', qb, n) + s.sum(-1)
        h = num / jnp.maximum(jnp.abs(den), jnp.exp(-m_t))[..., None]
        g = b[..., -1]
        log_w = g[..., None] - b + li
        m_new = jnp.maximum(g + m, log_w.max(-1))
        w = jnp.exp(log_w - m_new[..., None])
        decay = jnp.exp(g + m - m_new)
        C_new = decay[..., None, None] * C + jnp.einsum('bhs,bhsd,bhse->bhde', w, kb, vb)
        n_new = decay[..., None] * n + jnp.einsum('bhs,bhsd->bhd', w, kb)
        return (C_new, n_new, m_new), h

    (C, n, m), hs = lax.scan(step, (C0.astype(f32), n0.astype(f32), m0.astype(f32)), xs)
    h = jnp.moveaxis(hs, 0, 2).reshape(B, H, L, DH)
    return h, (C, n, m)


def mlstm_mixer(q, k, v, o, gates, w_qk_conv, b_gates, hn_g, st0):
    B, L, _ = q.shape
    qk = jax.nn.silu(dwconv1d_centred(jnp.concatenate([q, k], axis=-1), w_qk_conv))
    q, k = jnp.split(qk, 2, axis=-1)

    def heads(a):
        return a.reshape(B, L, MLSTM_HEADS, MLSTM_HEAD_DIM).transpose(0, 2, 1, 3)

    qh, kh, vh = heads(q), heads(k) * (MLSTM_HEAD_DIM ** -0.5), heads(v)
    g = (gates + b_gates).reshape(B, L, 4, MLSTM_HEADS).transpose(2, 0, 3, 1)
    C0, n0, m0 = st0
    h_f, (Cf, nf, mf) = mlstm_chunkwise(qh, kh, vh, g[0], g[1], C0[:, 0], n0[:, 0], m0[:, 0])
    flip = lambda a: jnp.flip(a, axis=2)
    h_b, (Cb, nb, mb) = mlstm_chunkwise(flip(qh), flip(kh), flip(vh), jnp.flip(g[2], -1), jnp.flip(g[3], -1),
                                        C0[:, 1], n0[:, 1], m0[:, 1])
    h = h_f + flip(h_b)
    mu = h.mean(-1, keepdims=True)
    var = jnp.square(h - mu).mean(-1, keepdims=True)
    h = ((h - mu) * lax.rsqrt(var + LN_EPS)).transpose(0, 2, 1, 3).reshape(B, L, MLSTM_W) * hn_g
    out = (jax.nn.sigmoid(o) * h).astype(q.dtype)
    dt = q.dtype
    new_state = (jnp.stack([Cf, Cb], axis=1).astype(dt), jnp.stack([nf, nb], axis=1).astype(dt),
                 jnp.stack([mf, mb], axis=1).astype(dt))
    return out, new_state


def conv_ffn(h, rows, cols, w_up, w_dw, b_dw, w_down):
    B, L, _ = h.shape
    up = h @ w_up
    a, g = jnp.split(up, 2, axis=-1)
    g4 = g.reshape(B, rows, cols, D_FF)
    g4 = lax.conv_general_dilated(g4, w_dw.reshape(3, 3, 1, D_FF).astype(g4.dtype), (1, 1), 'SAME',
                                  dimension_numbers=('NHWC', 'HWIO', 'NHWC'), feature_group_count=D_FF)
    g = g4.reshape(B, L, D_FF) + b_dw
    return (jax.nn.gelu(g) * a) @ w_down


def trunk_layer(x, cond, rows, cols, st0, p):
    B, L, _ = x.shape
    ada = jax.nn.silu(cond) @ p['w_ada'] + p['b_ada']
    sh1, sc1, g1, sh2, sc2, g2 = jnp.split(ada, 6, axis=-1)
    h = modulate(layer_norm(x), sh1, sc1)
    proj = h @ p['w_in']
    o1 = POOL_W
    xp = proj[..., :o1]
    q = proj[..., o1:o1 + MLSTM_W]
    k = proj[..., o1 + MLSTM_W:o1 + 2 * MLSTM_W]
    v = proj[..., o1 + 2 * MLSTM_W:o1 + 3 * MLSTM_W]
    o = proj[..., o1 + 3 * MLSTM_W:o1 + 4 * MLSTM_W]
    gates = proj[..., o1 + 4 * MLSTM_W:]
    pool_out = pool_mixer(xp.reshape(B, rows, cols, POOL_W), p['w_pool'], p['pool_scale']).astype(x.dtype)
    ml_out, st_new = mlstm_mixer(q, k, v, o, gates, p['w_qk_conv'], p['b_gates'], p['hn_g'], st0)
    mix = jnp.concatenate([pool_out, ml_out], axis=-1) @ p['w_out']
    x = layer_norm(ALPHA * x + g1[:, None, :] * mix, p['ln1_g'], p['ln1_b'])
    h = modulate(layer_norm(x), sh2, sc2)
    ffn = conv_ffn(h, rows, cols, p['w_up'], p['w_dw'], p['b_dw'], p['w_down'])
    x = layer_norm(ALPHA * x + g2[:, None, :] * ffn, p['ln2_g'], p['ln2_b'])
    return x, st_new


def setup_inputs(seed: int = 0) -> dict:
    key = jax.random.key(seed)
    ks = jax.random.split(key, 26)
    f32 = jnp.float32

    def nrm(k, shape, s):
        return jax.random.normal(k, shape, f32) * s

    H, DH = MLSTM_HEADS, MLSTM_HEAD_DIM
    gate_base = jnp.array([0.0, 3.0, 0.0, 3.0], f32)[None, :, None]
    return {
        'x_prompt': nrm(ks[0], (BATCH, SEQ, D_MODEL), 1.0),
        'x_sample': nrm(ks[1], (DEC_BATCH, DEC_SEQ, D_MODEL), 1.0),
        'c': nrm(ks[2], (DEC_BATCH, D_MODEL), 1.0),
        'state_C': nrm(ks[3], (DEC_BATCH, DEPTH, 2, H, DH, DH), DH ** -0.5),
        'state_n': nrm(ks[4], (DEC_BATCH, DEPTH, 2, H, DH), 0.5),
        'state_m': nrm(ks[5], (DEC_BATCH, DEPTH, 2, H), 1.0),
        'c_ctx': nrm(ks[6], (D_MODEL,), 1.0),
        'w_ada': nrm(ks[7], (DEPTH, D_MODEL, 6 * D_MODEL), D_MODEL ** -0.5),
        'b_ada': nrm(ks[8], (DEPTH, 6 * D_MODEL), 0.02),
        'w_in': nrm(ks[9], (DEPTH, D_MODEL, IN_COLS), D_MODEL ** -0.5),
        'b_gates': (gate_base + nrm(ks[10], (DEPTH, 4, H), 0.5)).reshape(DEPTH, N_GATE_COLS),
        'w_qk_conv': nrm(ks[11], (DEPTH, QK_CONV_W, 2 * MLSTM_W), QK_CONV_W ** -0.5),
        'hn_g': 1.0 + nrm(ks[12], (DEPTH, MLSTM_W), 0.05),
        'w_pool': nrm(ks[13], (DEPTH, N_POOL_GROUPS, POOL_GROUP_DIM, POOL_GROUP_DIM), POOL_GROUP_DIM ** -0.5),
        'pool_scale': 1.0 + nrm(ks[14], (DEPTH, POOL_W), 0.05),
        'w_out': nrm(ks[15], (DEPTH, D_MIX, D_MODEL), BETA * D_MIX ** -0.5),
        'ln1_g': 1.0 + nrm(ks[16], (DEPTH, D_MODEL), 0.05),
        'ln1_b': nrm(ks[17], (DEPTH, D_MODEL), 0.02),
        'w_up': nrm(ks[18], (DEPTH, D_MODEL, 2 * D_FF), D_MODEL ** -0.5),
        'w_dw': nrm(ks[19], (DEPTH, 3, 3, D_FF), 1.0 / 3.0),
        'b_dw': nrm(ks[20], (DEPTH, D_FF), 0.02),
        'w_down': nrm(ks[21], (DEPTH, D_FF, D_MODEL), BETA * D_FF ** -0.5),
        'ln2_g': 1.0 + nrm(ks[22], (DEPTH, D_MODEL), 0.05),
        'ln2_b': nrm(ks[23], (DEPTH, D_MODEL), 0.02),
    }


def reference(x_prompt, x_sample, c, state_C, state_n, state_m, c_ctx, w_ada, b_ada, w_in, b_gates,
              w_qk_conv, hn_g, w_pool, pool_scale, w_out, ln1_g, ln1_b, w_up, w_dw, b_dw, w_down,
              ln2_g, ln2_b):
    B_p, L_p, _ = x_prompt.shape
    L_s = x_sample.shape[1]
    rows = L_s // GRID_W
    dt = x_prompt.dtype
    zero_state = (jnp.zeros((B_p, 2, MLSTM_HEADS, MLSTM_HEAD_DIM, MLSTM_HEAD_DIM), dt),
                  jnp.zeros((B_p, 2, MLSTM_HEADS, MLSTM_HEAD_DIM), dt),
                  jnp.zeros((B_p, 2, MLSTM_HEADS), dt))
    y_p, y_s = x_prompt, x_sample
    Cs, ns, ms = [], [], []
    for l in range(DEPTH):
        p = dict(w_ada=w_ada[l], b_ada=b_ada[l], w_in=w_in[l], b_gates=b_gates[l], w_qk_conv=w_qk_conv[l],
                 hn_g=hn_g[l], w_pool=w_pool[l], pool_scale=pool_scale[l], w_out=w_out[l], ln1_g=ln1_g[l],
                 ln1_b=ln1_b[l], w_up=w_up[l], w_dw=w_dw[l], b_dw=b_dw[l], w_down=w_down[l], ln2_g=ln2_g[l],
                 ln2_b=ln2_b[l])
        y_p, (Cl, nl, ml) = trunk_layer(y_p, c_ctx[None, :], 1, L_p, zero_state, p)
        Cs.append(Cl)
        ns.append(nl)
        ms.append(ml)
        y_s, _ = trunk_layer(y_s, c, rows, GRID_W, (state_C[:, l], state_n[:, l], state_m[:, l]), p)
    new_state_C = jnp.stack(Cs, axis=1)
    new_state_n = jnp.stack(ns, axis=1)
    new_state_m = jnp.stack(ms, axis=1)
    return (y_p, y_s, new_state_C, new_state_n, new_state_m)
```

```python
import functools
import math

import numpy as np
import jax
import jax.numpy as jnp
from jax import lax
from jax.experimental import pallas as pl
from jax.experimental.pallas import tpu as pltpu

F32 = jnp.float32
BF16 = jnp.bfloat16

GRID_W = 64
POOL_WINDOWS = (2, 4, 8, 16)
N_POOL_GROUPS = len(POOL_WINDOWS)
MLSTM_HEADS = 4
LN_EPS = 1e-5

LANES = 128
SUBLANES = 8
VMEM_LIMIT_BYTES = 56 * 1024 * 1024

COND_ROWS = SUBLANES
MLSTM_CHUNK = 256
POOL_TILE = 256
TOKEN_TILE = 512
FFN_TOKENS = 2048
FFN_COLS = 256
LN_ROWS = 256


def _params(*sem):
    return pltpu.CompilerParams(dimension_semantics=sem, vmem_limit_bytes=VMEM_LIMIT_BYTES)


def _layer_norm(x):
    mu = jnp.mean(x, axis=-1, keepdims=True)
    xc = x - mu
    var = jnp.mean(xc * xc, axis=-1, keepdims=True)
    return xc * lax.rsqrt(var + LN_EPS)


def _dot(a, b):
    return jnp.dot(a, b, preferred_element_type=F32)


def _ada_kernel(cond_ref, w_ref, b_ref, out_ref):
    cnd = cond_ref[...]
    act = (cnd * jax.nn.sigmoid(cnd)).astype(BF16)
    out_ref[...] = _dot(act, w_ref[...].astype(BF16)) + b_ref[...]


def _ada(cond, w_ada, b_ada):
    depth, d, n = w_ada.shape
    tn = 1536
    return pl.pallas_call(
        _ada_kernel,
        out_shape=jax.ShapeDtypeStruct((depth, COND_ROWS, n), F32),
        grid=(depth, n // tn),
        in_specs=[
            pl.BlockSpec((COND_ROWS, d), lambda l, j: (0, 0)),
            pl.BlockSpec((None, d, tn), lambda l, j: (l, 0, j)),
            pl.BlockSpec((None, 1, tn), lambda l, j: (l, 0, j)),
        ],
        out_specs=pl.BlockSpec((None, COND_ROWS, tn), lambda l, j: (l, 0, j)),
        compiler_params=_params("parallel", "parallel"),
        name="ada",
    )(cond, w_ada, b_ada.reshape(depth, 1, n))


def _inproj_kernel(x_ref, mod_ref, w_ref, wg_ref, proj_ref, gates_ref):
    h = _layer_norm(x_ref[...]) * (1.0 + mod_ref[1:2, :]) + mod_ref[0:1, :]
    h = h.astype(BF16)
    proj_ref[...] = _dot(h, w_ref[...])
    gates_ref[...] = _dot(h, wg_ref[...])


def _inproj(x, mod, cond_row, w_main, w_gate):
    tokens, d = x.shape
    n = w_main.shape[1]
    ng = w_gate.shape[1]
    tm = TOKEN_TILE
    return pl.pallas_call(
        _inproj_kernel,
        out_shape=(jax.ShapeDtypeStruct((tokens, n), F32), jax.ShapeDtypeStruct((tokens, ng), F32)),
        grid=(tokens // tm,),
        in_specs=[
            pl.BlockSpec((tm, d), lambda i: (i, 0)),
            pl.BlockSpec((None, 6, d), lambda i: (cond_row(i * tm), 0, 0)),
            pl.BlockSpec((d, n), lambda i: (0, 0)),
            pl.BlockSpec((d, ng), lambda i: (0, 0)),
        ],
        out_specs=(pl.BlockSpec((tm, n), lambda i: (i, 0)), pl.BlockSpec((tm, ng), lambda i: (i, 0))),
        compiler_params=_params("parallel"),
        name="inproj",
    )(x, mod, w_main, w_gate)


def _pool_col_matrices(cols):
    t = np.arange(POOL_TILE)
    row, col = t // cols, t % cols
    mats = []
    for win in POOL_WINDOWS:
        hw = win // 2
        lo = np.maximum(col - hw, 0)
        hi = np.minimum(col + hw, cols)
        same_row = row[:, None] == row[None, :]
        inside = (col[None, :] >= lo[:, None]) & (col[None, :] < hi[:, None])
        mats.append((same_row & inside).astype(np.float32))
    return jnp.asarray(np.stack(mats), dtype=BF16)


def _pool_kernel(xp_ref, pmat_ref, wp_ref, ps_ref, out_ref, pad_ref, *, rows, cols, tokens):
    gdim = LANES
    tile = POOL_TILE
    halo = (max(POOL_WINDOWS) // 2) * cols if rows > 1 else 0
    t = lax.broadcasted_iota(jnp.int32, (tile, 1), 0)
    col = t & (cols - 1)
    if rows > 1:
        zeros = jnp.zeros((halo, gdim), F32)
        pad_ref[0:halo, :] = zeros
        pad_ref[halo + tokens:halo + tokens + halo, :] = zeros
    for gi, win in enumerate(POOL_WINDOWS):
        hw = win // 2
        gs = slice(gi * gdim, (gi + 1) * gdim)
        if rows > 1:
            pad_ref[halo:halo + tokens, :] = xp_ref[:, gs]
        cnt_c = (jnp.minimum(col + hw, cols) - jnp.maximum(col - hw, 0)).astype(F32)
        pm = pmat_ref[gi]
        wp = wp_ref[gi]
        scale = ps_ref[:, gs]

        def tile_body(tt, carry):
            off = pl.multiple_of(tt * tile, tile)
            xg = xp_ref[pl.ds(off, tile), gs]
            if rows > 1:
                acc = None
                for dr in range(-hw, hw):
                    term = pad_ref[pl.ds(pl.multiple_of(off + (halo + dr * cols), SUBLANES), tile), :]
                    acc = term if acc is None else acc + term
                rw = (off + t) >> int(math.log2(cols))
                cnt_r = (jnp.minimum(rw + hw, rows) - jnp.maximum(rw - hw, 0)).astype(F32)
                m1 = acc / cnt_r
            else:
                m1 = xg
            hi = m1.astype(BF16)
            lo = (m1 - hi.astype(F32)).astype(BF16)
            m2 = (_dot(pm, hi) + _dot(pm, lo)) / cnt_c
            y = (m2 - xg).astype(BF16)
            out_ref[pl.ds(off, tile), gs] = (_dot(y, wp) * scale).astype(BF16)
            return carry

        lax.fori_loop(0, tokens // tile, tile_body, 0)


def _pool(proj, pmat, w_pool, pool_scale, rows, cols, step_tokens):
    total = proj.shape[0]
    pool_w = N_POOL_GROUPS * LANES
    halo = (max(POOL_WINDOWS) // 2) * cols if rows > 1 else 0
    kern = functools.partial(_pool_kernel, rows=rows, cols=cols, tokens=step_tokens)
    return pl.pallas_call(
        kern,
        out_shape=jax.ShapeDtypeStruct((total, pool_w), BF16),
        grid=(total // step_tokens,),
        in_specs=[
            pl.BlockSpec((step_tokens, pool_w), lambda i: (i, 0)),
            pl.BlockSpec(pmat.shape, lambda i: (0, 0, 0)),
            pl.BlockSpec(w_pool.shape, lambda i: (0, 0, 0)),
            pl.BlockSpec((1, pool_w), lambda i: (0, 0)),
        ],
        out_specs=pl.BlockSpec((step_tokens, pool_w), lambda i: (i, 0)),
        scratch_shapes=[pltpu.VMEM((step_tokens + 2 * halo if rows > 1 else SUBLANES, LANES), F32)],
        compiler_params=_params("parallel"),
        name="pool",
    )(proj, pmat, w_pool, pool_scale)


def _log_sigmoid(x):
    return jnp.minimum(x, 0.0) - jnp.log(1.0 + jnp.exp(-jnp.abs(x)))


def _mlstm_kernel(q_ref, k_ref, v_ref, o_ref, gcol_ref, grow_ref, wq_ref, wk_ref, hng_ref, c0_ref, n0_ref, m0_ref,
                  out_ref, c_ref, n_ref, m_ref, qs, ks, vs, hf, hb, *, seq, chunk):
    dh = q_ref.shape[1]
    nc = seq // chunk
    row = lax.broadcasted_iota(jnp.int32, (seq, 1), 0)

    def conv_silu(x, w_ref):
        w = w_ref[...]
        prev = jnp.where(row == 0, 0.0, pltpu.roll(x, 1, 0))
        nxt = jnp.where(row == seq - 1, 0.0, pltpu.roll(x, seq - 1, 0))
        y = prev * w[0:1, :] + x * w[1:2, :] + nxt * w[2:3, :]
        return y * jax.nn.sigmoid(y)

    qs[...] = conv_silu(q_ref[...], wq_ref).astype(BF16)
    ks[...] = (conv_silu(k_ref[...], wk_ref) * (dh ** -0.5)).astype(BF16)
    vs[...] = v_ref[...].astype(BF16)

    ti = lax.broadcasted_iota(jnp.int32, (chunk, chunk), 0)
    si = lax.broadcasted_iota(jnp.int32, (chunk, chunk), 1)
    lower = si <= ti
    upper = si >= ti

    def step(c, mask, mask_t, gi, gf, state):
        cmat, nvec, m = state
        sl = pl.ds(pl.multiple_of(c * chunk, chunk), chunk)
        qc, kc, vc = qs[sl, :], ks[sl, :], vs[sl, :]
        gcol = gcol_ref[sl, :]
        grow = grow_ref[c]
        li_c = gcol[:, gi:gi + 1]
        lf_c = _log_sigmoid(gcol[:, gf:gf + 1])
        li_r = grow[gi:gi + 1, :]
        lf_r = _log_sigmoid(grow[gf:gf + 1, :])
        b_c = jnp.sum(jnp.where(mask, lf_r, 0.0), axis=1, keepdims=True)
        b_r = jnp.sum(jnp.where(mask_t, lf_c, 0.0), axis=0, keepdims=True)
        log_d = jnp.where(mask, b_c + (li_r - b_r), -jnp.inf)
        inter = b_c + m
        m_t = jnp.maximum(inter, jnp.max(log_d, axis=1, keepdims=True))
        d = jnp.exp(log_d - m_t)
        a_inter = jnp.exp(inter - m_t)
        s = lax.dot_general(qc, kc, (((1,), (1,)), ((), ())), preferred_element_type=F32) * d
        num = a_inter * _dot(qc, cmat.astype(BF16)) + _dot(s.astype(BF16), vc)
        qn = jnp.sum(qc.astype(F32) * nvec, axis=1, keepdims=True)
        den = a_inter * qn + jnp.sum(s, axis=1, keepdims=True)
        h = num * (1.0 / jnp.maximum(jnp.abs(den), jnp.exp(-m_t)))
        g = jnp.sum(lf_c, axis=0, keepdims=True)
        log_w = g - b_c + li_c
        m_new = jnp.maximum(g + m, jnp.max(log_w, axis=0, keepdims=True))
        w = jnp.exp(log_w - m_new)
        decay = jnp.exp(g + m - m_new)
        wk = w * kc.astype(F32)
        c_new = decay * cmat + lax.dot_general(wk.astype(BF16), vc, (((0,), (0,)), ((), ())),
                                               preferred_element_type=F32)
        n_new = decay * nvec + jnp.sum(wk, axis=0, keepdims=True)
        return h, (c_new, n_new, m_new), sl

    def body(i, carry):
        st_f, st_b = carry
        h_f, st_f, sl_f = step(i, lower, upper, 0, 1, st_f)
        hf[sl_f, :] = h_f
        h_b, st_b, sl_b = step(nc - 1 - i, upper, lower, 2, 3, st_b)
        hb[sl_b, :] = h_b
        return st_f, st_b

    init = ((c0_ref[0], n0_ref[0], m0_ref[0]), (c0_ref[1], n0_ref[1], m0_ref[1]))
    if nc == 1:
        st_f, st_b = body(0, init)
    else:
        st_f, st_b = lax.fori_loop(0, nc, body, init)

    h = hf[...] + hb[...]
    hn = _layer_norm(h) * hng_ref[...]
    out_ref[...] = (jax.nn.sigmoid(o_ref[...]) * hn).astype(BF16)
    for di, st in enumerate((st_f, st_b)):
        c_ref[di] = st[0]
        n_ref[di] = st[1]
        m_ref[di] = st[2]


def _mlstm(proj, gcol, grow, w_qk_conv, hn_g, c0, n0, m0, batch, seq, col0):
    heads = MLSTM_HEADS
    dh = c0.shape[-1]
    chunk = min(MLSTM_CHUNK, seq)
    nc = seq // chunk
    cb = col0 // dh
    kern = functools.partial(_mlstm_kernel, seq=seq, chunk=chunk)

    def tok_spec(group):
        return pl.BlockSpec((seq, dh), lambda b, h: (b, cb + group * heads + h))

    st_c = pl.BlockSpec((None, 2, None, dh, dh), lambda b, h: (b, 0, h, 0, 0))
    st_n = pl.BlockSpec((None, 2, None, 1, dh), lambda b, h: (b, 0, h, 0, 0))
    st_m = pl.BlockSpec((None, 2, None, 1, 1), lambda b, h: (b, 0, h, 0, 0))
    out, c_new, n_new, m_new = pl.pallas_call(
        kern,
        out_shape=(
            jax.ShapeDtypeStruct((batch * seq, heads * dh), BF16),
            jax.ShapeDtypeStruct((batch, 2, heads, dh, dh), F32),
            jax.ShapeDtypeStruct((batch, 2, heads, 1, dh), F32),
            jax.ShapeDtypeStruct((batch, 2, heads, 1, 1), F32),
        ),
        grid=(batch, heads),
        in_specs=[
            tok_spec(0), tok_spec(1), tok_spec(2), tok_spec(3),
            pl.BlockSpec((None, None, seq, 4), lambda b, h: (b, h, 0, 0)),
            pl.BlockSpec((None, None, nc, 4, chunk), lambda b, h: (b, h, 0, 0, 0)),
            pl.BlockSpec((3, dh), lambda b, h: (0, h)),
            pl.BlockSpec((3, dh), lambda b, h: (0, heads + h)),
            pl.BlockSpec((1, dh), lambda b, h: (0, h)),
            st_c, st_n, st_m,
        ],
        out_specs=(pl.BlockSpec((seq, dh), lambda b, h: (b, h)), st_c, st_n, st_m),
        scratch_shapes=[
            pltpu.VMEM((seq, dh), BF16), pltpu.VMEM((seq, dh), BF16), pltpu.VMEM((seq, dh), BF16),
            pltpu.VMEM((seq, dh), F32), pltpu.VMEM((seq, dh), F32),
        ],
        compiler_params=_params("parallel", "parallel"),
        name="mlstm",
    )(proj, proj, proj, proj, gcol, grow, w_qk_conv, w_qk_conv, hn_g, c0, n0, m0)
    return out, c_new, n_new, m_new


def _outproj_kernel(pool_ref, ml_ref, x_ref, mod_ref, wp_ref, wm_ref, g_ref, b_ref, x1_ref, h2_ref, *, alpha):
    mix = _dot(pool_ref[...], wp_ref[...]) + _dot(ml_ref[...], wm_ref[...])
    x1 = _layer_norm(alpha * x_ref[...] + mod_ref[2:3, :] * mix) * g_ref[...] + b_ref[...]
    x1_ref[...] = x1
    h2 = _layer_norm(x1) * (1.0 + mod_ref[4:5, :]) + mod_ref[3:4, :]
    h2_ref[...] = h2.astype(BF16)


def _outproj(pool_out, ml_out, x, mod, cond_row, w_out, ln_g, ln_b, alpha):
    tokens, d = x.shape
    half = pool_out.shape[1]
    tm = TOKEN_TILE
    kern = functools.partial(_outproj_kernel, alpha=alpha)
    return pl.pallas_call(
        kern,
        out_shape=(jax.ShapeDtypeStruct((tokens, d), F32), jax.ShapeDtypeStruct((tokens, d), BF16)),
        grid=(tokens // tm,),
        in_specs=[
            pl.BlockSpec((tm, half), lambda i: (i, 0)),
            pl.BlockSpec((tm, half), lambda i: (i, 0)),
            pl.BlockSpec((tm, d), lambda i: (i, 0)),
            pl.BlockSpec((None, 6, d), lambda i: (cond_row(i * tm), 0, 0)),
            pl.BlockSpec((half, d), lambda i: (0, 0)),
            pl.BlockSpec((half, d), lambda i: (1, 0)),
            pl.BlockSpec((1, d), lambda i: (0, 0)),
            pl.BlockSpec((1, d), lambda i: (0, 0)),
        ],
        out_specs=(pl.BlockSpec((tm, d), lambda i: (i, 0)), pl.BlockSpec((tm, d), lambda i: (i, 0))),
        compiler_params=_params("parallel"),
        name="outproj",
    )(pool_out, ml_out, x, mod, w_out, w_out, ln_g, ln_b)


def _gelu_tanh(x):
    return 0.5 * x * (1.0 + jnp.tanh(math.sqrt(2.0 / math.pi) * (x + 0.044715 * (x * x * x))))


def _ffn_kernel(h_ref, x1_ref, mod_ref, wa_ref, wg_ref, wdw_ref, bdw_ref, wd_ref, lng_ref, lnb_ref, out_ref,
                gpad, a_s, u_s, *, rows, cols, tokens, alpha):
    j = pl.program_id(1)
    tf = wa_ref.shape[1]
    base = cols if rows > 1 else 0

    @pl.when(j == 0)
    def _():
        out_ref[...] = jnp.zeros_like(out_ref)

    h = h_ref[...]
    a_s[...] = _dot(h, wa_ref[...])
    if rows > 1:
        zeros = jnp.zeros((cols, tf), F32)
        gpad[0:cols, :] = zeros
        gpad[cols + tokens:cols + tokens + cols, :] = zeros
    gpad[base:base + tokens, :] = _dot(h, wg_ref[...])

    wdw = wdw_ref[...]
    bdw = bdw_ref[...]
    ci = lax.broadcasted_iota(jnp.int32, (cols, 1), 0)
    first = ci == 0
    last = ci == cols - 1
    taps_y = (-1, 0, 1) if rows > 1 else (0,)

    def row_body(r, carry):
        off = pl.multiple_of(r * cols, cols)
        for lb in range(tf // LANES):
            ls = slice(lb * LANES, (lb + 1) * LANES)
            left = mid = right = None
            for dy in taps_y:
                xr = gpad[pl.ds(pl.multiple_of(off + (base + dy * cols), SUBLANES), cols), ls]
                kk = (dy + 1) * 3
                tl, tm_, tr = xr * wdw[kk:kk + 1, ls], xr * wdw[kk + 1:kk + 2, ls], xr * wdw[kk + 2:kk + 3, ls]
                left = tl if left is None else left + tl
                mid = tm_ if mid is None else mid + tm_
                right = tr if right is None else right + tr
            y = (mid + jnp.where(first, 0.0, pltpu.roll(left, 1, 0))
                 + jnp.where(last, 0.0, pltpu.roll(right, cols - 1, 0)) + bdw[:, ls])
            u = _gelu_tanh(y) * a_s[pl.ds(off, cols), ls]
            u_s[pl.ds(off, cols), ls] = u.astype(BF16)
        return carry

    lax.fori_loop(0, tokens // cols, row_body, 0)
    out_ref[...] += _dot(u_s[...], wd_ref[...])

    @pl.when(j == pl.num_programs(1) - 1)
    def _():
        gate = mod_ref[5:6, :]
        lng = lng_ref[...]
        lnb = lnb_ref[...]

        def ln_body(r, carry):
            sl = pl.ds(pl.multiple_of(r * LN_ROWS, LN_ROWS), LN_ROWS)
            z = alpha * x1_ref[sl, :] + gate * out_ref[sl, :]
            out_ref[sl, :] = _layer_norm(z) * lng + lnb
            return carry

        lax.fori_loop(0, tokens // LN_ROWS, ln_body, 0)


def _ffn(h2, x1, mod, cond_row, w_up, w_dw, b_dw, w_down, ln_g, ln_b, rows, cols, alpha):
    total, d = x1.shape
    d_ff = w_down.shape[0]
    tm = FFN_TOKENS
    tf = FFN_COLS
    nj = d_ff // tf
    kern = functools.partial(_ffn_kernel, rows=rows, cols=cols, tokens=tm, alpha=alpha)
    return pl.pallas_call(
        kern,
        out_shape=jax.ShapeDtypeStruct((total, d), F32),
        grid=(total // tm, nj),
        in_specs=[
            pl.BlockSpec((tm, d), lambda i, j: (i, 0)),
            pl.BlockSpec((tm, d), lambda i, j: (i, 0)),
            pl.BlockSpec((None, 6, d), lambda i, j: (cond_row(i * tm), 0, 0)),
            pl.BlockSpec((d, tf), lambda i, j: (0, j)),
            pl.BlockSpec((d, tf), lambda i, j: (0, nj + j)),
            pl.BlockSpec((9, tf), lambda i, j: (0, j)),
            pl.BlockSpec((1, tf), lambda i, j: (0, j)),
            pl.BlockSpec((tf, d), lambda i, j: (j, 0)),
            pl.BlockSpec((1, d), lambda i, j: (0, 0)),
            pl.BlockSpec((1, d), lambda i, j: (0, 0)),
        ],
        out_specs=pl.BlockSpec((tm, d), lambda i, j: (i, 0)),
        scratch_shapes=[
            pltpu.VMEM((tm + (2 * cols if rows > 1 else 0), tf), F32),
            pltpu.VMEM((tm, tf), F32),
            pltpu.VMEM((tm, tf), BF16),
        ],
        compiler_params=_params("parallel", "arbitrary"),
        name="ffn",
    )(h2, x1, mod, w_up, w_up, w_dw, b_dw, w_down, ln_g, ln_b)


def _trunk_layer(x, batch, rows, cols, mod, cond_row, state, p, pmat, alpha):
    seq = rows * cols
    heads = MLSTM_HEADS
    pool_w = N_POOL_GROUPS * LANES
    proj, gates = _inproj(x, mod, cond_row, p["w_in_main"], p["w_in_gate"])
    pool_out = _pool(proj, pmat, p["w_pool"], p["pool_scale"], rows, cols, min(FFN_TOKENS, batch * seq))

    chunk = min(MLSTM_CHUNK, seq)
    g = (gates + p["b_gates"]).reshape(batch, seq, 4, heads)
    gcol = g.transpose(0, 3, 1, 2)
    grow = g.reshape(batch, seq // chunk, chunk, 4, heads).transpose(0, 4, 1, 3, 2)
    c0, n0, m0 = state
    dh = c0.shape[-1]
    ml_out, c_new, n_new, m_new = _mlstm(
        proj, gcol, grow, p["w_qk_conv"], p["hn_g"], c0, n0.reshape(batch, 2, heads, 1, dh),
        m0.reshape(batch, 2, heads, 1, 1), batch, seq, pool_w)

    x1, h2 = _outproj(pool_out, ml_out, x, mod, cond_row, p["w_out"], p["ln1_g"], p["ln1_b"], alpha)
    x2 = _ffn(h2, x1, mod, cond_row, p["w_up"], p["w_dw"], p["b_dw"], p["w_down"], p["ln2_g"], p["ln2_b"],
              rows, cols, alpha)
    return x2, (c_new, n_new.reshape(batch, 2, heads, dh), m_new.reshape(batch, 2, heads))


def kernel(x_prompt, x_sample, c, state_C, state_n, state_m, c_ctx, w_ada, b_ada, w_in, b_gates, w_qk_conv, hn_g,
           w_pool, pool_scale, w_out, ln1_g, ln1_b, w_up, w_dw, b_dw, w_down, ln2_g, ln2_b):
    b_p, l_p, d = x_prompt.shape
    b_s, l_s, _ = x_sample.shape
    depth = w_ada.shape[0]
    heads = MLSTM_HEADS
    dh = state_C.shape[-1]
    pool_w = N_POOL_GROUPS * LANES
    n_main = pool_w + 4 * heads * dh
    d_ff = w_down.shape[1]
    alpha = (2.0 * depth) ** 0.25
    rows_s = l_s // GRID_W
    assert b_s + 1 <= COND_ROWS and l_p % POOL_TILE == 0 and POOL_TILE % GRID_W == 0

    cond = jnp.concatenate([c_ctx[None, :], c, jnp.zeros((COND_ROWS - 1 - b_s, d), F32)], axis=0)
    ada = _ada(cond, w_ada, b_ada).reshape(depth, COND_ROWS, 6, d)

    pmat_p = _pool_col_matrices(l_p)
    pmat_s = _pool_col_matrices(GRID_W)
    zero_state = (jnp.zeros((b_p, 2, heads, dh, dh), F32), jnp.zeros((b_p, 2, heads, dh), F32),
                  jnp.zeros((b_p, 2, heads), F32))

    y_p = x_prompt.reshape(b_p * l_p, d)
    y_s = x_sample.reshape(b_s * l_s, d)
    cs, ns, ms = [], [], []
    for l in range(depth):
        p = dict(
            w_in_main=w_in[l, :, :n_main].astype(BF16), w_in_gate=w_in[l, :, n_main:].astype(BF16),
            b_gates=b_gates[l], w_qk_conv=w_qk_conv[l], hn_g=hn_g[l][None, :],
            w_pool=w_pool[l].astype(BF16), pool_scale=pool_scale[l][None, :],
            w_out=w_out[l].astype(BF16), ln1_g=ln1_g[l][None, :], ln1_b=ln1_b[l][None, :],
            w_up=w_up[l].astype(BF16), w_dw=w_dw[l].reshape(9, d_ff), b_dw=b_dw[l][None, :],
            w_down=w_down[l].astype(BF16), ln2_g=ln2_g[l][None, :], ln2_b=ln2_b[l][None, :])
        y_p, (cl, nl, ml) = _trunk_layer(y_p, b_p, 1, l_p, ada[l], lambda t: 0, zero_state, p, pmat_p, alpha)
        cs.append(cl)
        ns.append(nl)
        ms.append(ml)
        y_s, _ = _trunk_layer(y_s, b_s, rows_s, GRID_W, ada[l], lambda t: 1 + t // l_s,
                              (state_C[:, l], state_n[:, l], state_m[:, l]), p, pmat_s, alpha)
    return (y_p.reshape(b_p, l_p, d), y_s.reshape(b_s, l_s, d), jnp.stack(cs, axis=1), jnp.stack(ns, axis=1),
            jnp.stack(ms, axis=1))
```

```python
import functools
import math

import numpy as np
import jax
import jax.numpy as jnp
from jax import lax
from jax.experimental import pallas as pl
from jax.experimental.pallas import tpu as pltpu

F32 = jnp.float32
BF16 = jnp.bfloat16

GRID_W = 64
POOL_WINDOWS = (2, 4, 8, 16)
N_POOL_GROUPS = len(POOL_WINDOWS)
MLSTM_HEADS = 4
LN_EPS = 1e-5

LANES = 128
SUBLANES = 8
BF16_ROWS = 16
VMEM_LIMIT_BYTES = 56 * 1024 * 1024

COND_ROWS = SUBLANES
MLSTM_CHUNK = LANES
MLSTM_STEP_BYTES = 2 * 1024 * 1024
POOL_TILE = 256
TOKEN_TILE = 512
FFN_TOKENS = 2048
FFN_BLOCK = 512
FFN_COLS = 256
LN_ROWS = 256


def _params(*sem):
    return pltpu.CompilerParams(dimension_semantics=sem, vmem_limit_bytes=VMEM_LIMIT_BYTES)


def _layer_norm(x):
    mu = jnp.mean(x, axis=-1, keepdims=True)
    xc = x - mu
    var = jnp.mean(xc * xc, axis=-1, keepdims=True)
    return xc * lax.rsqrt(var + LN_EPS)


def _dot(a, b):
    return jnp.dot(a, b, preferred_element_type=F32)


def _dot_nt(a, b):
    return lax.dot_general(a, b, (((1,), (1,)), ((), ())), preferred_element_type=F32)


def _ada_kernel(cond_ref, w_ref, b_ref, out_ref):
    cnd = cond_ref[...]
    act = (cnd * jax.nn.sigmoid(cnd)).astype(BF16)
    out_ref[...] = _dot(act, w_ref[...].astype(BF16)) + b_ref[...]


def _ada(cond, w_ada, b_ada):
    depth, d, n = w_ada.shape
    tn = 1536
    return pl.pallas_call(
        _ada_kernel,
        out_shape=jax.ShapeDtypeStruct((depth, COND_ROWS, n), F32),
        grid=(depth, n // tn),
        in_specs=[
            pl.BlockSpec((COND_ROWS, d), lambda l, j: (0, 0)),
            pl.BlockSpec((None, d, tn), lambda l, j: (l, 0, j)),
            pl.BlockSpec((None, 1, tn), lambda l, j: (l, 0, j)),
        ],
        out_specs=pl.BlockSpec((None, COND_ROWS, tn), lambda l, j: (l, 0, j)),
        compiler_params=_params("parallel", "parallel"),
        name="ada",
    )(cond, w_ada, b_ada.reshape(depth, 1, n))


def _inproj_kernel(x_ref, mod_ref, w_ref, wg_ref, proj_ref, gates_ref):
    h = _layer_norm(x_ref[...]) * (1.0 + mod_ref[1:2, :]) + mod_ref[0:1, :]
    h = h.astype(BF16)
    proj_ref[...] = _dot(h, w_ref[...])
    gates_ref[...] = _dot(h, wg_ref[...])


def _inproj(x, mod, cond_row, w_main, w_gate):
    tokens, d = x.shape
    n = w_main.shape[1]
    ng = w_gate.shape[1]
    tm = TOKEN_TILE
    return pl.pallas_call(
        _inproj_kernel,
        out_shape=(jax.ShapeDtypeStruct((tokens, n), F32), jax.ShapeDtypeStruct((tokens, ng), F32)),
        grid=(tokens // tm,),
        in_specs=[
            pl.BlockSpec((tm, d), lambda i: (i, 0)),
            pl.BlockSpec((None, 6, d), lambda i: (cond_row(i * tm), 0, 0)),
            pl.BlockSpec((d, n), lambda i: (0, 0)),
            pl.BlockSpec((d, ng), lambda i: (0, 0)),
        ],
        out_specs=(pl.BlockSpec((tm, n), lambda i: (i, 0)), pl.BlockSpec((tm, ng), lambda i: (i, 0))),
        compiler_params=_params("parallel"),
        name="inproj",
    )(x, mod, w_main, w_gate)


def _pool_col_matrices(cols):
    t = np.arange(POOL_TILE)
    row, col = t // cols, t % cols
    mats = []
    for win in POOL_WINDOWS:
        hw = win // 2
        lo = np.maximum(col - hw, 0)
        hi = np.minimum(col + hw, cols)
        same_row = row[:, None] == row[None, :]
        inside = (col[None, :] >= lo[:, None]) & (col[None, :] < hi[:, None])
        mats.append((same_row & inside).astype(np.float32))
    return jnp.asarray(np.stack(mats), dtype=BF16)


def _pool_kernel(xp_ref, pmat_ref, wp_ref, ps_ref, out_ref, pad_ref, *, rows, cols, tokens):
    gdim = LANES
    tile = POOL_TILE
    halo = (max(POOL_WINDOWS) // 2) * cols if rows > 1 else 0
    t = lax.broadcasted_iota(jnp.int32, (tile, 1), 0)
    col = t & (cols - 1)
    if rows > 1:
        zeros = jnp.zeros((halo, gdim), F32)
        pad_ref[0:halo, :] = zeros
        pad_ref[halo + tokens:halo + tokens + halo, :] = zeros
    for gi, win in enumerate(POOL_WINDOWS):
        hw = win // 2
        gs = slice(gi * gdim, (gi + 1) * gdim)
        if rows > 1:
            pad_ref[halo:halo + tokens, :] = xp_ref[:, gs]
        cnt_c = (jnp.minimum(col + hw, cols) - jnp.maximum(col - hw, 0)).astype(F32)
        pm = pmat_ref[gi]
        wp = wp_ref[gi]
        scale = ps_ref[:, gs]

        def tile_body(tt, carry):
            off = pl.multiple_of(tt * tile, tile)
            xg = xp_ref[pl.ds(off, tile), gs]
            if rows > 1:
                acc = None
                for dr in range(-hw, hw):
                    term = pad_ref[pl.ds(pl.multiple_of(off + (halo + dr * cols), SUBLANES), tile), :]
                    acc = term if acc is None else acc + term
                rw = (off + t) >> int(math.log2(cols))
                cnt_r = (jnp.minimum(rw + hw, rows) - jnp.maximum(rw - hw, 0)).astype(F32)
                m1 = acc / cnt_r
            else:
                m1 = xg
            hi = m1.astype(BF16)
            lo = (m1 - hi.astype(F32)).astype(BF16)
            m2 = (_dot(pm, hi) + _dot(pm, lo)) / cnt_c
            y = (m2 - xg).astype(BF16)
            out_ref[pl.ds(off, tile), gs] = (_dot(y, wp) * scale).astype(BF16)
            return carry

        lax.fori_loop(0, tokens // tile, tile_body, 0)


def _pool(proj, pmat, w_pool, pool_scale, rows, cols, step_tokens):
    total = proj.shape[0]
    pool_w = N_POOL_GROUPS * LANES
    halo = (max(POOL_WINDOWS) // 2) * cols if rows > 1 else 0
    kern = functools.partial(_pool_kernel, rows=rows, cols=cols, tokens=step_tokens)
    return pl.pallas_call(
        kern,
        out_shape=jax.ShapeDtypeStruct((total, pool_w), BF16),
        grid=(total // step_tokens,),
        in_specs=[
            pl.BlockSpec((step_tokens, pool_w), lambda i: (i, 0)),
            pl.BlockSpec(pmat.shape, lambda i: (0, 0, 0)),
            pl.BlockSpec(w_pool.shape, lambda i: (0, 0, 0)),
            pl.BlockSpec((1, pool_w), lambda i: (0, 0)),
        ],
        out_specs=pl.BlockSpec((step_tokens, pool_w), lambda i: (i, 0)),
        scratch_shapes=[pltpu.VMEM((step_tokens + 2 * halo if rows > 1 else SUBLANES, LANES), F32)],
        compiler_params=_params("parallel"),
        name="pool",
    )(proj, pmat, w_pool, pool_scale)


def _log_sigmoid(x):
    return jnp.minimum(x, 0.0) - jnp.log(1.0 + jnp.exp(-jnp.abs(x)))


def _split3(x):
    hi = x.astype(BF16)
    r1 = x - hi.astype(F32)
    mid = r1.astype(BF16)
    lo = (r1 - mid.astype(F32)).astype(BF16)
    return lo, mid, hi


def _mlstm_kernel(q_ref, k_ref, v_ref, o_ref, gi_ref, gf_ref, wq_ref, wk_ref, hng_ref,
                  c0_ref, n0_ref, m0_ref, out_ref, c_ref, n_ref, m_ref,
                  qs, ks, vt, ht_f, ht_b, rcb_s, br_s, ct_s, dt_s, mrow_s, z1_s, rhs_s, z2_s, *, seq, chunk, hp):
    dh = q_ref.shape[1] // hp
    nc = seq // chunk
    npair = 2 * hp
    aug = dh + BF16_ROWS
    row = lax.broadcasted_iota(jnp.int32, (seq, 1), 0)

    def conv_silu(x, w_ref):
        w = w_ref[...]
        prev = jnp.where(row == 0, 0.0, pltpu.roll(x, 1, 0))
        nxt = jnp.where(row == seq - 1, 0.0, pltpu.roll(x, seq - 1, 0))
        y = prev * w[0:1, :] + x * w[1:2, :] + nxt * w[2:3, :]
        return y * jax.nn.sigmoid(y)

    qs[...] = conv_silu(q_ref[...], wq_ref).astype(BF16)
    ks[...] = (conv_silu(k_ref[...], wk_ref) * (dh ** -0.5)).astype(BF16)

    ri = lax.broadcasted_iota(jnp.int32, (chunk, chunk), 0)
    ci = lax.broadcasted_iota(jnp.int32, (chunk, chunk), 1)
    lower = ci <= ri
    upper = ci >= ri
    upper_b = jnp.where(upper, 1.0, 0.0).astype(BF16)
    row_is_fwd = lax.broadcasted_iota(jnp.int32, (npair, chunk), 0) < hp
    ones_row = jnp.where(lax.broadcasted_iota(jnp.int32, (BF16_ROWS, chunk), 0) == 0, 1.0, 0.0).astype(BF16)

    def prep(c, carry):
        sl = pl.ds(pl.multiple_of(c * chunk, chunk), chunk)
        lf = _log_sigmoid(gf_ref[c])
        lf3 = _split3(lf)
        prefix = _dot(lf3[0], upper_b) + _dot(lf3[1], upper_b) + _dot(lf3[2], upper_b)
        suffix = prefix[:, chunk - 1:chunk] - prefix + lf
        br = jnp.where(row_is_fwd, prefix, suffix)
        br_s[c] = br
        rc = gi_ref[c] - br
        for p in range(npair):
            rcb_s[sl, p * LANES:(p + 1) * LANES] = jnp.transpose(jnp.broadcast_to(rc[p:p + 1, :], (LANES, chunk)))
        for hh in range(hp):
            vt[hh * nc + c, 0:dh, :] = jnp.transpose(v_ref[sl, hh * dh:(hh + 1) * dh]).astype(BF16)
            vt[hh * nc + c, dh:aug, :] = ones_row
        return carry

    lax.fori_loop(0, nc, prep, 0)

    n_rows = lax.broadcasted_iota(jnp.int32, (BF16_ROWS, dh), 0) == 0
    ms = []
    for p in range(npair):
        di, hh = divmod(p, hp)
        ct_s[p, 0:dh, :] = jnp.transpose(c0_ref[di, hh])
        ct_s[p, dh:aug, :] = jnp.where(n_rows, n0_ref[di, hh], 0.0)
        ms.append(m0_ref[di, hh])

    def pair_args(p, i):
        di, hh = divmod(p, hp)
        fwd = di == 0
        c = i if fwd else nc - 1 - i
        last = chunk - 1 if fwd else 0
        sl = pl.ds(pl.multiple_of(c * chunk, chunk), chunk)
        return fwd, hh, c, last, sl, slice(hh * dh, (hh + 1) * dh)

    def body(i, ms):
        for p in range(npair):
            fwd, hh, c, last, sl, hs = pair_args(p, i)
            rcb = rcb_s[sl, p * LANES:(p + 1) * LANES]
            xt = jnp.where(upper if fwd else lower, rcb, -jnp.inf)
            mrow = jnp.maximum(jnp.max(xt, axis=0, keepdims=True), ms[p])
            dt_s[p] = jnp.exp(xt - mrow)
            mrow_s[p] = jnp.broadcast_to(mrow, (SUBLANES, chunk))
            w = jnp.exp(rcb - mrow[:, last:last + 1])
            rhs_s[p, :, chunk:] = (w * ks[sl, hs].astype(F32)).astype(BF16)
        for p in range(npair):
            fwd, hh, c, last, sl, hs = pair_args(p, i)
            z1_s[p] = _dot_nt(jnp.concatenate([ks[sl, hs], ct_s[p].astype(BF16)], axis=0), qs[sl, hs])
        for p in range(npair):
            rhs_s[p, :, 0:chunk] = (z1_s[p, 0:chunk, :] * dt_s[p]).astype(BF16)
        for p in range(npair):
            fwd, hh, c, last, sl, hs = pair_args(p, i)
            z2_s[p] = _dot(vt[hh * nc + c], rhs_s[p])
        new_ms = []
        for p in range(npair):
            fwd, hh, c, last, sl, hs = pair_args(p, i)
            m = ms[p]
            mrow = mrow_s[p, 0:1, :]
            num = z1_s[p, chunk:, :] * jnp.exp(m - mrow) + z2_s[p, :, 0:chunk]
            b_r = br_s[c, p:p + 1, :]
            scale = 1.0 / jnp.maximum(jnp.abs(num[dh:dh + 1, :]), jnp.exp(-(b_r + mrow)))
            h_t = num[0:dh, :] * scale
            if fwd:
                ht_f[hh * nc + c] = h_t
            else:
                ht_b[hh * nc + c] = h_t
            mlast = mrow[:, last:last + 1]
            ct_s[p] = jnp.exp(m - mlast) * ct_s[p] + z2_s[p, :, chunk:]
            new_ms.append(b_r[:, last:last + 1] + mlast)
        return tuple(new_ms)

    ms = lax.fori_loop(0, nc, body, tuple(ms))

    def finish(c, carry):
        sl = pl.ds(pl.multiple_of(c * chunk, chunk), chunk)
        for hh in range(hp):
            hs = slice(hh * dh, (hh + 1) * dh)
            h_t = ht_f[hh * nc + c] + ht_b[hh * nc + c]
            mu = jnp.mean(h_t, axis=0, keepdims=True)
            hc = h_t - mu
            var = jnp.mean(hc * hc, axis=0, keepdims=True)
            hn = jnp.transpose(hc * lax.rsqrt(var + LN_EPS)) * hng_ref[:, hs]
            out_ref[sl, hs] = (jax.nn.sigmoid(o_ref[sl, hs]) * hn).astype(BF16)
        return carry

    lax.fori_loop(0, nc, finish, 0)

    for p in range(npair):
        di, hh = divmod(p, hp)
        c_ref[di, hh] = jnp.transpose(ct_s[p, 0:dh, :])
        n_ref[di, hh] = ct_s[p, dh:dh + 1, :]
        m_ref[di, hh] = ms[p]


def _mlstm(proj, gates, w_qk_conv, hn_g, c0, n0, m0, batch, seq, col0):
    heads = MLSTM_HEADS
    dh = c0.shape[-1]
    chunk = MLSTM_CHUNK
    assert chunk == LANES and seq % chunk == 0
    nc = seq // chunk
    hp = max(1, min(heads, MLSTM_STEP_BYTES // (seq * dh * 4)))
    hg = heads // hp
    npair = 2 * hp
    aug = dh + BF16_ROWS
    width = hp * dh
    cb = col0 // width
    gb = heads * dh // width
    kern = functools.partial(_mlstm_kernel, seq=seq, chunk=chunk, hp=hp)

    def pair_rows(g2):
        return g2.reshape(batch, nc, chunk, 2, hg, hp).transpose(0, 4, 1, 3, 5, 2).reshape(batch, hg, nc, npair, chunk)

    gi = pair_rows(gates[:, :, 0::2, :])
    gf = pair_rows(gates[:, :, 1::2, :])

    def tok_spec(group):
        return pl.BlockSpec((seq, width), lambda b, g: (b, cb + group * gb + g))

    row_spec = pl.BlockSpec((None, None, nc, npair, chunk), lambda b, g: (b, g, 0, 0, 0))
    st_c = pl.BlockSpec((None, 2, hp, dh, dh), lambda b, g: (b, 0, g, 0, 0))
    st_n = pl.BlockSpec((None, 2, hp, 1, dh), lambda b, g: (b, 0, g, 0, 0))
    st_m = pl.BlockSpec((None, 2, hp, 1, 1), lambda b, g: (b, 0, g, 0, 0))
    out, c_new, n_new, m_new = pl.pallas_call(
        kern,
        out_shape=(
            jax.ShapeDtypeStruct((batch * seq, heads * dh), BF16),
            jax.ShapeDtypeStruct((batch, 2, heads, dh, dh), F32),
            jax.ShapeDtypeStruct((batch, 2, heads, 1, dh), F32),
            jax.ShapeDtypeStruct((batch, 2, heads, 1, 1), F32),
        ),
        grid=(batch, hg),
        in_specs=[
            tok_spec(0), tok_spec(1), tok_spec(2), tok_spec(3),
            row_spec, row_spec,
            pl.BlockSpec((3, width), lambda b, g: (0, g)),
            pl.BlockSpec((3, width), lambda b, g: (0, gb + g)),
            pl.BlockSpec((1, width), lambda b, g: (0, g)),
            st_c, st_n, st_m,
        ],
        out_specs=(pl.BlockSpec((seq, width), lambda b, g: (b, g)), st_c, st_n, st_m),
        scratch_shapes=[
            pltpu.VMEM((seq, width), BF16), pltpu.VMEM((seq, width), BF16),
            pltpu.VMEM((hp * nc, aug, chunk), BF16),
            pltpu.VMEM((hp * nc, dh, chunk), F32), pltpu.VMEM((hp * nc, dh, chunk), F32),
            pltpu.VMEM((seq, npair * LANES), F32), pltpu.VMEM((nc, npair, chunk), F32),
            pltpu.VMEM((npair, aug, dh), F32),
            pltpu.VMEM((npair, chunk, chunk), F32), pltpu.VMEM((npair, SUBLANES, chunk), F32),
            pltpu.VMEM((npair, chunk + aug, chunk), F32), pltpu.VMEM((npair, chunk, chunk + dh), BF16),
            pltpu.VMEM((npair, aug, chunk + dh), F32),
        ],
        compiler_params=_params("parallel", "parallel"),
        name="mlstm",
    )(proj, proj, proj, proj, gi, gf, w_qk_conv, w_qk_conv, hn_g, c0, n0, m0)
    return out, c_new, n_new, m_new


def _outproj_kernel(pool_ref, ml_ref, x_ref, mod_ref, wp_ref, wm_ref, g_ref, b_ref, x1_ref, h2_ref, *, alpha):
    mix = _dot(pool_ref[...], wp_ref[...]) + _dot(ml_ref[...], wm_ref[...])
    x1 = _layer_norm(alpha * x_ref[...] + mod_ref[2:3, :] * mix) * g_ref[...] + b_ref[...]
    x1_ref[...] = x1
    h2 = _layer_norm(x1) * (1.0 + mod_ref[4:5, :]) + mod_ref[3:4, :]
    h2_ref[...] = h2.astype(BF16)


def _outproj(pool_out, ml_out, x, mod, cond_row, w_out, ln_g, ln_b, alpha):
    tokens, d = x.shape
    half = pool_out.shape[1]
    tm = TOKEN_TILE
    kern = functools.partial(_outproj_kernel, alpha=alpha)
    return pl.pallas_call(
        kern,
        out_shape=(jax.ShapeDtypeStruct((tokens, d), F32), jax.ShapeDtypeStruct((tokens, d), BF16)),
        grid=(tokens // tm,),
        in_specs=[
            pl.BlockSpec((tm, half), lambda i: (i, 0)),
            pl.BlockSpec((tm, half), lambda i: (i, 0)),
            pl.BlockSpec((tm, d), lambda i: (i, 0)),
            pl.BlockSpec((None, 6, d), lambda i: (cond_row(i * tm), 0, 0)),
            pl.BlockSpec((half, d), lambda i: (0, 0)),
            pl.BlockSpec((half, d), lambda i: (1, 0)),
            pl.BlockSpec((1, d), lambda i: (0, 0)),
            pl.BlockSpec((1, d), lambda i: (0, 0)),
        ],
        out_specs=(pl.BlockSpec((tm, d), lambda i: (i, 0)), pl.BlockSpec((tm, d), lambda i: (i, 0))),
        compiler_params=_params("parallel"),
        name="outproj",
    )(pool_out, ml_out, x, mod, w_out, w_out, ln_g, ln_b)


def _gelu_tanh(x):
    return 0.5 * x * (1.0 + jnp.tanh(math.sqrt(2.0 / math.pi) * (x + 0.044715 * (x * x * x))))


def _ffn_kernel(h_ref, x1_ref, mod_ref, wa_ref, wg_ref, wdw_ref, bdw_ref, wd_ref, lng_ref, lnb_ref, out_ref,
                gpad, a_s, u_s, *, rows, cols, tokens, alpha):
    j = pl.program_id(1)
    tf = wa_ref.shape[1]
    blk = FFN_BLOCK
    nb = tokens // blk
    base = cols if rows > 1 else 0

    @pl.when(j == 0)
    def _():
        out_ref[...] = jnp.zeros_like(out_ref)

    if rows > 1:
        zeros = jnp.zeros((cols, tf), F32)
        gpad[0:cols, :] = zeros
        gpad[cols + tokens:cols + tokens + cols, :] = zeros

    wdw = wdw_ref[...]
    bdw = bdw_ref[...]
    ci = lax.broadcasted_iota(jnp.int32, (cols, 1), 0)
    first = ci == 0
    last = ci == cols - 1
    taps_y = (-1, 0, 1) if rows > 1 else (0,)

    def up(b):
        hb = h_ref[b * blk:(b + 1) * blk, :]
        a_s[b * blk:(b + 1) * blk, :] = _dot(hb, wa_ref[...])
        gpad[base + b * blk:base + (b + 1) * blk, :] = _dot(hb, wg_ref[...])

    def conv_gate(b):
        for r in range(b * blk // cols, (b + 1) * blk // cols):
            off = r * cols
            for lb in range(tf // LANES):
                ls = slice(lb * LANES, (lb + 1) * LANES)
                left = mid = right = None
                for dy in taps_y:
                    xr = gpad[base + off + dy * cols:base + off + (dy + 1) * cols, ls]
                    kk = (dy + 1) * 3
                    tl, tc, tr = xr * wdw[kk:kk + 1, ls], xr * wdw[kk + 1:kk + 2, ls], xr * wdw[kk + 2:kk + 3, ls]
                    left = tl if left is None else left + tl
                    mid = tc if mid is None else mid + tc
                    right = tr if right is None else right + tr
                y = (mid + jnp.where(first, 0.0, pltpu.roll(left, 1, 0))
                     + jnp.where(last, 0.0, pltpu.roll(right, cols - 1, 0)) + bdw[:, ls])
                u_s[off:off + cols, ls] = (_gelu_tanh(y) * a_s[off:off + cols, ls]).astype(BF16)

    def down(b):
        out_ref[b * blk:(b + 1) * blk, :] += _dot(u_s[b * blk:(b + 1) * blk, :], wd_ref[...])

    for step in range(nb + 2):
        if step < nb:
            up(step)
        if 1 <= step <= nb:
            conv_gate(step - 1)
        if step >= 2:
            down(step - 2)

    @pl.when(j == pl.num_programs(1) - 1)
    def _():
        gate = mod_ref[5:6, :]
        lng = lng_ref[...]
        lnb = lnb_ref[...]

        def ln_body(r, carry):
            sl = pl.ds(pl.multiple_of(r * LN_ROWS, LN_ROWS), LN_ROWS)
            z = alpha * x1_ref[sl, :] + gate * out_ref[sl, :]
            out_ref[sl, :] = _layer_norm(z) * lng + lnb
            return carry

        lax.fori_loop(0, tokens // LN_ROWS, ln_body, 0)


def _ffn(h2, x1, mod, cond_row, w_up, w_dw, b_dw, w_down, ln_g, ln_b, rows, cols, alpha):
    total, d = x1.shape
    d_ff = w_down.shape[0]
    tm = FFN_TOKENS
    tf = FFN_COLS
    nj = d_ff // tf
    kern = functools.partial(_ffn_kernel, rows=rows, cols=cols, tokens=tm, alpha=alpha)
    return pl.pallas_call(
        kern,
        out_shape=jax.ShapeDtypeStruct((total, d), F32),
        grid=(total // tm, nj),
        in_specs=[
            pl.BlockSpec((tm, d), lambda i, j: (i, 0)),
            pl.BlockSpec((tm, d), lambda i, j: (i, 0)),
            pl.BlockSpec((None, 6, d), lambda i, j: (cond_row(i * tm), 0, 0)),
            pl.BlockSpec((d, tf), lambda i, j: (0, j)),
            pl.BlockSpec((d, tf), lambda i, j: (0, nj + j)),
            pl.BlockSpec((9, tf), lambda i, j: (0, j)),
            pl.BlockSpec((1, tf), lambda i, j: (0, j)),
            pl.BlockSpec((tf, d), lambda i, j: (j, 0)),
            pl.BlockSpec((1, d), lambda i, j: (0, 0)),
            pl.BlockSpec((1, d), lambda i, j: (0, 0)),
        ],
        out_specs=pl.BlockSpec((tm, d), lambda i, j: (i, 0)),
        scratch_shapes=[
            pltpu.VMEM((tm + (2 * cols if rows > 1 else 0), tf), F32),
            pltpu.VMEM((tm, tf), F32),
            pltpu.VMEM((tm, tf), BF16),
        ],
        compiler_params=_params("parallel", "arbitrary"),
        name="ffn",
    )(h2, x1, mod, w_up, w_up, w_dw, b_dw, w_down, ln_g, ln_b)


def _trunk_layer(x, batch, rows, cols, mod, cond_row, state, p, pmat, alpha):
    seq = rows * cols
    heads = MLSTM_HEADS
    pool_w = N_POOL_GROUPS * LANES
    proj, gates = _inproj(x, mod, cond_row, p["w_in_main"], p["w_in_gate"])
    pool_out = _pool(proj, pmat, p["w_pool"], p["pool_scale"], rows, cols, min(FFN_TOKENS, batch * seq))

    c0, n0, m0 = state
    dh = c0.shape[-1]
    ml_out, c_new, n_new, m_new = _mlstm(
        proj, (gates + p["b_gates"]).reshape(batch, seq, 4, heads), p["w_qk_conv"], p["hn_g"], c0,
        n0.reshape(batch, 2, heads, 1, dh), m0.reshape(batch, 2, heads, 1, 1), batch, seq, pool_w)

    x1, h2 = _outproj(pool_out, ml_out, x, mod, cond_row, p["w_out"], p["ln1_g"], p["ln1_b"], alpha)
    x2 = _ffn(h2, x1, mod, cond_row, p["w_up"], p["w_dw"], p["b_dw"], p["w_down"], p["ln2_g"], p["ln2_b"],
              rows, cols, alpha)
    return x2, (c_new, n_new.reshape(batch, 2, heads, dh), m_new.reshape(batch, 2, heads))


def kernel(x_prompt, x_sample, c, state_C, state_n, state_m, c_ctx, w_ada, b_ada, w_in, b_gates, w_qk_conv, hn_g,
           w_pool, pool_scale, w_out, ln1_g, ln1_b, w_up, w_dw, b_dw, w_down, ln2_g, ln2_b):
    b_p, l_p, d = x_prompt.shape
    b_s, l_s, _ = x_sample.shape
    depth = w_ada.shape[0]
    heads = MLSTM_HEADS
    dh = state_C.shape[-1]
    pool_w = N_POOL_GROUPS * LANES
    n_main = pool_w + 4 * heads * dh
    d_ff = w_down.shape[1]
    alpha = (2.0 * depth) ** 0.25
    rows_s = l_s // GRID_W
    assert b_s + 1 <= COND_ROWS and l_p % POOL_TILE == 0 and POOL_TILE % GRID_W == 0

    cond = jnp.concatenate([c_ctx[None, :], c, jnp.zeros((COND_ROWS - 1 - b_s, d), F32)], axis=0)
    ada = _ada(cond, w_ada, b_ada).reshape(depth, COND_ROWS, 6, d)

    pmat_p = _pool_col_matrices(l_p)
    pmat_s = _pool_col_matrices(GRID_W)
    zero_state = (jnp.zeros((b_p, 2, heads, dh, dh), F32), jnp.zeros((b_p, 2, heads, dh), F32),
                  jnp.zeros((b_p, 2, heads), F32))

    y_p = x_prompt.reshape(b_p * l_p, d)
    y_s = x_sample.reshape(b_s * l_s, d)
    cs, ns, ms = [], [], []
    for l in range(depth):
        p = dict(
            w_in_main=w_in[l, :, :n_main].astype(BF16), w_in_gate=w_in[l, :, n_main:].astype(BF16),
            b_gates=b_gates[l], w_qk_conv=w_qk_conv[l], hn_g=hn_g[l][None, :],
            w_pool=w_pool[l].astype(BF16), pool_scale=pool_scale[l][None, :],
            w_out=w_out[l].astype(BF16), ln1_g=ln1_g[l][None, :], ln1_b=ln1_b[l][None, :],
            w_up=w_up[l].astype(BF16), w_dw=w_dw[l].reshape(9, d_ff), b_dw=b_dw[l][None, :],
            w_down=w_down[l].astype(BF16), ln2_g=ln2_g[l][None, :], ln2_b=ln2_b[l][None, :])
        y_p, (cl, nl, ml) = _trunk_layer(y_p, b_p, 1, l_p, ada[l], lambda t: 0, zero_state, p, pmat_p, alpha)
        cs.append(cl)
        ns.append(nl)
        ms.append(ml)
        y_s, _ = _trunk_layer(y_s, b_s, rows_s, GRID_W, ada[l], lambda t: 1 + t // l_s,
                              (state_C[:, l], state_n[:, l], state_m[:, l]), p, pmat_s, alpha)
    return (y_p.reshape(b_p, l_p, d), y_s.reshape(b_s, l_s, d), jnp.stack(cs, axis=1), jnp.stack(ns, axis=1),
            jnp.stack(ms, axis=1))
```

```python
import functools
import math

import numpy as np
import jax
import jax.numpy as jnp
from jax import lax
from jax.experimental import pallas as pl
from jax.experimental.pallas import tpu as pltpu

F32 = jnp.float32
BF16 = jnp.bfloat16

GRID_W = 64
POOL_WINDOWS = (2, 4, 8, 16)
N_POOL_GROUPS = len(POOL_WINDOWS)
MLSTM_HEADS = 4
LN_EPS = 1e-5

LANES = 128
SUBLANES = 8
BF16_ROWS = 16
VMEM_LIMIT_BYTES = 56 * 1024 * 1024

COND_ROWS = SUBLANES
MLSTM_CHUNK = LANES
MLSTM_STEP_BYTES = 2 * 1024 * 1024
POOL_TILE = 256
TOKEN_TILE = 512
FFN_TOKENS = 2048
FFN_BLOCK = 512
FFN_COLS = 256
LN_ROWS = 256


def _params(*sem):
    return pltpu.CompilerParams(dimension_semantics=sem, vmem_limit_bytes=VMEM_LIMIT_BYTES)


def _layer_norm(x):
    mu = jnp.mean(x, axis=-1, keepdims=True)
    xc = x - mu
    var = jnp.mean(xc * xc, axis=-1, keepdims=True)
    return xc * lax.rsqrt(var + LN_EPS)


def _dot(a, b):
    return jnp.dot(a, b, preferred_element_type=F32)


def _dot_nt(a, b):
    return lax.dot_general(a, b, (((1,), (1,)), ((), ())), preferred_element_type=F32)


def _ada_kernel(cond_ref, w_ref, b_ref, out_ref):
    cnd = cond_ref[...]
    act = (cnd * jax.nn.sigmoid(cnd)).astype(BF16)
    out_ref[...] = _dot(act, w_ref[...].astype(BF16)) + b_ref[...]


def _ada(cond, w_ada, b_ada):
    depth, d, n = w_ada.shape
    tn = 1536
    return pl.pallas_call(
        _ada_kernel,
        out_shape=jax.ShapeDtypeStruct((depth, COND_ROWS, n), F32),
        grid=(depth, n // tn),
        in_specs=[
            pl.BlockSpec((COND_ROWS, d), lambda l, j: (0, 0)),
            pl.BlockSpec((None, d, tn), lambda l, j: (l, 0, j)),
            pl.BlockSpec((None, 1, tn), lambda l, j: (l, 0, j)),
        ],
        out_specs=pl.BlockSpec((None, COND_ROWS, tn), lambda l, j: (l, 0, j)),
        compiler_params=_params("parallel", "parallel"),
        name="ada",
    )(cond, w_ada, b_ada.reshape(depth, 1, n))


def _inproj_kernel(x_ref, mod_ref, w_ref, wgt_ref, bg_ref, proj_ref, gates_ref):
    h = _layer_norm(x_ref[...]) * (1.0 + mod_ref[1:2, :]) + mod_ref[0:1, :]
    h = h.astype(BF16)
    proj_ref[...] = _dot(h, w_ref[...])
    gates_t = _dot_nt(wgt_ref[...], h) + bg_ref[...]
    for k in range(gates_ref.shape[0]):
        gates_ref[k] = gates_t[:, k * LANES:(k + 1) * LANES]


def _inproj(x, mod, cond_row, w_main, w_gate_t, b_gate):
    tokens, d = x.shape
    n = w_main.shape[1]
    ng = w_gate_t.shape[0]
    tm = TOKEN_TILE
    return pl.pallas_call(
        _inproj_kernel,
        out_shape=(jax.ShapeDtypeStruct((tokens, n), F32), jax.ShapeDtypeStruct((tokens // LANES, ng, LANES), F32)),
        grid=(tokens // tm,),
        in_specs=[
            pl.BlockSpec((tm, d), lambda i: (i, 0)),
            pl.BlockSpec((None, 6, d), lambda i: (cond_row(i * tm), 0, 0)),
            pl.BlockSpec((d, n), lambda i: (0, 0)),
            pl.BlockSpec((ng, d), lambda i: (0, 0)),
            pl.BlockSpec((ng, 1), lambda i: (0, 0)),
        ],
        out_specs=(pl.BlockSpec((tm, n), lambda i: (i, 0)), pl.BlockSpec((tm // LANES, ng, LANES), lambda i: (i, 0, 0))),
        compiler_params=_params("parallel"),
        name="inproj",
    )(x, mod, w_main, w_gate_t, b_gate)


def _pool_col_matrices(cols):
    t = np.arange(POOL_TILE)
    row, col = t // cols, t % cols
    mats = []
    for win in POOL_WINDOWS:
        hw = win // 2
        lo = np.maximum(col - hw, 0)
        hi = np.minimum(col + hw, cols)
        same_row = row[:, None] == row[None, :]
        inside = (col[None, :] >= lo[:, None]) & (col[None, :] < hi[:, None])
        mats.append((same_row & inside).astype(np.float32))
    return jnp.asarray(np.stack(mats), dtype=BF16)


def _pool_kernel(xp_ref, pmat_ref, wp_ref, ps_ref, out_ref, pad_ref, *, rows, cols, tokens):
    gdim = LANES
    tile = POOL_TILE
    halo = (max(POOL_WINDOWS) // 2) * cols if rows > 1 else 0
    t = lax.broadcasted_iota(jnp.int32, (tile, 1), 0)
    col = t & (cols - 1)
    if rows > 1:
        zeros = jnp.zeros((halo, gdim), F32)
        pad_ref[0:halo, :] = zeros
        pad_ref[halo + tokens:halo + tokens + halo, :] = zeros
    for gi, win in enumerate(POOL_WINDOWS):
        hw = win // 2
        gs = slice(gi * gdim, (gi + 1) * gdim)
        if rows > 1:
            pad_ref[halo:halo + tokens, :] = xp_ref[:, gs]
        cnt_c = (jnp.minimum(col + hw, cols) - jnp.maximum(col - hw, 0)).astype(F32)
        pm = pmat_ref[gi]
        wp = wp_ref[gi]
        scale = ps_ref[:, gs]

        for tt in range(tokens // tile):
            off = tt * tile
            xg = xp_ref[off:off + tile, gs]
            if rows > 1:
                acc = None
                for dr in range(-hw, hw):
                    term = pad_ref[off + halo + dr * cols:off + halo + dr * cols + tile, :]
                    acc = term if acc is None else acc + term
                rw = (off + t) >> int(math.log2(cols))
                cnt_r = (jnp.minimum(rw + hw, rows) - jnp.maximum(rw - hw, 0)).astype(F32)
                m1 = acc / cnt_r
            else:
                m1 = xg
            hi = m1.astype(BF16)
            lo = (m1 - hi.astype(F32)).astype(BF16)
            m2 = (_dot(pm, hi) + _dot(pm, lo)) / cnt_c
            y = (m2 - xg).astype(BF16)
            out_ref[off:off + tile, gs] = (_dot(y, wp) * scale).astype(BF16)


def _pool(proj, pmat, w_pool, pool_scale, rows, cols, step_tokens):
    total = proj.shape[0]
    pool_w = N_POOL_GROUPS * LANES
    halo = (max(POOL_WINDOWS) // 2) * cols if rows > 1 else 0
    kern = functools.partial(_pool_kernel, rows=rows, cols=cols, tokens=step_tokens)
    return pl.pallas_call(
        kern,
        out_shape=jax.ShapeDtypeStruct((total, pool_w), BF16),
        grid=(total // step_tokens,),
        in_specs=[
            pl.BlockSpec((step_tokens, pool_w), lambda i: (i, 0)),
            pl.BlockSpec(pmat.shape, lambda i: (0, 0, 0)),
            pl.BlockSpec(w_pool.shape, lambda i: (0, 0, 0)),
            pl.BlockSpec((1, pool_w), lambda i: (0, 0)),
        ],
        out_specs=pl.BlockSpec((step_tokens, pool_w), lambda i: (i, 0)),
        scratch_shapes=[pltpu.VMEM((step_tokens + 2 * halo if rows > 1 else SUBLANES, LANES), F32)],
        compiler_params=_params("parallel"),
        name="pool",
    )(proj, pmat, w_pool, pool_scale)


def _log_sigmoid(x):
    return jnp.minimum(x, 0.0) - jnp.log(1.0 + jnp.exp(-jnp.abs(x)))


def _split3(x):
    hi = x.astype(BF16)
    r1 = x - hi.astype(F32)
    mid = r1.astype(BF16)
    lo = (r1 - mid.astype(F32)).astype(BF16)
    return lo, mid, hi


def _mlstm_kernel(*refs, seq, chunk, hp, zero_init, emit_state, n_unused, state_slot):
    refs = list(refs)
    q_ref, k_ref, v_ref, o_ref, g_ref, wq_ref, wk_ref, hng_ref = refs[:8]
    pos = 8
    if not zero_init:
        c0_ref, n0_ref, m0_ref = refs[pos:pos + 3]
        pos += 3
    pos += n_unused
    out_ref = refs[pos]
    pos += 1
    if emit_state:
        c_ref, n_ref, m_ref = refs[pos:pos + 3]
        pos += 3
    qpad, kpad, qs, ks, vt, ht_f, ht_b, rcb_s, br_s, rc_s, ct_s, dt_s, mrow_s, z1_s, rhs_s, z2_s = refs[pos:]
    dh = q_ref.shape[1] // hp
    nc = seq // chunk
    npair = 2 * hp
    aug = dh + BF16_ROWS
    margin = SUBLANES

    for pad_ref, src_ref in ((qpad, q_ref), (kpad, k_ref)):
        pad_ref[0:margin, :] = jnp.zeros((margin, hp * dh), F32)
        pad_ref[margin + seq:2 * margin + seq, :] = jnp.zeros((margin, hp * dh), F32)
        pad_ref[margin:margin + seq, :] = src_ref[...]

    def conv_silu(pad_ref, w_ref, c, hs):
        win = pad_ref[pl.ds(pl.multiple_of(c * chunk, chunk), chunk + 2 * margin), hs]
        w = w_ref[:, hs]
        y = (pltpu.roll(win, 1, 0) * w[0:1, :] + win * w[1:2, :]
             + pltpu.roll(win, chunk + 2 * margin - 1, 0) * w[2:3, :])[margin:margin + chunk]
        return y * jax.nn.sigmoid(y)

    ri = lax.broadcasted_iota(jnp.int32, (chunk, chunk), 0)
    ci = lax.broadcasted_iota(jnp.int32, (chunk, chunk), 1)
    lower = ci <= ri
    upper = ci >= ri
    upper_b = jnp.where(upper, 1.0, 0.0).astype(BF16)
    row_is_fwd = lax.broadcasted_iota(jnp.int32, (npair, chunk), 0) < hp
    ones_row = jnp.where(lax.broadcasted_iota(jnp.int32, (BF16_ROWS, chunk), 0) == 0, 1.0, 0.0).astype(BF16)

    gates = g_ref[...]
    lf = _log_sigmoid(gates[:, npair:, :])
    lf3 = _split3(lf.reshape(nc * npair, chunk))
    prefix = (_dot(lf3[0], upper_b) + _dot(lf3[1], upper_b) + _dot(lf3[2], upper_b)).reshape(nc, npair, chunk)
    suffix = prefix[:, :, chunk - 1:chunk] - prefix + lf
    br = jnp.where(row_is_fwd, prefix, suffix)
    br_s[...] = br
    rc_s[...] = gates[:, :npair, :] - br

    def prep(c, carry):
        sl = pl.ds(pl.multiple_of(c * chunk, chunk), chunk)
        rc = rc_s[c]
        for p in range(npair):
            rcb_s[sl, p * LANES:(p + 1) * LANES] = jnp.transpose(jnp.broadcast_to(rc[p:p + 1, :], (LANES, chunk)))
        for hh in range(hp):
            hs = slice(hh * dh, (hh + 1) * dh)
            qs[sl, hs] = conv_silu(qpad, wq_ref, c, hs).astype(BF16)
            ks[sl, hs] = (conv_silu(kpad, wk_ref, c, hs) * (dh ** -0.5)).astype(BF16)
            vt[hh * nc + c, 0:dh, :] = jnp.transpose(v_ref[sl, hs]).astype(BF16)
            vt[hh * nc + c, dh:aug, :] = ones_row
        return carry

    lax.fori_loop(0, nc, prep, 0)

    n_rows = lax.broadcasted_iota(jnp.int32, (BF16_ROWS, dh), 0) == 0
    ms = []
    for p in range(npair):
        di, hh = divmod(p, hp)
        if zero_init:
            ct_s[p] = jnp.zeros((aug, dh), F32)
            ms.append(jnp.zeros((1, 1), F32))
        else:
            ct_s[p, 0:dh, :] = jnp.transpose(c0_ref[di, hh])
            ct_s[p, dh:aug, :] = jnp.where(n_rows, n0_ref[di, hh], 0.0)
            ms.append(m0_ref[di, hh])

    def pair_args(p, i):
        di, hh = divmod(p, hp)
        fwd = di == 0
        c = i if fwd else nc - 1 - i
        last = chunk - 1 if fwd else 0
        sl = pl.ds(pl.multiple_of(c * chunk, chunk), chunk)
        return fwd, hh, c, last, sl, slice(hh * dh, (hh + 1) * dh)

    def body(i, ms):
        for p in range(npair):
            fwd, hh, c, last, sl, hs = pair_args(p, i)
            rcb = rcb_s[sl, p * LANES:(p + 1) * LANES]
            xt = jnp.where(upper if fwd else lower, rcb, -jnp.inf)
            mrow = jnp.maximum(jnp.max(xt, axis=0, keepdims=True), ms[p])
            dt_s[p] = jnp.exp(xt - mrow)
            mrow_s[p] = jnp.broadcast_to(mrow, (SUBLANES, chunk))
            w = jnp.exp(rcb - mrow[:, last:last + 1])
            rhs_s[p, :, chunk:] = (w * ks[sl, hs].astype(F32)).astype(BF16)
        for p in range(npair):
            fwd, hh, c, last, sl, hs = pair_args(p, i)
            z1_s[p] = _dot_nt(jnp.concatenate([ks[sl, hs], ct_s[p].astype(BF16)], axis=0), qs[sl, hs])
        for p in range(npair):
            rhs_s[p, :, 0:chunk] = (z1_s[p, 0:chunk, :] * dt_s[p]).astype(BF16)
        for p in range(npair):
            fwd, hh, c, last, sl, hs = pair_args(p, i)
            z2_s[p] = _dot(vt[hh * nc + c], rhs_s[p])
        new_ms = []
        for p in range(npair):
            fwd, hh, c, last, sl, hs = pair_args(p, i)
            m = ms[p]
            mrow = mrow_s[p, 0:1, :]
            num = z1_s[p, chunk:, :] * jnp.exp(m - mrow) + z2_s[p, :, 0:chunk]
            b_r = br_s[c, p:p + 1, :]
            scale = 1.0 / jnp.maximum(jnp.abs(num[dh:dh + 1, :]), jnp.exp(-(b_r + mrow)))
            h_t = num[0:dh, :] * scale
            if fwd:
                ht_f[hh * nc + c] = h_t
            else:
                ht_b[hh * nc + c] = h_t
            mlast = mrow[:, last:last + 1]
            ct_s[p] = jnp.exp(m - mlast) * ct_s[p] + z2_s[p, :, chunk:]
            new_ms.append(b_r[:, last:last + 1] + mlast)
        return tuple(new_ms)

    ms = lax.fori_loop(0, nc, body, tuple(ms))

    def finish(c, carry):
        sl = pl.ds(pl.multiple_of(c * chunk, chunk), chunk)
        for hh in range(hp):
            hs = slice(hh * dh, (hh + 1) * dh)
            h_t = ht_f[hh * nc + c] + ht_b[hh * nc + c]
            mu = jnp.mean(h_t, axis=0, keepdims=True)
            hc = h_t - mu
            var = jnp.mean(hc * hc, axis=0, keepdims=True)
            hn = jnp.transpose(hc * lax.rsqrt(var + LN_EPS)) * hng_ref[:, hs]
            out_ref[sl, hs] = (jax.nn.sigmoid(o_ref[sl, hs]) * hn).astype(BF16)
        return carry

    lax.fori_loop(0, nc, finish, 0)

    if emit_state:
        for slot in range(c_ref.shape[0]):
            if slot != state_slot:
                c_ref[slot] = jnp.zeros(c_ref.shape[1:], F32)
                n_ref[slot] = jnp.zeros(n_ref.shape[1:], F32)
                m_ref[slot] = jnp.zeros(m_ref.shape[1:], F32)
        for p in range(npair):
            di, hh = divmod(p, hp)
            c_ref[state_slot, di, hh] = jnp.transpose(ct_s[p, 0:dh, :])
            n_ref[state_slot, di, hh] = ct_s[p, dh:dh + 1, :]
            m_ref[state_slot, di, hh] = ms[p]


def _mlstm_heads_per_step(seq, dh):
    return max(1, min(MLSTM_HEADS, MLSTM_STEP_BYTES // (seq * dh * 4)))


def _gate_row_order(hp):
    heads = MLSTM_HEADS
    order = []
    for g in range(heads // hp):
        for kind in range(2):
            for di in range(2):
                for hh in range(hp):
                    order.append((di * 2 + kind) * heads + g * hp + hh)
    return np.asarray(order)


def _mlstm(proj, gates_t, w_qk_conv, hn_g, batch, seq, col0, dh, init_state=None, layer=None, depth=None, prev=None):
    heads = MLSTM_HEADS
    chunk = MLSTM_CHUNK
    assert chunk == LANES and seq % chunk == 0
    nc = seq // chunk
    hp = _mlstm_heads_per_step(seq, dh)
    hg = heads // hp
    npair = 2 * hp
    aug = dh + BF16_ROWS
    width = hp * dh
    cb = col0 // width
    gb = heads * dh // width
    zero_init = init_state is None
    emit_state = layer is not None
    prev = () if prev is None else tuple(prev)
    slots, first_slot = (1, layer) if prev else (depth, 0)
    kern = functools.partial(_mlstm_kernel, seq=seq, chunk=chunk, hp=hp, zero_init=zero_init, emit_state=emit_state,
                             n_unused=len(prev), state_slot=(layer - first_slot) if emit_state else None)

    def tok_spec(group):
        return pl.BlockSpec((seq, width), lambda b, g: (b, cb + group * gb + g))

    in_specs = [
        tok_spec(0), tok_spec(1), tok_spec(2), tok_spec(3),
        pl.BlockSpec((nc, 2 * npair, chunk), lambda b, g: (b, g, 0)),
        pl.BlockSpec((3, width), lambda b, g: (0, g)),
        pl.BlockSpec((3, width), lambda b, g: (0, gb + g)),
        pl.BlockSpec((1, width), lambda b, g: (0, g)),
    ]
    args = [proj, proj, proj, proj, gates_t, w_qk_conv, w_qk_conv, hn_g]
    if not zero_init:
        in_specs += [pl.BlockSpec((None, 2, hp, dh, dh), lambda b, g: (b, 0, g, 0, 0)),
                     pl.BlockSpec((None, 2, hp, 1, dh), lambda b, g: (b, 0, g, 0, 0)),
                     pl.BlockSpec((None, 2, hp, 1, 1), lambda b, g: (b, 0, g, 0, 0))]
        args += list(init_state)
    aliases = {}
    for k, arr in enumerate(prev):
        aliases[len(args)] = 1 + k
        in_specs.append(pl.BlockSpec(memory_space=pl.ANY))
        args.append(arr)
    out_shape = [jax.ShapeDtypeStruct((batch * seq, heads * dh), BF16)]
    out_specs = [pl.BlockSpec((seq, width), lambda b, g: (b, g))]
    if emit_state:
        out_shape += [jax.ShapeDtypeStruct((batch, depth, 2, heads, dh, dh), F32),
                      jax.ShapeDtypeStruct((batch, depth, 2, heads, 1, dh), F32),
                      jax.ShapeDtypeStruct((batch, depth, 2, heads, 1, 1), F32)]
        out_specs += [pl.BlockSpec((None, slots, 2, hp, dh, dh), lambda b, g: (b, first_slot, 0, g, 0, 0)),
                      pl.BlockSpec((None, slots, 2, hp, 1, dh), lambda b, g: (b, first_slot, 0, g, 0, 0)),
                      pl.BlockSpec((None, slots, 2, hp, 1, 1), lambda b, g: (b, first_slot, 0, g, 0, 0))]
    res = pl.pallas_call(
        kern,
        out_shape=tuple(out_shape),
        grid=(batch, hg),
        in_specs=in_specs,
        out_specs=tuple(out_specs),
        scratch_shapes=[
            pltpu.VMEM((seq + 2 * SUBLANES, width), F32), pltpu.VMEM((seq + 2 * SUBLANES, width), F32),
            pltpu.VMEM((seq, width), BF16), pltpu.VMEM((seq, width), BF16),
            pltpu.VMEM((hp * nc, aug, chunk), BF16),
            pltpu.VMEM((hp * nc, dh, chunk), F32), pltpu.VMEM((hp * nc, dh, chunk), F32),
            pltpu.VMEM((seq, npair * LANES), F32), pltpu.VMEM((nc, npair, chunk), F32),
            pltpu.VMEM((nc, npair, chunk), F32),
            pltpu.VMEM((npair, aug, dh), F32),
            pltpu.VMEM((npair, chunk, chunk), F32), pltpu.VMEM((npair, SUBLANES, chunk), F32),
            pltpu.VMEM((npair, chunk + aug, chunk), F32), pltpu.VMEM((npair, chunk, chunk + dh), BF16),
            pltpu.VMEM((npair, aug, chunk + dh), F32),
        ],
        input_output_aliases=aliases,
        compiler_params=_params("parallel", "parallel"),
        name="mlstm",
    )(*args)
    return res[0], tuple(res[1:])


def _outproj_kernel(pool_ref, ml_ref, x_ref, mod_ref, wp_ref, wm_ref, g_ref, b_ref, x1_ref, h2_ref, *, alpha):
    mix = _dot(pool_ref[...], wp_ref[...]) + _dot(ml_ref[...], wm_ref[...])
    x1 = _layer_norm(alpha * x_ref[...] + mod_ref[2:3, :] * mix) * g_ref[...] + b_ref[...]
    x1_ref[...] = x1
    h2 = _layer_norm(x1) * (1.0 + mod_ref[4:5, :]) + mod_ref[3:4, :]
    h2_ref[...] = h2.astype(BF16)


def _outproj(pool_out, ml_out, x, mod, cond_row, w_out, ln_g, ln_b, alpha):
    tokens, d = x.shape
    half = pool_out.shape[1]
    tm = TOKEN_TILE
    kern = functools.partial(_outproj_kernel, alpha=alpha)
    return pl.pallas_call(
        kern,
        out_shape=(jax.ShapeDtypeStruct((tokens, d), F32), jax.ShapeDtypeStruct((tokens, d), BF16)),
        grid=(tokens // tm,),
        in_specs=[
            pl.BlockSpec((tm, half), lambda i: (i, 0)),
            pl.BlockSpec((tm, half), lambda i: (i, 0)),
            pl.BlockSpec((tm, d), lambda i: (i, 0)),
            pl.BlockSpec((None, 6, d), lambda i: (cond_row(i * tm), 0, 0)),
            pl.BlockSpec((half, d), lambda i: (0, 0)),
            pl.BlockSpec((half, d), lambda i: (1, 0)),
            pl.BlockSpec((1, d), lambda i: (0, 0)),
            pl.BlockSpec((1, d), lambda i: (0, 0)),
        ],
        out_specs=(pl.BlockSpec((tm, d), lambda i: (i, 0)), pl.BlockSpec((tm, d), lambda i: (i, 0))),
        compiler_params=_params("parallel"),
        name="outproj",
    )(pool_out, ml_out, x, mod, w_out, w_out, ln_g, ln_b)


def _gelu_tanh(x):
    return 0.5 * x * (1.0 + jnp.tanh(math.sqrt(2.0 / math.pi) * (x + 0.044715 * (x * x * x))))


def _ffn_kernel(h_ref, x1_ref, mod_ref, wa_ref, wg_ref, wdw_ref, bdw_ref, wd_ref, lng_ref, lnb_ref, out_ref,
                gpad, a_s, u_s, *, rows, cols, tokens, alpha):
    j = pl.program_id(1)
    tf = wa_ref.shape[1]
    blk = FFN_BLOCK
    nb = tokens // blk
    base = cols if rows > 1 else 0

    @pl.when(j == 0)
    def _():
        out_ref[...] = jnp.zeros_like(out_ref)

    if rows > 1:
        zeros = jnp.zeros((cols, tf), F32)
        gpad[0:cols, :] = zeros
        gpad[cols + tokens:cols + tokens + cols, :] = zeros

    wdw = wdw_ref[...]
    bdw = bdw_ref[...]
    ci = lax.broadcasted_iota(jnp.int32, (cols, 1), 0)
    first = ci == 0
    last = ci == cols - 1
    taps_y = (-1, 0, 1) if rows > 1 else (0,)

    def up(b):
        hb = h_ref[b * blk:(b + 1) * blk, :]
        a_s[b * blk:(b + 1) * blk, :] = _dot(hb, wa_ref[...])
        gpad[base + b * blk:base + (b + 1) * blk, :] = _dot(hb, wg_ref[...])

    def conv_gate(b):
        for r in range(b * blk // cols, (b + 1) * blk // cols):
            off = r * cols
            for lb in range(tf // LANES):
                ls = slice(lb * LANES, (lb + 1) * LANES)
                left = mid = right = None
                for dy in taps_y:
                    xr = gpad[base + off + dy * cols:base + off + (dy + 1) * cols, ls]
                    kk = (dy + 1) * 3
                    tl, tc, tr = xr * wdw[kk:kk + 1, ls], xr * wdw[kk + 1:kk + 2, ls], xr * wdw[kk + 2:kk + 3, ls]
                    left = tl if left is None else left + tl
                    mid = tc if mid is None else mid + tc
                    right = tr if right is None else right + tr
                y = (mid + jnp.where(first, 0.0, pltpu.roll(left, 1, 0))
                     + jnp.where(last, 0.0, pltpu.roll(right, cols - 1, 0)) + bdw[:, ls])
                u_s[off:off + cols, ls] = (_gelu_tanh(y) * a_s[off:off + cols, ls]).astype(BF16)

    def down(b):
        out_ref[b * blk:(b + 1) * blk, :] += _dot(u_s[b * blk:(b + 1) * blk, :], wd_ref[...])

    for step in range(nb + 2):
        if step < nb:
            up(step)
        if 1 <= step <= nb:
            conv_gate(step - 1)
        if step >= 2:
            down(step - 2)

    @pl.when(j == pl.num_programs(1) - 1)
    def _():
        gate = mod_ref[5:6, :]
        lng = lng_ref[...]
        lnb = lnb_ref[...]

        def ln_body(r, carry):
            sl = pl.ds(pl.multiple_of(r * LN_ROWS, LN_ROWS), LN_ROWS)
            z = alpha * x1_ref[sl, :] + gate * out_ref[sl, :]
            out_ref[sl, :] = _layer_norm(z) * lng + lnb
            return carry

        lax.fori_loop(0, tokens // LN_ROWS, ln_body, 0)


def _ffn(h2, x1, mod, cond_row, w_up, w_dw, b_dw, w_down, ln_g, ln_b, rows, cols, alpha):
    total, d = x1.shape
    d_ff = w_down.shape[0]
    tm = FFN_TOKENS
    tf = FFN_COLS
    nj = d_ff // tf
    kern = functools.partial(_ffn_kernel, rows=rows, cols=cols, tokens=tm, alpha=alpha)
    return pl.pallas_call(
        kern,
        out_shape=jax.ShapeDtypeStruct((total, d), F32),
        grid=(total // tm, nj),
        in_specs=[
            pl.BlockSpec((tm, d), lambda i, j: (i, 0)),
            pl.BlockSpec((tm, d), lambda i, j: (i, 0)),
            pl.BlockSpec((None, 6, d), lambda i, j: (cond_row(i * tm), 0, 0)),
            pl.BlockSpec((None, d, tf), lambda i, j: (j, 0, 0)),
            pl.BlockSpec((None, d, tf), lambda i, j: (nj + j, 0, 0)),
            pl.BlockSpec((9, tf), lambda i, j: (0, j)),
            pl.BlockSpec((1, tf), lambda i, j: (0, j)),
            pl.BlockSpec((tf, d), lambda i, j: (j, 0)),
            pl.BlockSpec((1, d), lambda i, j: (0, 0)),
            pl.BlockSpec((1, d), lambda i, j: (0, 0)),
        ],
        out_specs=pl.BlockSpec((tm, d), lambda i, j: (i, 0)),
        scratch_shapes=[
            pltpu.VMEM((tm + (2 * cols if rows > 1 else 0), tf), F32),
            pltpu.VMEM((tm, tf), F32),
            pltpu.VMEM((tm, tf), BF16),
        ],
        compiler_params=_params("parallel", "arbitrary"),
        name="ffn",
    )(h2, x1, mod, w_up, w_up, w_dw, b_dw, w_down, ln_g, ln_b)


def _trunk_layer(x, batch, rows, cols, mod, cond_row, p, pmat, alpha, dh, **state_kw):
    seq = rows * cols
    hp = _mlstm_heads_per_step(seq, dh)
    proj, gates_t = _inproj(x, mod, cond_row, p["w_in_main"], p["w_gate_t"][hp], p["b_gate"][hp])
    pool_out = _pool(proj, pmat, p["w_pool"], p["pool_scale"], rows, cols, min(FFN_TOKENS, batch * seq))
    ml_out, states = _mlstm(proj, gates_t, p["w_qk_conv"], p["hn_g"], batch, seq, N_POOL_GROUPS * LANES, dh,
                            **state_kw)
    x1, h2 = _outproj(pool_out, ml_out, x, mod, cond_row, p["w_out"], p["ln1_g"], p["ln1_b"], alpha)
    x2 = _ffn(h2, x1, mod, cond_row, p["w_up"], p["w_dw"], p["b_dw"], p["w_down"], p["ln2_g"], p["ln2_b"],
              rows, cols, alpha)
    return x2, states


def kernel(x_prompt, x_sample, c, state_C, state_n, state_m, c_ctx, w_ada, b_ada, w_in, b_gates, w_qk_conv, hn_g,
           w_pool, pool_scale, w_out, ln1_g, ln1_b, w_up, w_dw, b_dw, w_down, ln2_g, ln2_b):
    b_p, l_p, d = x_prompt.shape
    b_s, l_s, _ = x_sample.shape
    depth = w_ada.shape[0]
    heads = MLSTM_HEADS
    dh = state_C.shape[-1]
    pool_w = N_POOL_GROUPS * LANES
    n_main = pool_w + 4 * heads * dh
    d_ff = w_down.shape[1]
    nj2 = 2 * d_ff // FFN_COLS
    alpha = (2.0 * depth) ** 0.25
    rows_s = l_s // GRID_W
    assert b_s + 1 <= COND_ROWS and l_p % POOL_TILE == 0 and POOL_TILE % GRID_W == 0

    cond = jnp.concatenate([c_ctx[None, :], c, jnp.zeros((COND_ROWS - 1 - b_s, d), F32)], axis=0)
    ada = _ada(cond, w_ada, b_ada).reshape(depth, COND_ROWS, 6, d)

    pmat_p = _pool_col_matrices(l_p)
    pmat_s = _pool_col_matrices(GRID_W)
    hps = sorted({_mlstm_heads_per_step(l_p, dh), _mlstm_heads_per_step(l_s, dh)})

    y_p = x_prompt.reshape(b_p * l_p, d)
    y_s = x_sample.reshape(b_s * l_s, d)
    states = None
    for l in range(depth):
        w_gate = w_in[l, :, n_main:]
        p = dict(
            w_in_main=w_in[l, :, :n_main].astype(BF16),
            w_gate_t={hp: w_gate[:, _gate_row_order(hp)].T.astype(BF16) for hp in hps},
            b_gate={hp: b_gates[l][_gate_row_order(hp)][:, None] for hp in hps},
            w_qk_conv=w_qk_conv[l], hn_g=hn_g[l][None, :],
            w_pool=w_pool[l].astype(BF16), pool_scale=pool_scale[l][None, :],
            w_out=w_out[l].astype(BF16), ln1_g=ln1_g[l][None, :], ln1_b=ln1_b[l][None, :],
            w_up=w_up[l].astype(BF16).reshape(d, nj2, FFN_COLS).transpose(1, 0, 2),
            w_dw=w_dw[l].reshape(9, d_ff), b_dw=b_dw[l][None, :],
            w_down=w_down[l].astype(BF16), ln2_g=ln2_g[l][None, :], ln2_b=ln2_b[l][None, :])
        y_p, states = _trunk_layer(y_p, b_p, 1, l_p, ada[l], lambda t: 0, p, pmat_p, alpha, dh,
                                   layer=l, depth=depth, prev=states)
        init = (state_C[:, l], state_n[:, l].reshape(b_s, 2, heads, 1, dh), state_m[:, l].reshape(b_s, 2, heads, 1, 1))
        y_s, _ = _trunk_layer(y_s, b_s, rows_s, GRID_W, ada[l], lambda t: 1 + t // l_s, p, pmat_s, alpha, dh,
                              init_state=init)
    new_c, new_n, new_m = states
    return (y_p.reshape(b_p, l_p, d), y_s.reshape(b_s, l_s, d), new_c, new_n.reshape(b_p, depth, 2, heads, dh),
            new_m.reshape(b_p, depth, 2, heads))
```

```python
import functools
import math

import numpy as np
import jax
import jax.numpy as jnp
from jax import lax
from jax.experimental import pallas as pl
from jax.experimental.pallas import tpu as pltpu

F32 = jnp.float32
BF16 = jnp.bfloat16

GRID_W = 64
POOL_WINDOWS = (2, 4, 8, 16)
N_POOL_GROUPS = len(POOL_WINDOWS)
MLSTM_HEADS = 4
LN_EPS = 1e-5

LANES = 128
SUBLANES = 8
BF16_ROWS = 16
VMEM_LIMIT_BYTES = 56 * 1024 * 1024

COND_ROWS = SUBLANES
MLSTM_CHUNK = LANES
MLSTM_STEP_BYTES = 2 * 1024 * 1024
POOL_TILE = 256
TOKEN_TILE = 1024
TOKEN_SUB = 256
FFN_TOKENS = 2048
FFN_BLOCK = 512
FFN_COLS = 256
LN_ROWS = 256


def _params(*sem):
    return pltpu.CompilerParams(dimension_semantics=sem, vmem_limit_bytes=VMEM_LIMIT_BYTES)


def _layer_norm(x):
    mu = jnp.mean(x, axis=-1, keepdims=True)
    xc = x - mu
    var = jnp.mean(xc * xc, axis=-1, keepdims=True)
    return xc * lax.rsqrt(var + LN_EPS)


def _dot(a, b):
    return jnp.dot(a, b, preferred_element_type=F32)


def _dot_nt(a, b):
    return lax.dot_general(a, b, (((1,), (1,)), ((), ())), preferred_element_type=F32)


def _ada_kernel(cond_ref, w_ref, b_ref, out_ref):
    cnd = cond_ref[...]
    act = (cnd * jax.nn.sigmoid(cnd)).astype(BF16)
    out_ref[...] = _dot(act, w_ref[...].astype(BF16)) + b_ref[...]


def _ada(cond, w_ada, b_ada):
    depth, d, n = w_ada.shape
    tn = 1536
    return pl.pallas_call(
        _ada_kernel,
        out_shape=jax.ShapeDtypeStruct((depth, COND_ROWS, n), F32),
        grid=(depth, n // tn),
        in_specs=[
            pl.BlockSpec((COND_ROWS, d), lambda l, j: (0, 0)),
            pl.BlockSpec((None, d, tn), lambda l, j: (l, 0, j)),
            pl.BlockSpec((None, 1, tn), lambda l, j: (l, 0, j)),
        ],
        out_specs=pl.BlockSpec((None, COND_ROWS, tn), lambda l, j: (l, 0, j)),
        compiler_params=_params("parallel", "parallel"),
        name="ada",
    )(cond, w_ada, b_ada.reshape(depth, 1, n))


def _inproj_kernel(x_ref, mod_ref, w_ref, wvt_ref, wgt_ref, bg_ref, proj_ref, vt_ref, gates_ref):
    sub = TOKEN_SUB
    for s in range(x_ref.shape[0] // sub):
        rs = slice(s * sub, (s + 1) * sub)
        h = _layer_norm(x_ref[rs, :]) * (1.0 + mod_ref[1:2, :]) + mod_ref[0:1, :]
        h = h.astype(BF16)
        proj_ref[rs, :] = _dot(h, w_ref[...])
        v_t = _dot_nt(wvt_ref[...], h).astype(BF16)
        gates_t = _dot_nt(wgt_ref[...], h) + bg_ref[...]
        for k in range(sub // LANES):
            vt_ref[s * (sub // LANES) + k] = v_t[:, k * LANES:(k + 1) * LANES]
            gates_ref[s * (sub // LANES) + k] = gates_t[:, k * LANES:(k + 1) * LANES]


def _inproj(x, mod, cond_row, w_main, w_v_t, w_gate_t, b_gate):
    tokens, d = x.shape
    n = w_main.shape[1]
    nv = w_v_t.shape[0]
    ng = w_gate_t.shape[0]
    tm = TOKEN_TILE
    return pl.pallas_call(
        _inproj_kernel,
        out_shape=(jax.ShapeDtypeStruct((tokens, n), F32), jax.ShapeDtypeStruct((tokens // LANES, nv, LANES), BF16),
                   jax.ShapeDtypeStruct((tokens // LANES, ng, LANES), F32)),
        grid=(tokens // tm,),
        in_specs=[
            pl.BlockSpec((tm, d), lambda i: (i, 0)),
            pl.BlockSpec((None, 6, d), lambda i: (cond_row(i * tm), 0, 0)),
            pl.BlockSpec((d, n), lambda i: (0, 0)),
            pl.BlockSpec((nv, d), lambda i: (0, 0)),
            pl.BlockSpec((ng, d), lambda i: (0, 0)),
            pl.BlockSpec((ng, 1), lambda i: (0, 0)),
        ],
        out_specs=(pl.BlockSpec((tm, n), lambda i: (i, 0)),
                   pl.BlockSpec((tm // LANES, nv, LANES), lambda i: (i, 0, 0)),
                   pl.BlockSpec((tm // LANES, ng, LANES), lambda i: (i, 0, 0))),
        compiler_params=_params("parallel"),
        name="inproj",
    )(x, mod, w_main, w_v_t, w_gate_t, b_gate)


def _pool_col_matrices(cols):
    t = np.arange(POOL_TILE)
    row, col = t // cols, t % cols
    mats = []
    for win in POOL_WINDOWS:
        hw = win // 2
        lo = np.maximum(col - hw, 0)
        hi = np.minimum(col + hw, cols)
        same_row = row[:, None] == row[None, :]
        inside = (col[None, :] >= lo[:, None]) & (col[None, :] < hi[:, None])
        mats.append((same_row & inside).astype(np.float32))
    return jnp.asarray(np.stack(mats), dtype=BF16)


def _pool_kernel(xp_ref, pmat_ref, wp_ref, ps_ref, out_ref, pad_ref, *, rows, cols, tokens):
    gdim = LANES
    tile = POOL_TILE
    halo = (max(POOL_WINDOWS) // 2) * cols if rows > 1 else 0
    t = lax.broadcasted_iota(jnp.int32, (tile, 1), 0)
    col = t & (cols - 1)
    if rows > 1:
        zeros = jnp.zeros((halo, gdim), F32)
        pad_ref[0:halo, :] = zeros
        pad_ref[halo + tokens:halo + tokens + halo, :] = zeros
    for gi, win in enumerate(POOL_WINDOWS):
        hw = win // 2
        gs = slice(gi * gdim, (gi + 1) * gdim)
        if rows > 1:
            pad_ref[halo:halo + tokens, :] = xp_ref[:, gs]
        cnt_c = (jnp.minimum(col + hw, cols) - jnp.maximum(col - hw, 0)).astype(F32)
        pm = pmat_ref[gi]
        wp = wp_ref[gi]
        scale = ps_ref[:, gs]

        for tt in range(tokens // tile):
            off = tt * tile
            xg = xp_ref[off:off + tile, gs]
            if rows > 1:
                acc = None
                for dr in range(-hw, hw):
                    term = pad_ref[off + halo + dr * cols:off + halo + dr * cols + tile, :]
                    acc = term if acc is None else acc + term
                rw = (off + t) >> int(math.log2(cols))
                cnt_r = (jnp.minimum(rw + hw, rows) - jnp.maximum(rw - hw, 0)).astype(F32)
                m1 = acc / cnt_r
            else:
                m1 = xg
            hi = m1.astype(BF16)
            lo = (m1 - hi.astype(F32)).astype(BF16)
            m2 = (_dot(pm, hi) + _dot(pm, lo)) / cnt_c
            y = (m2 - xg).astype(BF16)
            out_ref[off:off + tile, gs] = (_dot(y, wp) * scale).astype(BF16)


def _pool(proj, pmat, w_pool, pool_scale, rows, cols, step_tokens):
    total = proj.shape[0]
    pool_w = N_POOL_GROUPS * LANES
    halo = (max(POOL_WINDOWS) // 2) * cols if rows > 1 else 0
    kern = functools.partial(_pool_kernel, rows=rows, cols=cols, tokens=step_tokens)
    return pl.pallas_call(
        kern,
        out_shape=jax.ShapeDtypeStruct((total, pool_w), BF16),
        grid=(total // step_tokens,),
        in_specs=[
            pl.BlockSpec((step_tokens, pool_w), lambda i: (i, 0)),
            pl.BlockSpec(pmat.shape, lambda i: (0, 0, 0)),
            pl.BlockSpec(w_pool.shape, lambda i: (0, 0, 0)),
            pl.BlockSpec((1, pool_w), lambda i: (0, 0)),
        ],
        out_specs=pl.BlockSpec((step_tokens, pool_w), lambda i: (i, 0)),
        scratch_shapes=[pltpu.VMEM((step_tokens + 2 * halo if rows > 1 else SUBLANES, LANES), F32)],
        compiler_params=_params("parallel"),
        name="pool",
    )(proj, pmat, w_pool, pool_scale)


def _log_sigmoid(x):
    return jnp.minimum(x, 0.0) - jnp.log(1.0 + jnp.exp(-jnp.abs(x)))


def _split3(x):
    hi = x.astype(BF16)
    r1 = x - hi.astype(F32)
    mid = r1.astype(BF16)
    lo = (r1 - mid.astype(F32)).astype(BF16)
    return lo, mid, hi


def _mlstm_kernel(*refs, seq, chunk, hp, zero_init, emit_state, n_unused, state_slot):
    refs = list(refs)
    q_ref, k_ref, vt_ref, o_ref, g_ref, wq_ref, wk_ref, hng_ref = refs[:8]
    pos = 8
    if not zero_init:
        c0_ref, n0_ref, m0_ref = refs[pos:pos + 3]
        pos += 3
    pos += n_unused
    out_ref = refs[pos]
    pos += 1
    if emit_state:
        c_ref, n_ref, m_ref = refs[pos:pos + 3]
        pos += 3
    qpad, kpad, qs, ks, ht_f, ht_b, rcb_s, br_s, rc_s, ct_s, dt_s, mrow_s, z1_s, rhs_s, z2_s = refs[pos:]
    dh = q_ref.shape[1] // hp
    nc = seq // chunk
    npair = 2 * hp
    aug = dh + BF16_ROWS
    margin = SUBLANES

    for pad_ref, src_ref in ((qpad, q_ref), (kpad, k_ref)):
        pad_ref[0:margin, :] = jnp.zeros((margin, hp * dh), F32)
        pad_ref[margin + seq:2 * margin + seq, :] = jnp.zeros((margin, hp * dh), F32)
        pad_ref[margin:margin + seq, :] = src_ref[...]

    def conv_silu(pad_ref, w_ref, c, hs):
        win = pad_ref[pl.ds(pl.multiple_of(c * chunk, chunk), chunk + 2 * margin), hs]
        w = w_ref[:, hs]
        y = (pltpu.roll(win, 1, 0) * w[0:1, :] + win * w[1:2, :]
             + pltpu.roll(win, chunk + 2 * margin - 1, 0) * w[2:3, :])[margin:margin + chunk]
        return y * jax.nn.sigmoid(y)

    ri = lax.broadcasted_iota(jnp.int32, (chunk, chunk), 0)
    ci = lax.broadcasted_iota(jnp.int32, (chunk, chunk), 1)
    lower = ci <= ri
    upper = ci >= ri
    upper_b = jnp.where(upper, 1.0, 0.0).astype(BF16)
    row_is_fwd = lax.broadcasted_iota(jnp.int32, (npair, chunk), 0) < hp
    ones_row = jnp.where(lax.broadcasted_iota(jnp.int32, (BF16_ROWS, chunk), 0) == 0, 1.0, 0.0).astype(BF16)

    gates = g_ref[...]
    lf = _log_sigmoid(gates[:, npair:, :])
    lf3 = _split3(lf.reshape(nc * npair, chunk))
    prefix = (_dot(lf3[0], upper_b) + _dot(lf3[1], upper_b) + _dot(lf3[2], upper_b)).reshape(nc, npair, chunk)
    suffix = prefix[:, :, chunk - 1:chunk] - prefix + lf
    br = jnp.where(row_is_fwd, prefix, suffix)
    br_s[...] = br
    rc_s[...] = gates[:, :npair, :] - br

    def prep(c, carry):
        sl = pl.ds(pl.multiple_of(c * chunk, chunk), chunk)
        rc = rc_s[c]
        for p in range(npair):
            rcb_s[sl, p * LANES:(p + 1) * LANES] = jnp.transpose(jnp.broadcast_to(rc[p:p + 1, :], (LANES, chunk)))
        for hh in range(hp):
            hs = slice(hh * dh, (hh + 1) * dh)
            qs[sl, hs] = conv_silu(qpad, wq_ref, c, hs).astype(BF16)
            ks[sl, hs] = (conv_silu(kpad, wk_ref, c, hs) * (dh ** -0.5)).astype(BF16)
        return carry

    lax.fori_loop(0, nc, prep, 0)

    n_rows = lax.broadcasted_iota(jnp.int32, (BF16_ROWS, dh), 0) == 0
    ms = []
    for p in range(npair):
        di, hh = divmod(p, hp)
        if zero_init:
            ct_s[p] = jnp.zeros((aug, dh), F32)
            ms.append(jnp.zeros((1, 1), F32))
        else:
            ct_s[p, 0:dh, :] = jnp.transpose(c0_ref[di, hh])
            ct_s[p, dh:aug, :] = jnp.where(n_rows, n0_ref[di, hh], 0.0)
            ms.append(m0_ref[di, hh])

    def pair_args(p, i):
        di, hh = divmod(p, hp)
        fwd = di == 0
        c = i if fwd else nc - 1 - i
        last = chunk - 1 if fwd else 0
        sl = pl.ds(pl.multiple_of(c * chunk, chunk), chunk)
        return fwd, hh, c, last, sl, slice(hh * dh, (hh + 1) * dh)

    def body(i, ms):
        for p in range(npair):
            fwd, hh, c, last, sl, hs = pair_args(p, i)
            rcb = rcb_s[sl, p * LANES:(p + 1) * LANES]
            xt = jnp.where(upper if fwd else lower, rcb, -jnp.inf)
            mrow = jnp.maximum(jnp.max(xt, axis=0, keepdims=True), ms[p])
            dt_s[p] = jnp.exp(xt - mrow)
            mrow_s[p] = jnp.broadcast_to(mrow, (SUBLANES, chunk))
            w = jnp.exp(rcb - mrow[:, last:last + 1])
            rhs_s[p, :, chunk:] = (w * ks[sl, hs].astype(F32)).astype(BF16)
        zeros_q = jnp.zeros((chunk, dh), BF16)
        for pk in range(npair // 2):
            _, _, _, _, sl_a, hs_a = pair_args(2 * pk, i)
            _, _, _, _, sl_b, hs_b = pair_args(2 * pk + 1, i)
            lhs = jnp.concatenate([
                jnp.concatenate([ks[sl_a, hs_a], ks[sl_b, hs_b]], axis=1),
                jnp.concatenate([ct_s[2 * pk].astype(BF16), ct_s[2 * pk + 1].astype(BF16)], axis=1)], axis=0)
            q_diag = jnp.concatenate([
                jnp.concatenate([qs[sl_a, hs_a], zeros_q], axis=1),
                jnp.concatenate([zeros_q, qs[sl_b, hs_b]], axis=1)], axis=0)
            z1_s[pk] = _dot_nt(lhs, q_diag)
        for p in range(npair):
            lanes = slice((p % 2) * chunk, (p % 2 + 1) * chunk)
            rhs_s[p, :, 0:chunk] = (z1_s[p // 2, 0:chunk, lanes] * dt_s[p]).astype(BF16)
        for p in range(npair):
            fwd, hh, c, last, sl, hs = pair_args(p, i)
            z2_s[p] = _dot(jnp.concatenate([vt_ref[c, hs, :], ones_row], axis=0), rhs_s[p])
        new_ms = []
        for p in range(npair):
            fwd, hh, c, last, sl, hs = pair_args(p, i)
            m = ms[p]
            mrow = mrow_s[p, 0:1, :]
            lanes = slice((p % 2) * chunk, (p % 2 + 1) * chunk)
            num = z1_s[p // 2, chunk:, lanes] * jnp.exp(m - mrow) + z2_s[p, :, 0:chunk]
            b_r = br_s[c, p:p + 1, :]
            scale = 1.0 / jnp.maximum(jnp.abs(num[dh:dh + 1, :]), jnp.exp(-(b_r + mrow)))
            h_t = num[0:dh, :] * scale
            if fwd:
                ht_f[hh * nc + c] = h_t
            else:
                ht_b[hh * nc + c] = h_t
            mlast = mrow[:, last:last + 1]
            ct_s[p] = jnp.exp(m - mlast) * ct_s[p] + z2_s[p, :, chunk:]
            new_ms.append(b_r[:, last:last + 1] + mlast)
        return tuple(new_ms)

    ms = lax.fori_loop(0, nc, body, tuple(ms))

    def finish(c, carry):
        sl = pl.ds(pl.multiple_of(c * chunk, chunk), chunk)
        for hh in range(hp):
            hs = slice(hh * dh, (hh + 1) * dh)
            h_t = ht_f[hh * nc + c] + ht_b[hh * nc + c]
            mu = jnp.mean(h_t, axis=0, keepdims=True)
            hc = h_t - mu
            var = jnp.mean(hc * hc, axis=0, keepdims=True)
            hn = jnp.transpose(hc * lax.rsqrt(var + LN_EPS)) * hng_ref[:, hs]
            out_ref[sl, hs] = (jax.nn.sigmoid(o_ref[sl, hs]) * hn).astype(BF16)
        return carry

    lax.fori_loop(0, nc, finish, 0)

    if emit_state:
        for slot in range(c_ref.shape[0]):
            if slot != state_slot:
                c_ref[slot] = jnp.zeros(c_ref.shape[1:], F32)
                n_ref[slot] = jnp.zeros(n_ref.shape[1:], F32)
                m_ref[slot] = jnp.zeros(m_ref.shape[1:], F32)
        for p in range(npair):
            di, hh = divmod(p, hp)
            c_ref[state_slot, di, hh] = jnp.transpose(ct_s[p, 0:dh, :])
            n_ref[state_slot, di, hh] = ct_s[p, dh:dh + 1, :]
            m_ref[state_slot, di, hh] = ms[p]


def _mlstm_heads_per_step(seq, dh):
    return max(1, min(MLSTM_HEADS, MLSTM_STEP_BYTES // (seq * dh * 4)))


def _gate_row_order(hp):
    heads = MLSTM_HEADS
    order = []
    for g in range(heads // hp):
        for kind in range(2):
            for di in range(2):
                for hh in range(hp):
                    order.append((di * 2 + kind) * heads + g * hp + hh)
    return np.asarray(order)


def _mlstm(proj, v_t, gates_t, w_qk_conv, hn_g, batch, seq, col0, dh, init_state=None, layer=None, depth=None,
           prev=None):
    heads = MLSTM_HEADS
    chunk = MLSTM_CHUNK
    assert chunk == LANES and seq % chunk == 0
    nc = seq // chunk
    hp = _mlstm_heads_per_step(seq, dh)
    hg = heads // hp
    npair = 2 * hp
    aug = dh + BF16_ROWS
    width = hp * dh
    cb = col0 // width
    gb = heads * dh // width
    zero_init = init_state is None
    emit_state = layer is not None
    prev = () if prev is None else tuple(prev)
    slots, first_slot = (1, layer) if prev else (depth, 0)
    kern = functools.partial(_mlstm_kernel, seq=seq, chunk=chunk, hp=hp, zero_init=zero_init, emit_state=emit_state,
                             n_unused=len(prev), state_slot=(layer - first_slot) if emit_state else None)

    def tok_spec(group):
        return pl.BlockSpec((seq, width), lambda b, g: (b, cb + group * gb + g))

    in_specs = [
        tok_spec(0), tok_spec(1), pl.BlockSpec((nc, width, chunk), lambda b, g: (b, g, 0)), tok_spec(2),
        pl.BlockSpec((nc, 2 * npair, chunk), lambda b, g: (b, g, 0)),
        pl.BlockSpec((3, width), lambda b, g: (0, g)),
        pl.BlockSpec((3, width), lambda b, g: (0, gb + g)),
        pl.BlockSpec((1, width), lambda b, g: (0, g)),
    ]
    args = [proj, proj, v_t, proj, gates_t, w_qk_conv, w_qk_conv, hn_g]
    if not zero_init:
        in_specs += [pl.BlockSpec((None, 2, hp, dh, dh), lambda b, g: (b, 0, g, 0, 0)),
                     pl.BlockSpec((None, 2, hp, 1, dh), lambda b, g: (b, 0, g, 0, 0)),
                     pl.BlockSpec((None, 2, hp, 1, 1), lambda b, g: (b, 0, g, 0, 0))]
        args += list(init_state)
    aliases = {}
    for k, arr in enumerate(prev):
        aliases[len(args)] = 1 + k
        in_specs.append(pl.BlockSpec(memory_space=pl.ANY))
        args.append(arr)
    out_shape = [jax.ShapeDtypeStruct((batch * seq, heads * dh), BF16)]
    out_specs = [pl.BlockSpec((seq, width), lambda b, g: (b, g))]
    if emit_state:
        out_shape += [jax.ShapeDtypeStruct((batch, depth, 2, heads, dh, dh), F32),
                      jax.ShapeDtypeStruct((batch, depth, 2, heads, 1, dh), F32),
                      jax.ShapeDtypeStruct((batch, depth, 2, heads, 1, 1), F32)]
        out_specs += [pl.BlockSpec((None, slots, 2, hp, dh, dh), lambda b, g: (b, first_slot, 0, g, 0, 0)),
                      pl.BlockSpec((None, slots, 2, hp, 1, dh), lambda b, g: (b, first_slot, 0, g, 0, 0)),
                      pl.BlockSpec((None, slots, 2, hp, 1, 1), lambda b, g: (b, first_slot, 0, g, 0, 0))]
    res = pl.pallas_call(
        kern,
        out_shape=tuple(out_shape),
        grid=(batch, hg),
        in_specs=in_specs,
        out_specs=tuple(out_specs),
        scratch_shapes=[
            pltpu.VMEM((seq + 2 * SUBLANES, width), F32), pltpu.VMEM((seq + 2 * SUBLANES, width), F32),
            pltpu.VMEM((seq, width), BF16), pltpu.VMEM((seq, width), BF16),
            pltpu.VMEM((hp * nc, dh, chunk), F32), pltpu.VMEM((hp * nc, dh, chunk), F32),
            pltpu.VMEM((seq, npair * LANES), F32), pltpu.VMEM((nc, npair, chunk), F32),
            pltpu.VMEM((nc, npair, chunk), F32),
            pltpu.VMEM((npair, aug, dh), F32),
            pltpu.VMEM((npair, chunk, chunk), F32), pltpu.VMEM((npair, SUBLANES, chunk), F32),
            pltpu.VMEM((npair // 2, chunk + aug, 2 * chunk), F32), pltpu.VMEM((npair, chunk, chunk + dh), BF16),
            pltpu.VMEM((npair, aug, chunk + dh), F32),
        ],
        input_output_aliases=aliases,
        compiler_params=_params("parallel", "parallel"),
        name="mlstm",
    )(*args)
    return res[0], tuple(res[1:])


def _outproj_kernel(pool_ref, ml_ref, x_ref, mod_ref, wp_ref, wm_ref, g_ref, b_ref, x1_ref, h2_ref, *, alpha):
    sub = TOKEN_SUB
    for s in range(x_ref.shape[0] // sub):
        rs = slice(s * sub, (s + 1) * sub)
        mix = _dot(pool_ref[rs, :], wp_ref[...]) + _dot(ml_ref[rs, :], wm_ref[...])
        x1 = _layer_norm(alpha * x_ref[rs, :] + mod_ref[2:3, :] * mix) * g_ref[...] + b_ref[...]
        x1_ref[rs, :] = x1
        h2 = _layer_norm(x1) * (1.0 + mod_ref[4:5, :]) + mod_ref[3:4, :]
        h2_ref[rs, :] = h2.astype(BF16)


def _outproj(pool_out, ml_out, x, mod, cond_row, w_out, ln_g, ln_b, alpha):
    tokens, d = x.shape
    half = pool_out.shape[1]
    tm = TOKEN_TILE
    kern = functools.partial(_outproj_kernel, alpha=alpha)
    return pl.pallas_call(
        kern,
        out_shape=(jax.ShapeDtypeStruct((tokens, d), F32), jax.ShapeDtypeStruct((tokens, d), BF16)),
        grid=(tokens // tm,),
        in_specs=[
            pl.BlockSpec((tm, half), lambda i: (i, 0)),
            pl.BlockSpec((tm, half), lambda i: (i, 0)),
            pl.BlockSpec((tm, d), lambda i: (i, 0)),
            pl.BlockSpec((None, 6, d), lambda i: (cond_row(i * tm), 0, 0)),
            pl.BlockSpec((half, d), lambda i: (0, 0)),
            pl.BlockSpec((half, d), lambda i: (1, 0)),
            pl.BlockSpec((1, d), lambda i: (0, 0)),
            pl.BlockSpec((1, d), lambda i: (0, 0)),
        ],
        out_specs=(pl.BlockSpec((tm, d), lambda i: (i, 0)), pl.BlockSpec((tm, d), lambda i: (i, 0))),
        compiler_params=_params("parallel"),
        name="outproj",
    )(pool_out, ml_out, x, mod, w_out, w_out, ln_g, ln_b)


def _gelu_tanh(x):
    return 0.5 * x * (1.0 + jnp.tanh(math.sqrt(2.0 / math.pi) * (x + 0.044715 * (x * x * x))))


def _ffn_kernel(h_ref, x1_ref, mod_ref, wa_ref, wg_ref, wdw_ref, bdw_ref, wd_ref, lng_ref, lnb_ref, out_ref,
                gpad, a_s, u_s, *, rows, cols, tokens, alpha):
    j = pl.program_id(1)
    tf = wa_ref.shape[1]
    blk = FFN_BLOCK
    nb = tokens // blk
    base = (cols if rows > 1 else 0) + SUBLANES

    @pl.when(j == 0)
    def _():
        out_ref[...] = jnp.zeros_like(out_ref)

    gpad[0:base, :] = jnp.zeros((base, tf), F32)
    gpad[base + tokens:base + tokens + base, :] = jnp.zeros((base, tf), F32)

    wdw = wdw_ref[...]
    bdw = bdw_ref[...]
    ci = lax.broadcasted_iota(jnp.int32, (cols, 1), 0)
    first = ci == 0
    last = ci == cols - 1
    taps_y = (-1, 0, 1) if rows > 1 else (0,)

    def up(b):
        hb = h_ref[b * blk:(b + 1) * blk, :]
        a_s[b * blk:(b + 1) * blk, :] = _dot(hb, wa_ref[...])
        gpad[base + b * blk:base + (b + 1) * blk, :] = _dot(hb, wg_ref[...])

    def conv_gate(b):
        for r in range(b * blk // cols, (b + 1) * blk // cols):
            off = r * cols
            for lb in range(tf // LANES):
                ls = slice(lb * LANES, (lb + 1) * LANES)
                left = mid = right = None
                for dy in taps_y:
                    r0 = base + off + dy * cols
                    kk = (dy + 1) * 3
                    tl = gpad[r0 - 1:r0 - 1 + cols, ls] * wdw[kk:kk + 1, ls]
                    tc = gpad[r0:r0 + cols, ls] * wdw[kk + 1:kk + 2, ls]
                    tr = gpad[r0 + 1:r0 + 1 + cols, ls] * wdw[kk + 2:kk + 3, ls]
                    left = tl if left is None else left + tl
                    mid = tc if mid is None else mid + tc
                    right = tr if right is None else right + tr
                y = mid + jnp.where(first, 0.0, left) + jnp.where(last, 0.0, right) + bdw[:, ls]
                u_s[off:off + cols, ls] = (_gelu_tanh(y) * a_s[off:off + cols, ls]).astype(BF16)

    def down(b):
        out_ref[b * blk:(b + 1) * blk, :] += _dot(u_s[b * blk:(b + 1) * blk, :], wd_ref[...])

    for step in range(nb + 2):
        if step < nb:
            up(step)
        if 1 <= step <= nb:
            conv_gate(step - 1)
        if step >= 2:
            down(step - 2)

    @pl.when(j == pl.num_programs(1) - 1)
    def _():
        gate = mod_ref[5:6, :]
        lng = lng_ref[...]
        lnb = lnb_ref[...]

        def ln_body(r, carry):
            sl = pl.ds(pl.multiple_of(r * LN_ROWS, LN_ROWS), LN_ROWS)
            z = alpha * x1_ref[sl, :] + gate * out_ref[sl, :]
            out_ref[sl, :] = _layer_norm(z) * lng + lnb
            return carry

        lax.fori_loop(0, tokens // LN_ROWS, ln_body, 0)


def _ffn(h2, x1, mod, cond_row, w_up, w_dw, b_dw, w_down, ln_g, ln_b, rows, cols, alpha):
    total, d = x1.shape
    d_ff = w_down.shape[0]
    tm = FFN_TOKENS
    tf = FFN_COLS
    nj = d_ff // tf
    kern = functools.partial(_ffn_kernel, rows=rows, cols=cols, tokens=tm, alpha=alpha)
    return pl.pallas_call(
        kern,
        out_shape=jax.ShapeDtypeStruct((total, d), F32),
        grid=(total // tm, nj),
        in_specs=[
            pl.BlockSpec((tm, d), lambda i, j: (i, 0)),
            pl.BlockSpec((tm, d), lambda i, j: (i, 0)),
            pl.BlockSpec((None, 6, d), lambda i, j: (cond_row(i * tm), 0, 0)),
            pl.BlockSpec((d, tf), lambda i, j: (0, j)),
            pl.BlockSpec((d, tf), lambda i, j: (0, nj + j)),
            pl.BlockSpec((9, tf), lambda i, j: (0, j)),
            pl.BlockSpec((1, tf), lambda i, j: (0, j)),
            pl.BlockSpec((tf, d), lambda i, j: (j, 0)),
            pl.BlockSpec((1, d), lambda i, j: (0, 0)),
            pl.BlockSpec((1, d), lambda i, j: (0, 0)),
        ],
        out_specs=pl.BlockSpec((tm, d), lambda i, j: (i, 0)),
        scratch_shapes=[
            pltpu.VMEM((tm + 2 * ((cols if rows > 1 else 0) + SUBLANES), tf), F32),
            pltpu.VMEM((tm, tf), F32),
            pltpu.VMEM((tm, tf), BF16),
        ],
        compiler_params=_params("parallel", "arbitrary"),
        name="ffn",
    )(h2, x1, mod, w_up, w_up, w_dw, b_dw, w_down, ln_g, ln_b)


def _trunk_layer(x, batch, rows, cols, mod, cond_row, p, pmat, alpha, dh, **state_kw):
    seq = rows * cols
    hp = _mlstm_heads_per_step(seq, dh)
    proj, v_t, gates_t = _inproj(x, mod, cond_row, p["w_in_main"], p["w_v_t"], p["w_gate_t"][hp], p["b_gate"][hp])
    pool_out = _pool(proj, pmat, p["w_pool"], p["pool_scale"], rows, cols, min(FFN_TOKENS, batch * seq))
    ml_out, states = _mlstm(proj, v_t, gates_t, p["w_qk_conv"], p["hn_g"], batch, seq, N_POOL_GROUPS * LANES, dh,
                            **state_kw)
    x1, h2 = _outproj(pool_out, ml_out, x, mod, cond_row, p["w_out"], p["ln1_g"], p["ln1_b"], alpha)
    x2 = _ffn(h2, x1, mod, cond_row, p["w_up"], p["w_dw"], p["b_dw"], p["w_down"], p["ln2_g"], p["ln2_b"],
              rows, cols, alpha)
    return x2, states


def kernel(x_prompt, x_sample, c, state_C, state_n, state_m, c_ctx, w_ada, b_ada, w_in, b_gates, w_qk_conv, hn_g,
           w_pool, pool_scale, w_out, ln1_g, ln1_b, w_up, w_dw, b_dw, w_down, ln2_g, ln2_b):
    b_p, l_p, d = x_prompt.shape
    b_s, l_s, _ = x_sample.shape
    depth = w_ada.shape[0]
    heads = MLSTM_HEADS
    dh = state_C.shape[-1]
    pool_w = N_POOL_GROUPS * LANES
    mw = heads * dh
    n_main = pool_w + 4 * mw
    v0 = pool_w + 2 * mw
    d_ff = w_down.shape[1]
    alpha = (2.0 * depth) ** 0.25
    rows_s = l_s // GRID_W
    assert b_s + 1 <= COND_ROWS and l_p % POOL_TILE == 0 and POOL_TILE % GRID_W == 0

    cond = jnp.concatenate([c_ctx[None, :], c, jnp.zeros((COND_ROWS - 1 - b_s, d), F32)], axis=0)
    ada = _ada(cond, w_ada, b_ada).reshape(depth, COND_ROWS, 6, d)

    pmat_p = _pool_col_matrices(l_p)
    pmat_s = _pool_col_matrices(GRID_W)
    assert min(FFN_TOKENS, b_s * l_s) == l_s
    hps = sorted({_mlstm_heads_per_step(l_p, dh), _mlstm_heads_per_step(l_s, dh)})

    y_p = x_prompt.reshape(b_p * l_p, d)
    y_s = x_sample.reshape(b_s * l_s, d)
    states = None
    for l in range(depth):
        w_gate = w_in[l, :, n_main:]
        p = dict(
            w_in_main=jnp.concatenate([w_in[l, :, :v0], w_in[l, :, v0 + mw:n_main]], axis=1).astype(BF16),
            w_v_t=w_in[l, :, v0:v0 + mw].T.astype(BF16),
            w_gate_t={hp: w_gate[:, _gate_row_order(hp)].T.astype(BF16) for hp in hps},
            b_gate={hp: b_gates[l][_gate_row_order(hp)][:, None] for hp in hps},
            w_qk_conv=w_qk_conv[l], hn_g=hn_g[l][None, :],
            w_pool=w_pool[l].astype(BF16), pool_scale=pool_scale[l][None, :],
            w_out=w_out[l].astype(BF16), ln1_g=ln1_g[l][None, :], ln1_b=ln1_b[l][None, :],
            w_up=w_up[l].astype(BF16),
            w_dw=w_dw[l].reshape(9, d_ff), b_dw=b_dw[l][None, :],
            w_down=w_down[l].astype(BF16), ln2_g=ln2_g[l][None, :], ln2_b=ln2_b[l][None, :])
        y_p, states = _trunk_layer(y_p, b_p, 1, l_p, ada[l], lambda t: 0, p, pmat_p, alpha, dh,
                                   layer=l, depth=depth, prev=states)
        init = (state_C[:, l], state_n[:, l].reshape(b_s, 2, heads, 1, dh), state_m[:, l].reshape(b_s, 2, heads, 1, 1))
        y_s, _ = _trunk_layer(y_s, b_s, rows_s, GRID_W, ada[l], lambda t: 1 + t // l_s, p, pmat_s, alpha, dh,
                              init_state=init)
    new_c, new_n, new_m = states
    return (y_p.reshape(b_p, l_p, d), y_s.reshape(b_s, l_s, d), new_c, new_n.reshape(b_p, depth, 2, heads, dh),
            new_m.reshape(b_p, depth, 2, heads))
```

```python
import functools
import math

import numpy as np
import jax
import jax.numpy as jnp
from jax import lax
from jax.experimental import pallas as pl
from jax.experimental.pallas import tpu as pltpu

F32 = jnp.float32
BF16 = jnp.bfloat16

GRID_W = 64
POOL_WINDOWS = (2, 4, 8, 16)
N_POOL_GROUPS = len(POOL_WINDOWS)
MLSTM_HEADS = 4
LN_EPS = 1e-5

LANES = 128
SUBLANES = 8
BF16_ROWS = 16
VMEM_LIMIT_BYTES = 56 * 1024 * 1024

COND_ROWS = SUBLANES
MLSTM_CHUNK = LANES
MLSTM_UNROLL_CHUNKS = 2
MLSTM_STEP_BYTES =2 * 1024 * 1024
POOL_TILE = 256
TOKEN_TILE = 1024
TOKEN_SUB = 256
FFN_TOKENS = 2048
FFN_BLOCK = 512
FFN_COLS = 256
LN_ROWS = 256


def _params(*sem):
    return pltpu.CompilerParams(dimension_semantics=sem, vmem_limit_bytes=VMEM_LIMIT_BYTES)


def _layer_norm(x):
    mu = jnp.mean(x, axis=-1, keepdims=True)
    xc = x - mu
    var = jnp.mean(xc * xc, axis=-1, keepdims=True)
    return xc * lax.rsqrt(var + LN_EPS)


def _dot(a, b):
    return jnp.dot(a, b, preferred_element_type=F32)


def _dot_nt(a, b):
    return lax.dot_general(a, b, (((1,), (1,)), ((), ())), preferred_element_type=F32)


def _ada_kernel(cond_ref, w_ref, b_ref, out_ref):
    cnd = cond_ref[...]
    act = (cnd * jax.nn.sigmoid(cnd)).astype(BF16)
    out_ref[...] = _dot(act, w_ref[...].astype(BF16)) + b_ref[...]


def _ada(cond, w_ada, b_ada):
    depth, d, n = w_ada.shape
    tn = 1536
    return pl.pallas_call(
        _ada_kernel,
        out_shape=jax.ShapeDtypeStruct((depth, COND_ROWS, n), F32),
        grid=(depth, n // tn),
        in_specs=[
            pl.BlockSpec((COND_ROWS, d), lambda l, j: (0, 0)),
            pl.BlockSpec((None, d, tn), lambda l, j: (l, 0, j)),
            pl.BlockSpec((None, 1, tn), lambda l, j: (l, 0, j)),
        ],
        out_specs=pl.BlockSpec((None, COND_ROWS, tn), lambda l, j: (l, 0, j)),
        compiler_params=_params("parallel", "parallel"),
        name="ada",
    )(cond, w_ada, b_ada.reshape(depth, 1, n))


def _inproj_kernel(x_ref, mod_ref, w_ref, wvt_ref, wgt_ref, bg_ref, proj_ref, vt_ref, gates_ref):
    sub = TOKEN_SUB
    for s in range(x_ref.shape[0] // sub):
        rs = slice(s * sub, (s + 1) * sub)
        h = _layer_norm(x_ref[rs, :]) * (1.0 + mod_ref[1:2, :]) + mod_ref[0:1, :]
        h = h.astype(BF16)
        proj_ref[rs, :] = _dot(h, w_ref[...])
        v_t = _dot_nt(wvt_ref[...], h).astype(BF16)
        gates_t = _dot_nt(wgt_ref[...], h) + bg_ref[...]
        for k in range(sub // LANES):
            vt_ref[s * (sub // LANES) + k] = v_t[:, k * LANES:(k + 1) * LANES]
            gates_ref[s * (sub // LANES) + k] = gates_t[:, k * LANES:(k + 1) * LANES]


def _inproj(x, mod, cond_row, w_main, w_v_t, w_gate_t, b_gate):
    tokens, d = x.shape
    n = w_main.shape[1]
    nv = w_v_t.shape[0]
    ng = w_gate_t.shape[0]
    tm = TOKEN_TILE
    return pl.pallas_call(
        _inproj_kernel,
        out_shape=(jax.ShapeDtypeStruct((tokens, n), F32), jax.ShapeDtypeStruct((tokens // LANES, nv, LANES), BF16),
                   jax.ShapeDtypeStruct((tokens // LANES, ng, LANES), F32)),
        grid=(tokens // tm,),
        in_specs=[
            pl.BlockSpec((tm, d), lambda i: (i, 0)),
            pl.BlockSpec((None, 6, d), lambda i: (cond_row(i * tm), 0, 0)),
            pl.BlockSpec((d, n), lambda i: (0, 0)),
            pl.BlockSpec((nv, d), lambda i: (0, 0)),
            pl.BlockSpec((ng, d), lambda i: (0, 0)),
            pl.BlockSpec((ng, 1), lambda i: (0, 0)),
        ],
        out_specs=(pl.BlockSpec((tm, n), lambda i: (i, 0)),
                   pl.BlockSpec((tm // LANES, nv, LANES), lambda i: (i, 0, 0)),
                   pl.BlockSpec((tm // LANES, ng, LANES), lambda i: (i, 0, 0))),
        compiler_params=_params("parallel"),
        name="inproj",
    )(x, mod, w_main, w_v_t, w_gate_t, b_gate)


def _pool_col_matrices(cols):
    t = np.arange(POOL_TILE)
    row, col = t // cols, t % cols
    mats = []
    for win in POOL_WINDOWS:
        hw = win // 2
        lo = np.maximum(col - hw, 0)
        hi = np.minimum(col + hw, cols)
        same_row = row[:, None] == row[None, :]
        inside = (col[None, :] >= lo[:, None]) & (col[None, :] < hi[:, None])
        mats.append((same_row & inside).astype(np.float32))
    return jnp.asarray(np.stack(mats), dtype=BF16)


def _pool_kernel(xp_ref, pmat_ref, wp_ref, ps_ref, out_ref, pad_ref, *, rows, cols, tokens):
    gdim = LANES
    tile = POOL_TILE
    halo = (max(POOL_WINDOWS) // 2) * cols if rows > 1 else 0
    t = lax.broadcasted_iota(jnp.int32, (tile, 1), 0)
    col = t & (cols - 1)
    if rows > 1:
        zeros = jnp.zeros((halo, gdim), F32)
        pad_ref[0:halo, :] = zeros
        pad_ref[halo + tokens:halo + tokens + halo, :] = zeros
    for gi, win in enumerate(POOL_WINDOWS):
        hw = win // 2
        gs = slice(gi * gdim, (gi + 1) * gdim)
        if rows > 1:
            pad_ref[halo:halo + tokens, :] = xp_ref[:, gs]
        cnt_c = (jnp.minimum(col + hw, cols) - jnp.maximum(col - hw, 0)).astype(F32)
        pm = pmat_ref[gi]
        wp = wp_ref[gi]
        scale = ps_ref[:, gs]

        for tt in range(tokens // tile):
            off = tt * tile
            xg = xp_ref[off:off + tile, gs]
            if rows > 1:
                acc = None
                for dr in range(-hw, hw):
                    term = pad_ref[off + halo + dr * cols:off + halo + dr * cols + tile, :]
                    acc = term if acc is None else acc + term
                rw = (off + t) >> int(math.log2(cols))
                cnt_r = (jnp.minimum(rw + hw, rows) - jnp.maximum(rw - hw, 0)).astype(F32)
                m1 = acc / cnt_r
            else:
                m1 = xg
            hi = m1.astype(BF16)
            lo = (m1 - hi.astype(F32)).astype(BF16)
            m2 = (_dot(pm, hi) + _dot(pm, lo)) / cnt_c
            y = (m2 - xg).astype(BF16)
            out_ref[off:off + tile, gs] = (_dot(y, wp) * scale).astype(BF16)


def _pool(proj, pmat, w_pool, pool_scale, rows, cols, step_tokens):
    total = proj.shape[0]
    pool_w = N_POOL_GROUPS * LANES
    halo = (max(POOL_WINDOWS) // 2) * cols if rows > 1 else 0
    kern = functools.partial(_pool_kernel, rows=rows, cols=cols, tokens=step_tokens)
    return pl.pallas_call(
        kern,
        out_shape=jax.ShapeDtypeStruct((total, pool_w), BF16),
        grid=(total // step_tokens,),
        in_specs=[
            pl.BlockSpec((step_tokens, pool_w), lambda i: (i, 0)),
            pl.BlockSpec(pmat.shape, lambda i: (0, 0, 0)),
            pl.BlockSpec(w_pool.shape, lambda i: (0, 0, 0)),
            pl.BlockSpec((1, pool_w), lambda i: (0, 0)),
        ],
        out_specs=pl.BlockSpec((step_tokens, pool_w), lambda i: (i, 0)),
        scratch_shapes=[pltpu.VMEM((step_tokens + 2 * halo if rows > 1 else SUBLANES, LANES), F32)],
        compiler_params=_params("parallel"),
        name="pool",
    )(proj, pmat, w_pool, pool_scale)


def _log_sigmoid(x):
    return jnp.minimum(x, 0.0) - jnp.log(1.0 + jnp.exp(-jnp.abs(x)))


def _split3(x):
    hi = x.astype(BF16)
    r1 = x - hi.astype(F32)
    mid = r1.astype(BF16)
    lo = (r1 - mid.astype(F32)).astype(BF16)
    return lo, mid, hi


def _mlstm_kernel(*refs, seq, chunk, hp, zero_init, emit_state, n_unused, state_slot):
    refs = list(refs)
    q_ref, k_ref, vt_ref, o_ref, g_ref, wq_ref, wk_ref, hng_ref = refs[:8]
    pos = 8
    if not zero_init:
        c0_ref, n0_ref, m0_ref = refs[pos:pos + 3]
        pos += 3
    pos += n_unused
    out_ref = refs[pos]
    pos += 1
    if emit_state:
        c_ref, n_ref, m_ref = refs[pos:pos + 3]
        pos += 3
    qpad, kpad, qs, ks, ht_f, ht_b, rcb_s, br_s, rc_s, ct_s, dt_s, mrow_s, z1_s, rhs_s, z2_s = refs[pos:]
    dh = q_ref.shape[1] // hp
    nc = seq // chunk
    npair = 2 * hp
    aug = dh + BF16_ROWS
    margin = SUBLANES
    unroll = nc <= MLSTM_UNROLL_CHUNKS

    for pad_ref, src_ref in ((qpad, q_ref), (kpad, k_ref)):
        pad_ref[0:margin, :] = jnp.zeros((margin, hp * dh), F32)
        pad_ref[margin + seq:2 * margin + seq, :] = jnp.zeros((margin, hp * dh), F32)
        pad_ref[margin:margin + seq, :] = src_ref[...]

    def conv_silu(pad_ref, w_ref, c, hs):
        win = pad_ref[pl.ds(pl.multiple_of(c * chunk, chunk), chunk + 2 * margin), hs]
        w = w_ref[:, hs]
        y = (pltpu.roll(win, 1, 0) * w[0:1, :] + win * w[1:2, :]
             + pltpu.roll(win, chunk + 2 * margin - 1, 0) * w[2:3, :])[margin:margin + chunk]
        return y * jax.nn.sigmoid(y)

    ri = lax.broadcasted_iota(jnp.int32, (chunk, chunk), 0)
    ci = lax.broadcasted_iota(jnp.int32, (chunk, chunk), 1)
    lower = ci <= ri
    upper = ci >= ri
    upper_b = jnp.where(upper, 1.0, 0.0).astype(BF16)
    row_is_fwd = lax.broadcasted_iota(jnp.int32, (npair, chunk), 0) < hp
    ones_row = jnp.where(lax.broadcasted_iota(jnp.int32, (BF16_ROWS, chunk), 0) == 0, 1.0, 0.0).astype(BF16)

    gates = g_ref[...]
    lf = _log_sigmoid(gates[:, npair:, :])
    lf3 = _split3(lf.reshape(nc * npair, chunk))
    prefix = (_dot(lf3[0], upper_b) + _dot(lf3[1], upper_b) + _dot(lf3[2], upper_b)).reshape(nc, npair, chunk)
    suffix = prefix[:, :, chunk - 1:chunk] - prefix + lf
    br = jnp.where(row_is_fwd, prefix, suffix)
    br_s[...] = br
    rc_s[...] = gates[:, :npair, :] - br

    def prep(c, carry):
        sl = pl.ds(pl.multiple_of(c * chunk, chunk), chunk)
        rc = rc_s[c]
        for p in range(npair):
            rcb_s[sl, p * LANES:(p + 1) * LANES] = jnp.transpose(jnp.broadcast_to(rc[p:p + 1, :], (LANES, chunk)))
        for hh in range(hp):
            hs = slice(hh * dh, (hh + 1) * dh)
            qs[sl, hs] = conv_silu(qpad, wq_ref, c, hs).astype(BF16)
            ks[sl, hs] = (conv_silu(kpad, wk_ref, c, hs) * (dh ** -0.5)).astype(BF16)
        return carry

    n_rows =lax.broadcasted_iota(jnp.int32, (BF16_ROWS, dh), 0) == 0
    ms = []
    for p in range(npair):
        di, hh = divmod(p, hp)
        if zero_init:
            ct_s[p] = jnp.zeros((aug, dh), F32)
            ms.append(jnp.zeros((1, 1), F32))
        else:
            ct_s[p, 0:dh, :] = jnp.transpose(c0_ref[di, hh])
            ct_s[p, dh:aug, :] = jnp.where(n_rows, n0_ref[di, hh], 0.0)
            ms.append(m0_ref[di, hh])

    def pair_args(p, i):
        di, hh = divmod(p, hp)
        fwd = di == 0
        c = i if fwd else nc - 1 - i
        last = chunk - 1 if fwd else 0
        sl = pl.ds(pl.multiple_of(c * chunk, chunk), chunk)
        return fwd, hh, c, last, sl, slice(hh * dh, (hh + 1) * dh)

    def body(i, ms):
        for p in range(npair):
            fwd, hh, c, last, sl, hs = pair_args(p, i)
            rcb = rcb_s[sl, p * LANES:(p + 1) * LANES]
            xt = jnp.where(upper if fwd else lower, rcb, -jnp.inf)
            mrow = jnp.maximum(jnp.max(xt, axis=0, keepdims=True), ms[p])
            dt_s[p] = jnp.exp(xt - mrow)
            mrow_s[p] = jnp.broadcast_to(mrow, (SUBLANES, chunk))
            w = jnp.exp(rcb - mrow[:, last:last + 1])
            rhs_s[p, :, chunk:] = (w * ks[sl, hs].astype(F32)).astype(BF16)
        zeros_q = jnp.zeros((chunk, dh), BF16)
        for pk in range(npair // 2):
            _, _, _, _, sl_a, hs_a = pair_args(2 * pk, i)
            _, _, _, _, sl_b, hs_b = pair_args(2 * pk + 1, i)
            lhs = jnp.concatenate([
                jnp.concatenate([ks[sl_a, hs_a], ks[sl_b, hs_b]], axis=1),
                jnp.concatenate([ct_s[2 * pk].astype(BF16), ct_s[2 * pk + 1].astype(BF16)], axis=1)], axis=0)
            q_diag = jnp.concatenate([
                jnp.concatenate([qs[sl_a, hs_a], zeros_q], axis=1),
                jnp.concatenate([zeros_q, qs[sl_b, hs_b]], axis=1)], axis=0)
            z1_s[pk] = _dot_nt(lhs, q_diag)
        for p in range(npair):
            lanes = slice((p % 2) * chunk, (p % 2 + 1) * chunk)
            rhs_s[p, :, 0:chunk] = (z1_s[p // 2, 0:chunk, lanes] * dt_s[p]).astype(BF16)
        for p in range(npair):
            fwd, hh, c, last, sl, hs = pair_args(p, i)
            z2_s[p] = _dot(jnp.concatenate([vt_ref[c, hs, :], ones_row], axis=0), rhs_s[p])
        new_ms = []
        for p in range(npair):
            fwd, hh, c, last, sl, hs = pair_args(p, i)
            m = ms[p]
            mrow = mrow_s[p, 0:1, :]
            lanes = slice((p % 2) * chunk, (p % 2 + 1) * chunk)
            num = z1_s[p // 2, chunk:, lanes] * jnp.exp(m - mrow) + z2_s[p, :, 0:chunk]
            b_r = br_s[c, p:p + 1, :]
            scale = 1.0 / jnp.maximum(jnp.abs(num[dh:dh + 1, :]), jnp.exp(-(b_r + mrow)))
            h_t = num[0:dh, :] * scale
            if fwd:
                ht_f[hh * nc + c] = h_t
            else:
                ht_b[hh * nc + c] = h_t
            mlast = mrow[:, last:last + 1]
            ct_s[p] = jnp.exp(m - mlast) * ct_s[p] + z2_s[p, :, chunk:]
            new_ms.append(b_r[:, last:last + 1] + mlast)
        return tuple(new_ms)

    def finish(c, carry):
        sl = pl.ds(pl.multiple_of(c * chunk, chunk), chunk)
        for hh in range(hp):
            hs = slice(hh * dh, (hh + 1) * dh)
            h_t = ht_f[hh * nc + c] + ht_b[hh * nc + c]
            mu = jnp.mean(h_t, axis=0, keepdims=True)
            hc = h_t - mu
            var = jnp.mean(hc * hc, axis=0, keepdims=True)
            hn = jnp.transpose(hc * lax.rsqrt(var + LN_EPS)) * hng_ref[:, hs]
            out_ref[sl, hs] = (jax.nn.sigmoid(o_ref[sl, hs]) * hn).astype(BF16)
        return carry

    if unroll:
        lax.fori_loop(0, nc, prep, 0, unroll=True)
        ms = lax.fori_loop(0, nc, body, tuple(ms), unroll=True)
        lax.fori_loop(0, nc, finish, 0, unroll=True)
    else:
        half = nc // 2
        prep(0, 0)
        prep(nc - 1, 0)

        def first_half(i, ms):
            ms = body(i, ms)
            prep(i + 1, 0)
            prep(nc - 2 - i, 0)
            return ms

        def second_half(i, ms):
            ms = body(i, ms)
            finish(i, 0)
            finish(nc - 1 - i, 0)
            return ms

        ms = lax.fori_loop(0, half, first_half, tuple(ms))
        ms = lax.fori_loop(half, nc, second_half, ms)

    if emit_state:
        for slot in range(c_ref.shape[0]):
            if slot != state_slot:
                c_ref[slot] = jnp.zeros(c_ref.shape[1:], F32)
                n_ref[slot] = jnp.zeros(n_ref.shape[1:], F32)
                m_ref[slot] = jnp.zeros(m_ref.shape[1:], F32)
        for p in range(npair):
            di, hh = divmod(p, hp)
            c_ref[state_slot, di, hh] = jnp.transpose(ct_s[p, 0:dh, :])
            n_ref[state_slot, di, hh] = ct_s[p, dh:dh + 1, :]
            m_ref[state_slot, di, hh] = ms[p]


def _mlstm_heads_per_step(seq, dh):
    return max(1, min(MLSTM_HEADS, MLSTM_STEP_BYTES // (seq * dh * 4)))


def _gate_row_order(hp):
    heads = MLSTM_HEADS
    order = []
    for g in range(heads // hp):
        for kind in range(2):
            for di in range(2):
                for hh in range(hp):
                    order.append((di * 2 + kind) * heads + g * hp + hh)
    return np.asarray(order)


def _mlstm(proj, v_t, gates_t, w_qk_conv, hn_g, batch, seq, col0, dh, init_state=None, layer=None, depth=None,
           prev=None):
    heads = MLSTM_HEADS
    chunk = MLSTM_CHUNK
    assert chunk == LANES and seq % chunk == 0 and (seq // chunk <= MLSTM_UNROLL_CHUNKS or seq // chunk % 2 == 0)
    nc = seq // chunk
    hp = _mlstm_heads_per_step(seq, dh)
    hg = heads // hp
    npair = 2 * hp
    aug = dh + BF16_ROWS
    width = hp * dh
    cb = col0 // width
    gb = heads * dh // width
    zero_init = init_state is None
    emit_state = layer is not None
    prev = () if prev is None else tuple(prev)
    slots, first_slot = (1, layer) if prev else (depth, 0)
    kern = functools.partial(_mlstm_kernel, seq=seq, chunk=chunk, hp=hp, zero_init=zero_init, emit_state=emit_state,
                             n_unused=len(prev), state_slot=(layer - first_slot) if emit_state else None)

    def tok_spec(group):
        return pl.BlockSpec((seq, width), lambda b, g: (b, cb + group * gb + g))

    in_specs = [
        tok_spec(0), tok_spec(1), pl.BlockSpec((nc, width, chunk), lambda b, g: (b, g, 0)), tok_spec(2),
        pl.BlockSpec((nc, 2 * npair, chunk), lambda b, g: (b, g, 0)),
        pl.BlockSpec((3, width), lambda b, g: (0, g)),
        pl.BlockSpec((3, width), lambda b, g: (0, gb + g)),
        pl.BlockSpec((1, width), lambda b, g: (0, g)),
    ]
    args = [proj, proj, v_t, proj, gates_t, w_qk_conv, w_qk_conv, hn_g]
    if not zero_init:
        in_specs += [pl.BlockSpec((None, 2, hp, dh, dh), lambda b, g: (b, 0, g, 0, 0)),
                     pl.BlockSpec((None, 2, hp, 1, dh), lambda b, g: (b, 0, g, 0, 0)),
                     pl.BlockSpec((None, 2, hp, 1, 1), lambda b, g: (b, 0, g, 0, 0))]
        args += list(init_state)
    aliases = {}
    for k, arr in enumerate(prev):
        aliases[len(args)] = 1 + k
        in_specs.append(pl.BlockSpec(memory_space=pl.ANY))
        args.append(arr)
    out_shape = [jax.ShapeDtypeStruct((batch * seq, heads * dh), BF16)]
    out_specs = [pl.BlockSpec((seq, width), lambda b, g: (b, g))]
    if emit_state:
        out_shape += [jax.ShapeDtypeStruct((batch, depth, 2, heads, dh, dh), F32),
                      jax.ShapeDtypeStruct((batch, depth, 2, heads, 1, dh), F32),
                      jax.ShapeDtypeStruct((batch, depth, 2, heads, 1, 1), F32)]
        out_specs += [pl.BlockSpec((None, slots, 2, hp, dh, dh), lambda b, g: (b, first_slot, 0, g, 0, 0)),
                      pl.BlockSpec((None, slots, 2, hp, 1, dh), lambda b, g: (b, first_slot, 0, g, 0, 0)),
                      pl.BlockSpec((None, slots, 2, hp, 1, 1), lambda b, g: (b, first_slot, 0, g, 0, 0))]
    res = pl.pallas_call(
        kern,
        out_shape=tuple(out_shape),
        grid=(batch, hg),
        in_specs=in_specs,
        out_specs=tuple(out_specs),
        scratch_shapes=[
            pltpu.VMEM((seq + 2 * SUBLANES, width), F32), pltpu.VMEM((seq + 2 * SUBLANES, width), F32),
            pltpu.VMEM((seq, width), BF16), pltpu.VMEM((seq, width), BF16),
            pltpu.VMEM((hp * nc, dh, chunk), F32), pltpu.VMEM((hp * nc, dh, chunk), F32),
            pltpu.VMEM((seq, npair * LANES), F32), pltpu.VMEM((nc, npair, chunk), F32),
            pltpu.VMEM((nc, npair, chunk), F32),
            pltpu.VMEM((npair, aug, dh), F32),
            pltpu.VMEM((npair, chunk, chunk), F32), pltpu.VMEM((npair, SUBLANES, chunk), F32),
            pltpu.VMEM((npair // 2, chunk + aug, 2 * chunk), F32), pltpu.VMEM((npair, chunk, chunk + dh), BF16),
            pltpu.VMEM((npair, aug, chunk + dh), F32),
        ],
        input_output_aliases=aliases,
        compiler_params=_params("parallel", "parallel"),
        name="mlstm",
    )(*args)
    return res[0], tuple(res[1:])


def _outproj_kernel(pool_ref, ml_ref, x_ref, mod_ref, wp_ref, wm_ref, g_ref, b_ref, x1_ref, h2_ref, *, alpha):
    sub = TOKEN_SUB
    for s in range(x_ref.shape[0] // sub):
        rs = slice(s * sub, (s + 1) * sub)
        mix = _dot(pool_ref[rs, :], wp_ref[...]) + _dot(ml_ref[rs, :], wm_ref[...])
        x1 = _layer_norm(alpha * x_ref[rs, :] + mod_ref[2:3, :] * mix) * g_ref[...] + b_ref[...]
        x1_ref[rs, :] = x1
        h2 = _layer_norm(x1) * (1.0 + mod_ref[4:5, :]) + mod_ref[3:4, :]
        h2_ref[rs, :] = h2.astype(BF16)


def _outproj(pool_out, ml_out, x, mod, cond_row, w_out, ln_g, ln_b, alpha):
    tokens, d = x.shape
    half = pool_out.shape[1]
    tm = TOKEN_TILE
    kern = functools.partial(_outproj_kernel, alpha=alpha)
    return pl.pallas_call(
        kern,
        out_shape=(jax.ShapeDtypeStruct((tokens, d), F32), jax.ShapeDtypeStruct((tokens, d), BF16)),
        grid=(tokens // tm,),
        in_specs=[
            pl.BlockSpec((tm, half), lambda i: (i, 0)),
            pl.BlockSpec((tm, half), lambda i: (i, 0)),
            pl.BlockSpec((tm, d), lambda i: (i, 0)),
            pl.BlockSpec((None, 6, d), lambda i: (cond_row(i * tm), 0, 0)),
            pl.BlockSpec((half, d), lambda i: (0, 0)),
            pl.BlockSpec((half, d), lambda i: (1, 0)),
            pl.BlockSpec((1, d), lambda i: (0, 0)),
            pl.BlockSpec((1, d), lambda i: (0, 0)),
        ],
        out_specs=(pl.BlockSpec((tm, d), lambda i: (i, 0)), pl.BlockSpec((tm, d), lambda i: (i, 0))),
        compiler_params=_params("parallel"),
        name="outproj",
    )(pool_out, ml_out, x, mod, w_out, w_out, ln_g, ln_b)


def _gelu_tanh(x):
    return 0.5 * x * (1.0 + jnp.tanh(math.sqrt(2.0 / math.pi) * (x + 0.044715 * (x * x * x))))


def _ffn_kernel(h_ref, x1_ref, mod_ref, wa_ref, wg_ref, wdw_ref, bdw_ref, wd_ref, lng_ref, lnb_ref, out_ref,
                gpad, a_s, u_s, wa_b, wg_b, wd_b, *, rows, cols, tokens, alpha):
    j = pl.program_id(1)
    tf = wa_ref.shape[1]
    blk = FFN_BLOCK
    nb = tokens // blk
    base = (cols if rows > 1 else 0) + SUBLANES

    @pl.when(j == 0)
    def _():
        out_ref[...] = jnp.zeros_like(out_ref)

    wa_b[...] = wa_ref[...].astype(BF16)
    wg_b[...] = wg_ref[...].astype(BF16)
    wd_b[...] = wd_ref[...].astype(BF16)
    gpad[0:base, :] = jnp.zeros((base, tf), F32)
    gpad[base + tokens:base + tokens + base, :] = jnp.zeros((base, tf), F32)

    wdw = wdw_ref[...]
    bdw = bdw_ref[...]
    ci = lax.broadcasted_iota(jnp.int32, (cols, 1), 0)
    first = ci == 0
    last = ci == cols - 1
    taps_y = (-1, 0, 1) if rows > 1 else (0,)

    def up(b):
        hb = h_ref[b * blk:(b + 1) * blk, :]
        a_s[b * blk:(b + 1) * blk, :] = _dot(hb, wa_b[...])
        gpad[base + b * blk:base + (b + 1) * blk, :] = _dot(hb, wg_b[...])

    def conv_gate(b):
        for r in range(b * blk // cols, (b + 1) * blk // cols):
            off = r * cols
            for lb in range(tf // LANES):
                ls = slice(lb * LANES, (lb + 1) * LANES)
                left = mid = right = None
                for dy in taps_y:
                    r0 = base + off + dy * cols
                    kk = (dy + 1) * 3
                    tl = gpad[r0 - 1:r0 - 1 + cols, ls] * wdw[kk:kk + 1, ls]
                    tc = gpad[r0:r0 + cols, ls] * wdw[kk + 1:kk + 2, ls]
                    tr = gpad[r0 + 1:r0 + 1 + cols, ls] * wdw[kk + 2:kk + 3, ls]
                    left = tl if left is None else left + tl
                    mid = tc if mid is None else mid + tc
                    right = tr if right is None else right + tr
                y = mid + jnp.where(first, 0.0, left) + jnp.where(last, 0.0, right) + bdw[:, ls]
                u_s[off:off + cols, ls] = (_gelu_tanh(y) * a_s[off:off + cols, ls]).astype(BF16)

    def down(b):
        out_ref[b * blk:(b + 1) * blk, :] += _dot(u_s[b * blk:(b + 1) * blk, :], wd_b[...])

    for step in range(nb + 2):
        if step < nb:
            up(step)
        if 1 <= step <= nb:
            conv_gate(step - 1)
        if step >= 2:
            down(step - 2)

    @pl.when(j == pl.num_programs(1) - 1)
    def _():
        gate = mod_ref[5:6, :]
        lng = lng_ref[...]
        lnb = lnb_ref[...]

        def ln_body(r, carry):
            sl = pl.ds(pl.multiple_of(r * LN_ROWS, LN_ROWS), LN_ROWS)
            z = alpha * x1_ref[sl, :] + gate * out_ref[sl, :]
            out_ref[sl, :] = _layer_norm(z) * lng + lnb
            return carry

        lax.fori_loop(0, tokens // LN_ROWS, ln_body, 0)


def _ffn(h2, x1, mod, cond_row, layer, w_up, w_dw, b_dw, w_down, ln_g, ln_b, rows, cols, alpha):
    total, d = x1.shape
    d_ff = w_down.shape[1]
    tm = FFN_TOKENS
    tf = FFN_COLS
    nj = d_ff // tf
    kern = functools.partial(_ffn_kernel, rows=rows, cols=cols, tokens=tm, alpha=alpha)
    return pl.pallas_call(
        kern,
        out_shape=jax.ShapeDtypeStruct((total, d), F32),
        grid=(total // tm, nj),
        in_specs=[
            pl.BlockSpec((tm, d), lambda i, j: (i, 0)),
            pl.BlockSpec((tm, d), lambda i, j: (i, 0)),
            pl.BlockSpec((None, 6, d), lambda i, j: (cond_row(i * tm), 0, 0)),
            pl.BlockSpec((None, d, tf), lambda i, j: (layer, 0, j)),
            pl.BlockSpec((None, d, tf), lambda i, j: (layer, 0, nj + j)),
            pl.BlockSpec((9, tf), lambda i, j: (0, j)),
            pl.BlockSpec((1, tf), lambda i, j: (0, j)),
            pl.BlockSpec((None, tf, d), lambda i, j: (layer, j, 0)),
            pl.BlockSpec((1, d), lambda i, j: (0, 0)),
            pl.BlockSpec((1, d), lambda i, j: (0, 0)),
        ],
        out_specs=pl.BlockSpec((tm, d), lambda i, j: (i, 0)),
        scratch_shapes=[
            pltpu.VMEM((tm + 2 * ((cols if rows > 1 else 0) + SUBLANES), tf), F32),
            pltpu.VMEM((tm, tf), F32),
            pltpu.VMEM((tm, tf), BF16),
            pltpu.VMEM((d, tf), BF16), pltpu.VMEM((d, tf), BF16), pltpu.VMEM((tf, d), BF16),
        ],
        compiler_params=_params("parallel", "arbitrary"),
        name="ffn",
    )(h2, x1, mod, w_up, w_up, w_dw, b_dw, w_down, ln_g, ln_b)


def _trunk_layer(x, batch, rows, cols, mod, cond_row, p, pmat, alpha, dh, **state_kw):
    seq = rows * cols
    hp = _mlstm_heads_per_step(seq, dh)
    proj, v_t, gates_t = _inproj(x, mod, cond_row, p["w_in_main"], p["w_v_t"], p["w_gate_t"][hp], p["b_gate"][hp])
    pool_out = _pool(proj, pmat, p["w_pool"], p["pool_scale"], rows, cols, min(FFN_TOKENS, batch * seq))
    ml_out, states = _mlstm(proj, v_t, gates_t, p["w_qk_conv"], p["hn_g"], batch, seq, N_POOL_GROUPS * LANES, dh,
                            **state_kw)
    x1, h2 = _outproj(pool_out, ml_out, x, mod, cond_row, p["w_out"], p["ln1_g"], p["ln1_b"], alpha)
    x2 = _ffn(h2, x1, mod, cond_row, p["layer"], p["w_up"], p["w_dw"], p["b_dw"], p["w_down"], p["ln2_g"],
              p["ln2_b"], rows, cols, alpha)
    return x2, states


def kernel(x_prompt, x_sample, c, state_C, state_n, state_m, c_ctx, w_ada, b_ada, w_in, b_gates, w_qk_conv, hn_g,
           w_pool, pool_scale, w_out, ln1_g, ln1_b, w_up, w_dw, b_dw, w_down, ln2_g, ln2_b):
    b_p, l_p, d = x_prompt.shape
    b_s, l_s, _ = x_sample.shape
    depth = w_ada.shape[0]
    heads = MLSTM_HEADS
    dh = state_C.shape[-1]
    pool_w = N_POOL_GROUPS * LANES
    mw = heads * dh
    n_main = pool_w + 4 * mw
    v0 = pool_w + 2 * mw
    d_ff = w_down.shape[1]
    alpha = (2.0 * depth) ** 0.25
    rows_s = l_s // GRID_W
    assert b_s + 1 <= COND_ROWS and l_p % POOL_TILE == 0 and POOL_TILE % GRID_W == 0

    cond = jnp.concatenate([c_ctx[None, :], c, jnp.zeros((COND_ROWS - 1 - b_s, d), F32)], axis=0)
    ada = _ada(cond, w_ada, b_ada).reshape(depth, COND_ROWS, 6, d)

    pmat_p = _pool_col_matrices(l_p)
    pmat_s = _pool_col_matrices(GRID_W)
    assert min(FFN_TOKENS, b_s * l_s) == l_s
    hps = sorted({_mlstm_heads_per_step(l_p, dh), _mlstm_heads_per_step(l_s, dh)})

    y_p = x_prompt.reshape(b_p * l_p, d)
    y_s = x_sample.reshape(b_s * l_s, d)
    states = None
    for l in range(depth):
        w_gate = w_in[l, :, n_main:]
        p = dict(
            w_in_main=jnp.concatenate([w_in[l, :, :v0], w_in[l, :, v0 + mw:n_main]], axis=1).astype(BF16),
            w_v_t=w_in[l, :, v0:v0 + mw].T.astype(BF16),
            w_gate_t={hp: w_gate[:, _gate_row_order(hp)].T.astype(BF16) for hp in hps},
            b_gate={hp: b_gates[l][_gate_row_order(hp)][:, None] for hp in hps},
            w_qk_conv=w_qk_conv[l], hn_g=hn_g[l][None, :],
            w_pool=w_pool[l].astype(BF16), pool_scale=pool_scale[l][None, :],
            w_out=w_out[l].astype(BF16), ln1_g=ln1_g[l][None, :], ln1_b=ln1_b[l][None, :],
            layer=l, w_up=w_up,
            w_dw=w_dw[l].reshape(9, d_ff), b_dw=b_dw[l][None, :],
            w_down=w_down, ln2_g=ln2_g[l][None, :], ln2_b=ln2_b[l][None, :])
        y_p, states = _trunk_layer(y_p, b_p, 1, l_p, ada[l], lambda t: 0, p, pmat_p, alpha, dh,
                                   layer=l, depth=depth, prev=states)
        init = (state_C[:, l], state_n[:, l].reshape(b_s, 2, heads, 1, dh), state_m[:, l].reshape(b_s, 2, heads, 1, 1))
        y_s, _ = _trunk_layer(y_s, b_s, rows_s, GRID_W, ada[l], lambda t: 1 + t // l_s, p, pmat_s, alpha, dh,
                              init_state=init)
    new_c, new_n, new_m = states
    return (y_p.reshape(b_p, l_p, d), y_s.reshape(b_s, l_s, d), new_c, new_n.reshape(b_p, depth, 2, heads, dh),
            new_m.reshape(b_p, depth, 2, heads))
```

```python
import functools
import math

import numpy as np
import jax
import jax.numpy as jnp
from jax import lax
from jax.experimental import pallas as pl
from jax.experimental.pallas import tpu as pltpu

F32 = jnp.float32
BF16 = jnp.bfloat16

GRID_W = 64
POOL_WINDOWS = (2, 4, 8, 16)
N_POOL_GROUPS = len(POOL_WINDOWS)
MLSTM_HEADS = 4
LN_EPS = 1e-5

LANES = 128
SUBLANES = 8
BF16_ROWS = 16
VMEM_LIMIT_BYTES = 56 * 1024 * 1024

COND_ROWS = SUBLANES
MLSTM_CHUNK = LANES
MLSTM_UNROLL_CHUNKS = 2
MLSTM_STEP_BYTES =2 * 1024 * 1024
POOL_TILE = 256
TOKEN_TILE = 1024
TOKEN_SUB = 256
FFN_TOKENS = 2048
FFN_BLOCK = 512
FFN_COLS = 256
LN_ROWS = 256


def _params(*sem):
    return pltpu.CompilerParams(dimension_semantics=sem, vmem_limit_bytes=VMEM_LIMIT_BYTES)


def _layer_norm(x):
    mu = jnp.mean(x, axis=-1, keepdims=True)
    xc = x - mu
    var = jnp.mean(xc * xc, axis=-1, keepdims=True)
    return xc * lax.rsqrt(var + LN_EPS)


def _dot(a, b):
    return jnp.dot(a, b, preferred_element_type=F32)


def _dot_nt(a, b):
    return lax.dot_general(a, b, (((1,), (1,)), ((), ())), preferred_element_type=F32)


def _ada_kernel(cond_ref, w_ref, b_ref, out_ref):
    cnd = cond_ref[...]
    act = (cnd * jax.nn.sigmoid(cnd)).astype(BF16)
    out_ref[...] = _dot(act, w_ref[...].astype(BF16)) + b_ref[...]


def _ada(cond, w_ada, b_ada):
    depth, d, n = w_ada.shape
    tn = 1536
    return pl.pallas_call(
        _ada_kernel,
        out_shape=jax.ShapeDtypeStruct((depth, COND_ROWS, n), F32),
        grid=(depth, n // tn),
        in_specs=[
            pl.BlockSpec((COND_ROWS, d), lambda l, j: (0, 0)),
            pl.BlockSpec((None, d, tn), lambda l, j: (l, 0, j)),
            pl.BlockSpec((None, 1, tn), lambda l, j: (l, 0, j)),
        ],
        out_specs=pl.BlockSpec((None, COND_ROWS, tn), lambda l, j: (l, 0, j)),
        compiler_params=_params("parallel", "parallel"),
        name="ada",
    )(cond, w_ada, b_ada.reshape(depth, 1, n))


def _inproj_kernel(x_ref, mod_ref, w_ref, wvt_ref, wgt_ref, bg_ref, proj_ref, vt_ref, gates_ref):
    sub = TOKEN_SUB
    for s in range(x_ref.shape[0] // sub):
        rs = slice(s * sub, (s + 1) * sub)
        h = _layer_norm(x_ref[rs, :]) * (1.0 + mod_ref[1:2, :]) + mod_ref[0:1, :]
        h = h.astype(BF16)
        proj_ref[rs, :] = _dot(h, w_ref[...])
        v_t = _dot_nt(wvt_ref[...], h).astype(BF16)
        gates_t = _dot_nt(wgt_ref[...], h) + bg_ref[...]
        for k in range(sub // LANES):
            vt_ref[s * (sub // LANES) + k] = v_t[:, k * LANES:(k + 1) * LANES]
            gates_ref[s * (sub // LANES) + k] = gates_t[:, k * LANES:(k + 1) * LANES]


def _inproj(x, mod, cond_row, w_main, w_v_t, w_gate_t, b_gate):
    tokens, d = x.shape
    n = w_main.shape[1]
    nv = w_v_t.shape[0]
    ng = w_gate_t.shape[0]
    tm = TOKEN_TILE
    return pl.pallas_call(
        _inproj_kernel,
        out_shape=(jax.ShapeDtypeStruct((tokens, n), F32), jax.ShapeDtypeStruct((tokens // LANES, nv, LANES), BF16),
                   jax.ShapeDtypeStruct((tokens // LANES, ng, LANES), F32)),
        grid=(tokens // tm,),
        in_specs=[
            pl.BlockSpec((tm, d), lambda i: (i, 0)),
            pl.BlockSpec((None, 6, d), lambda i: (cond_row(i * tm), 0, 0)),
            pl.BlockSpec((d, n), lambda i: (0, 0)),
            pl.BlockSpec((nv, d), lambda i: (0, 0)),
            pl.BlockSpec((ng, d), lambda i: (0, 0)),
            pl.BlockSpec((ng, 1), lambda i: (0, 0)),
        ],
        out_specs=(pl.BlockSpec((tm, n), lambda i: (i, 0)),
                   pl.BlockSpec((tm // LANES, nv, LANES), lambda i: (i, 0, 0)),
                   pl.BlockSpec((tm // LANES, ng, LANES), lambda i: (i, 0, 0))),
        compiler_params=_params("parallel"),
        name="inproj",
    )(x, mod, w_main, w_v_t, w_gate_t, b_gate)


def _pool_col_matrices(cols):
    t = np.arange(POOL_TILE)
    row, col = t // cols, t % cols
    mats = []
    for win in POOL_WINDOWS:
        hw = win // 2
        lo = np.maximum(col - hw, 0)
        hi = np.minimum(col + hw, cols)
        same_row = row[:, None] == row[None, :]
        inside = (col[None, :] >= lo[:, None]) & (col[None, :] < hi[:, None])
        mats.append((same_row & inside).astype(np.float32))
    return jnp.asarray(np.stack(mats), dtype=BF16)


def _pool_kernel(xp_ref, pmat_ref, wp_ref, ps_ref, out_ref, pad_ref, hl_s, y_s, *, rows, cols, tokens):
    gdim = LANES
    tile = POOL_TILE
    halo = (max(POOL_WINDOWS) // 2) * cols if rows > 1 else 0
    t = lax.broadcasted_iota(jnp.int32, (tile, 1), 0)
    col = t & (cols - 1)
    if rows > 1:
        zeros = jnp.zeros((halo, gdim), F32)
        pad_ref[0:halo, :] = zeros
        pad_ref[halo + tokens:halo + tokens + halo, :] = zeros
    for gi, win in enumerate(POOL_WINDOWS):
        hw = win // 2
        gs = slice(gi * gdim, (gi + 1) * gdim)
        if rows > 1:
            pad_ref[halo:halo + tokens, :] = xp_ref[:, gs]
        cnt_c = (jnp.minimum(col + hw, cols) - jnp.maximum(col - hw, 0)).astype(F32)
        pm = pmat_ref[gi]
        wp = wp_ref[gi]
        scale = ps_ref[:, gs]

        nt = tokens // tile
        for tt in range(nt):
            off = tt * tile
            if rows > 1:
                acc = None
                for dr in range(-hw, hw):
                    term = pad_ref[off + halo + dr * cols:off + halo + dr * cols + tile, :]
                    acc = term if acc is None else acc + term
                rw = (off + t) >> int(math.log2(cols))
                cnt_r = (jnp.minimum(rw + hw, rows) - jnp.maximum(rw - hw, 0)).astype(F32)
                m1 = acc / cnt_r
            else:
                m1 = xp_ref[off:off + tile, gs]
            hi = m1.astype(BF16)
            hl_s[gi * nt + tt] = jnp.concatenate([hi, (m1 - hi.astype(F32)).astype(BF16)], axis=1)
        for tt in range(nt):
            off = tt * tile
            both = _dot(pm, hl_s[gi * nt + tt])
            m2 = (both[:, :gdim] + both[:, gdim:]) / cnt_c
            y_s[gi, off:off + tile, :] = (m2 - xp_ref[off:off + tile, gs]).astype(BF16)
        out_ref[:, gs] = (_dot(y_s[gi], wp) * scale).astype(BF16)


def _pool(proj, pmat, w_pool, pool_scale, rows, cols, step_tokens):
    total = proj.shape[0]
    pool_w = N_POOL_GROUPS * LANES
    halo = (max(POOL_WINDOWS) // 2) * cols if rows > 1 else 0
    kern = functools.partial(_pool_kernel, rows=rows, cols=cols, tokens=step_tokens)
    return pl.pallas_call(
        kern,
        out_shape=jax.ShapeDtypeStruct((total, pool_w), BF16),
        grid=(total // step_tokens,),
        in_specs=[
            pl.BlockSpec((step_tokens, pool_w), lambda i: (i, 0)),
            pl.BlockSpec(pmat.shape, lambda i: (0, 0, 0)),
            pl.BlockSpec(w_pool.shape, lambda i: (0, 0, 0)),
            pl.BlockSpec((1, pool_w), lambda i: (0, 0)),
        ],
        out_specs=pl.BlockSpec((step_tokens, pool_w), lambda i: (i, 0)),
        scratch_shapes=[
            pltpu.VMEM((step_tokens + 2 * halo if rows > 1 else SUBLANES, LANES), F32),
            pltpu.VMEM((N_POOL_GROUPS * (step_tokens // POOL_TILE), POOL_TILE, 2 * LANES), BF16),
            pltpu.VMEM((N_POOL_GROUPS, step_tokens, LANES), BF16),
        ],
        compiler_params=_params("parallel"),
        name="pool",
    )(proj, pmat, w_pool, pool_scale)


def _log_sigmoid(x):
    return jnp.minimum(x, 0.0) - jnp.log(1.0 + jnp.exp(-jnp.abs(x)))


def _split3(x):
    hi = x.astype(BF16)
    r1 = x - hi.astype(F32)
    mid = r1.astype(BF16)
    lo = (r1 - mid.astype(F32)).astype(BF16)
    return lo, mid, hi


def _mlstm_kernel(*refs, seq, chunk, hp, zero_init, emit_state, n_unused, state_slot):
    refs = list(refs)
    q_ref, k_ref, vt_ref, o_ref, g_ref, wq_ref, wk_ref, hng_ref = refs[:8]
    pos = 8
    if not zero_init:
        c0_ref, n0_ref, m0_ref = refs[pos:pos + 3]
        pos += 3
    pos += n_unused
    out_ref = refs[pos]
    pos += 1
    if emit_state:
        c_ref, n_ref, m_ref = refs[pos:pos + 3]
        pos += 3
    qpad, kpad, qs, ks, ht_f, ht_b, rcb_s, br_s, rc_s, ct_s, dt_s, mrow_s, z1_s, rhs_s, z2_s = refs[pos:]
    dh = q_ref.shape[1] // hp
    nc = seq // chunk
    npair = 2 * hp
    aug = dh + BF16_ROWS
    margin = SUBLANES
    unroll = nc <= MLSTM_UNROLL_CHUNKS

    for pad_ref, src_ref in ((qpad, q_ref), (kpad, k_ref)):
        pad_ref[0:margin, :] = jnp.zeros((margin, hp * dh), F32)
        pad_ref[margin + seq:2 * margin + seq, :] = jnp.zeros((margin, hp * dh), F32)
        pad_ref[margin:margin + seq, :] = src_ref[...]

    def conv_silu(pad_ref, w_ref, c, hs):
        w = w_ref[:, hs]
        start = c * chunk if unroll else pl.multiple_of(c * chunk, chunk)
        win = pad_ref[pl.ds(start, chunk + 2 * margin), hs]
        y = (pltpu.roll(win, 1, 0) * w[0:1, :] + win * w[1:2, :]
             + pltpu.roll(win, chunk + 2 * margin - 1, 0) * w[2:3, :])[margin:margin + chunk]
        return y * jax.nn.sigmoid(y)

    ri = lax.broadcasted_iota(jnp.int32, (chunk, chunk), 0)
    ci = lax.broadcasted_iota(jnp.int32, (chunk, chunk), 1)
    lower = ci <= ri
    upper = ci >= ri
    upper_b = jnp.where(upper, 1.0, 0.0).astype(BF16)
    row_is_fwd = lax.broadcasted_iota(jnp.int32, (npair, chunk), 0) < hp
    ones_row = jnp.where(lax.broadcasted_iota(jnp.int32, (BF16_ROWS, chunk), 0) == 0, 1.0, 0.0).astype(BF16)

    gates = g_ref[...]
    lf = _log_sigmoid(gates[:, npair:, :])
    lf3 = _split3(lf.reshape(nc * npair, chunk))
    prefix = (_dot(lf3[0], upper_b) + _dot(lf3[1], upper_b) + _dot(lf3[2], upper_b)).reshape(nc, npair, chunk)
    suffix = prefix[:, :, chunk - 1:chunk] - prefix + lf
    br = jnp.where(row_is_fwd, prefix, suffix)
    br_s[...] = br
    rc_s[...] = gates[:, :npair, :] - br

    def rows_of(c):
        return slice(c * chunk, (c + 1) * chunk) if unroll else pl.ds(pl.multiple_of(c * chunk, chunk), chunk)

    def prep(c, carry):
        sl = rows_of(c)
        rc = rc_s[c]
        for p in range(npair):
            rcb_s[sl, p * LANES:(p + 1) * LANES] = jnp.transpose(jnp.broadcast_to(rc[p:p + 1, :], (LANES, chunk)))
        for hh in range(hp):
            hs = slice(hh * dh, (hh + 1) * dh)
            qs[sl, hs] = conv_silu(qpad, wq_ref, c, hs).astype(BF16)
            ks[sl, hs] = (conv_silu(kpad, wk_ref, c, hs) * (dh ** -0.5)).astype(BF16)
        return carry

    n_rows =lax.broadcasted_iota(jnp.int32, (BF16_ROWS, dh), 0) == 0
    ms = []
    for p in range(npair):
        di, hh = divmod(p, hp)
        if zero_init:
            ct_s[p] = jnp.zeros((aug, dh), F32)
            ms.append(jnp.zeros((1, 1), F32))
        else:
            ct_s[p, 0:dh, :] = jnp.transpose(c0_ref[di, hh])
            ct_s[p, dh:aug, :] = jnp.where(n_rows, n0_ref[di, hh], 0.0)
            ms.append(m0_ref[di, hh])

    def pair_args(p, i):
        di, hh = divmod(p, hp)
        fwd = di == 0
        c = i if fwd else nc - 1 - i
        last = chunk - 1 if fwd else 0
        sl = rows_of(c)
        return fwd, hh, c, last, sl, slice(hh * dh, (hh + 1) * dh)

    def body(i, ms):
        for p in range(npair):
            fwd, hh, c, last, sl, hs = pair_args(p, i)
            rcb = rcb_s[sl, p * LANES:(p + 1) * LANES]
            xt = jnp.where(upper if fwd else lower, rcb, -jnp.inf)
            mrow = jnp.maximum(jnp.max(xt, axis=0, keepdims=True), ms[p])
            dt_s[p] = jnp.exp(xt - mrow)
            mrow_s[p] = jnp.broadcast_to(mrow, (SUBLANES, chunk))
            w = jnp.exp(rcb - mrow[:, last:last + 1])
            rhs_s[p, :, chunk:] = (w * ks[sl, hs].astype(F32)).astype(BF16)
        zeros_q = jnp.zeros((chunk, dh), BF16)
        for pk in range(npair // 2):
            _, _, _, _, sl_a, hs_a = pair_args(2 * pk, i)
            _, _, _, _, sl_b, hs_b = pair_args(2 * pk + 1, i)
            lhs = jnp.concatenate([
                jnp.concatenate([ks[sl_a, hs_a], ks[sl_b, hs_b]], axis=1),
                jnp.concatenate([ct_s[2 * pk].astype(BF16), ct_s[2 * pk + 1].astype(BF16)], axis=1)], axis=0)
            q_diag = jnp.concatenate([
                jnp.concatenate([qs[sl_a, hs_a], zeros_q], axis=1),
                jnp.concatenate([zeros_q, qs[sl_b, hs_b]], axis=1)], axis=0)
            z1_s[pk] = _dot_nt(lhs, q_diag)
        for p in range(npair):
            lanes = slice((p % 2) * chunk, (p % 2 + 1) * chunk)
            rhs_s[p, :, 0:chunk] = (z1_s[p // 2, 0:chunk, lanes] * dt_s[p]).astype(BF16)
        for p in range(npair):
            fwd, hh, c, last, sl, hs = pair_args(p, i)
            z2_s[p] = _dot(jnp.concatenate([vt_ref[c, hs, :], ones_row], axis=0), rhs_s[p])
        new_ms = []
        for p in range(npair):
            fwd, hh, c, last, sl, hs = pair_args(p, i)
            m = ms[p]
            mrow = mrow_s[p, 0:1, :]
            lanes = slice((p % 2) * chunk, (p % 2 + 1) * chunk)
            num = z1_s[p // 2, chunk:, lanes] * jnp.exp(m - mrow) + z2_s[p, :, 0:chunk]
            b_r = br_s[c, p:p + 1, :]
            scale = 1.0 / jnp.maximum(jnp.abs(num[dh:dh + 1, :]), jnp.exp(-(b_r + mrow)))
            h_t = num[0:dh, :] * scale
            if fwd:
                ht_f[hh * nc + c] = h_t
            else:
                ht_b[hh * nc + c] = h_t
            mlast = mrow[:, last:last + 1]
            ct_s[p] = jnp.exp(m - mlast) * ct_s[p] + z2_s[p, :, chunk:]
            new_ms.append(b_r[:, last:last + 1] + mlast)
        return tuple(new_ms)

    def finish(c, carry):
        sl = rows_of(c)
        for hh in range(hp):
            hs = slice(hh * dh, (hh + 1) * dh)
            h_t = ht_f[hh * nc + c] + ht_b[hh * nc + c]
            mu = jnp.mean(h_t, axis=0, keepdims=True)
            hc = h_t - mu
            var = jnp.mean(hc * hc, axis=0, keepdims=True)
            hn = jnp.transpose(hc * lax.rsqrt(var + LN_EPS)) * hng_ref[:, hs]
            out_ref[sl, hs] = (jax.nn.sigmoid(o_ref[sl, hs]) * hn).astype(BF16)
        return carry

    if unroll:
        for c in range(nc):
            prep(c, 0)
        ms = tuple(ms)
        for i in range(nc):
            ms = body(i, ms)
        for c in range(nc):
            finish(c, 0)
    else:
        half = nc // 2
        prep(0, 0)
        prep(nc - 1, 0)

        def first_half(i, ms):
            ms = body(i, ms)
            prep(i + 1, 0)
            prep(nc - 2 - i, 0)
            return ms

        def second_half(i, ms):
            ms = body(i, ms)
            finish(i, 0)
            finish(nc - 1 - i, 0)
            return ms

        ms = lax.fori_loop(0, half, first_half, tuple(ms))
        ms = lax.fori_loop(half, nc, second_half, ms)

    if emit_state:
        for slot in range(c_ref.shape[0]):
            if slot != state_slot:
                c_ref[slot] = jnp.zeros(c_ref.shape[1:], F32)
                n_ref[slot] = jnp.zeros(n_ref.shape[1:], F32)
                m_ref[slot] = jnp.zeros(m_ref.shape[1:], F32)
        for p in range(npair):
            di, hh = divmod(p, hp)
            c_ref[state_slot, di, hh] = jnp.transpose(ct_s[p, 0:dh, :])
            n_ref[state_slot, di, hh] = ct_s[p, dh:dh + 1, :]
            m_ref[state_slot, di, hh] = ms[p]


def _mlstm_heads_per_step(seq, dh):
    return max(1, min(MLSTM_HEADS, MLSTM_STEP_BYTES // (seq * dh * 4)))


def _gate_row_order(hp):
    heads = MLSTM_HEADS
    order = []
    for g in range(heads // hp):
        for kind in range(2):
            for di in range(2):
                for hh in range(hp):
                    order.append((di * 2 + kind) * heads + g * hp + hh)
    return np.asarray(order)


def _mlstm(proj, v_t, gates_t, w_qk_conv, hn_g, batch, seq, col0, dh, init_state=None, layer=None, depth=None,
           prev=None):
    heads = MLSTM_HEADS
    chunk = MLSTM_CHUNK
    assert chunk == LANES and seq % chunk == 0 and (seq // chunk <= MLSTM_UNROLL_CHUNKS or seq // chunk % 2 == 0)
    nc = seq // chunk
    hp = _mlstm_heads_per_step(seq, dh)
    hg = heads // hp
    npair = 2 * hp
    aug = dh + BF16_ROWS
    width = hp * dh
    cb = col0 // width
    gb = heads * dh // width
    zero_init = init_state is None
    emit_state = layer is not None
    prev = () if prev is None else tuple(prev)
    slots, first_slot = (1, layer) if prev else (depth, 0)
    kern = functools.partial(_mlstm_kernel, seq=seq, chunk=chunk, hp=hp, zero_init=zero_init, emit_state=emit_state,
                             n_unused=len(prev), state_slot=(layer - first_slot) if emit_state else None)

    def tok_spec(group):
        return pl.BlockSpec((seq, width), lambda b, g: (b, cb + group * gb + g))

    in_specs = [
        tok_spec(0), tok_spec(1), pl.BlockSpec((nc, width, chunk), lambda b, g: (b, g, 0)), tok_spec(2),
        pl.BlockSpec((nc, 2 * npair, chunk), lambda b, g: (b, g, 0)),
        pl.BlockSpec((3, width), lambda b, g: (0, g)),
        pl.BlockSpec((3, width), lambda b, g: (0, gb + g)),
        pl.BlockSpec((1, width), lambda b, g: (0, g)),
    ]
    args = [proj, proj, v_t, proj, gates_t, w_qk_conv, w_qk_conv, hn_g]
    if not zero_init:
        in_specs += [pl.BlockSpec((None, 2, hp, dh, dh), lambda b, g: (b, 0, g, 0, 0)),
                     pl.BlockSpec((None, 2, hp, 1, dh), lambda b, g: (b, 0, g, 0, 0)),
                     pl.BlockSpec((None, 2, hp, 1, 1), lambda b, g: (b, 0, g, 0, 0))]
        args += list(init_state)
    aliases = {}
    for k, arr in enumerate(prev):
        aliases[len(args)] = 1 + k
        in_specs.append(pl.BlockSpec(memory_space=pl.ANY))
        args.append(arr)
    out_shape = [jax.ShapeDtypeStruct((batch * seq, heads * dh), BF16)]
    out_specs = [pl.BlockSpec((seq, width), lambda b, g: (b, g))]
    if emit_state:
        out_shape += [jax.ShapeDtypeStruct((batch, depth, 2, heads, dh, dh), F32),
                      jax.ShapeDtypeStruct((batch, depth, 2, heads, 1, dh), F32),
                      jax.ShapeDtypeStruct((batch, depth, 2, heads, 1, 1), F32)]
        out_specs += [pl.BlockSpec((None, slots, 2, hp, dh, dh), lambda b, g: (b, first_slot, 0, g, 0, 0)),
                      pl.BlockSpec((None, slots, 2, hp, 1, dh), lambda b, g: (b, first_slot, 0, g, 0, 0)),
                      pl.BlockSpec((None, slots, 2, hp, 1, 1), lambda b, g: (b, first_slot, 0, g, 0, 0))]
    res = pl.pallas_call(
        kern,
        out_shape=tuple(out_shape),
        grid=(batch, hg),
        in_specs=in_specs,
        out_specs=tuple(out_specs),
        scratch_shapes=[
            pltpu.VMEM((seq + 2 * SUBLANES, width), F32), pltpu.VMEM((seq + 2 * SUBLANES, width), F32),
            pltpu.VMEM((seq, width), BF16), pltpu.VMEM((seq, width), BF16),
            pltpu.VMEM((hp * nc, dh, chunk), F32), pltpu.VMEM((hp * nc, dh, chunk), F32),
            pltpu.VMEM((seq, npair * LANES), F32), pltpu.VMEM((nc, npair, chunk), F32),
            pltpu.VMEM((nc, npair, chunk), F32),
            pltpu.VMEM((npair, aug, dh), F32),
            pltpu.VMEM((npair, chunk, chunk), F32), pltpu.VMEM((npair, SUBLANES, chunk), F32),
            pltpu.VMEM((npair // 2, chunk + aug, 2 * chunk), F32), pltpu.VMEM((npair, chunk, chunk + dh), BF16),
            pltpu.VMEM((npair, aug, chunk + dh), F32),
        ],
        input_output_aliases=aliases,
        compiler_params=_params("parallel", "parallel"),
        name="mlstm",
    )(*args)
    return res[0], tuple(res[1:])


def _outproj_kernel(pool_ref, ml_ref, x_ref, mod_ref, wp_ref, wm_ref, g_ref, b_ref, x1_ref, h2_ref, *, alpha):
    sub = TOKEN_SUB
    for s in range(x_ref.shape[0] // sub):
        rs = slice(s * sub, (s + 1) * sub)
        mix = _dot(pool_ref[rs, :], wp_ref[...]) + _dot(ml_ref[rs, :], wm_ref[...])
        x1 = _layer_norm(alpha * x_ref[rs, :] + mod_ref[2:3, :] * mix) * g_ref[...] + b_ref[...]
        x1_ref[rs, :] = x1
        h2 = _layer_norm(x1) * (1.0 + mod_ref[4:5, :]) + mod_ref[3:4, :]
        h2_ref[rs, :] = h2.astype(BF16)


def _outproj(pool_out, ml_out, x, mod, cond_row, w_out, ln_g, ln_b, alpha):
    tokens, d = x.shape
    half = pool_out.shape[1]
    tm = TOKEN_TILE
    kern = functools.partial(_outproj_kernel, alpha=alpha)
    return pl.pallas_call(
        kern,
        out_shape=(jax.ShapeDtypeStruct((tokens, d), F32), jax.ShapeDtypeStruct((tokens, d), BF16)),
        grid=(tokens // tm,),
        in_specs=[
            pl.BlockSpec((tm, half), lambda i: (i, 0)),
            pl.BlockSpec((tm, half), lambda i: (i, 0)),
            pl.BlockSpec((tm, d), lambda i: (i, 0)),
            pl.BlockSpec((None, 6, d), lambda i: (cond_row(i * tm), 0, 0)),
            pl.BlockSpec((half, d), lambda i: (0, 0)),
            pl.BlockSpec((half, d), lambda i: (1, 0)),
            pl.BlockSpec((1, d), lambda i: (0, 0)),
            pl.BlockSpec((1, d), lambda i: (0, 0)),
        ],
        out_specs=(pl.BlockSpec((tm, d), lambda i: (i, 0)), pl.BlockSpec((tm, d), lambda i: (i, 0))),
        compiler_params=_params("parallel"),
        name="outproj",
    )(pool_out, ml_out, x, mod, w_out, w_out, ln_g, ln_b)


def _gelu_tanh(x):
    return 0.5 * x * (1.0 + jnp.tanh(math.sqrt(2.0 / math.pi) * (x + 0.044715 * (x * x * x))))


def _ffn_kernel(h_ref, x1_ref, mod_ref, wa_ref, wg_ref, wdw_ref, bdw_ref, wd_ref, lng_ref, lnb_ref, out_ref,
                gpad, a_s, u_s, wa_b, wg_b, wd_b, *, rows, cols, tokens, alpha):
    j = pl.program_id(1)
    tf = wa_ref.shape[1]
    blk = FFN_BLOCK
    nb = tokens // blk
    base = (cols if rows > 1 else 0) + SUBLANES

    @pl.when(j == 0)
    def _():
        out_ref[...] = jnp.zeros_like(out_ref)

    wa_b[...] = wa_ref[...].astype(BF16)
    wg_b[...] = wg_ref[...].astype(BF16)
    wd_b[...] = wd_ref[...].astype(BF16)
    gpad[0:base, :] = jnp.zeros((base, tf), F32)
    gpad[base + tokens:base + tokens + base, :] = jnp.zeros((base, tf), F32)

    wdw = wdw_ref[...]
    bdw = bdw_ref[...]
    ci = lax.broadcasted_iota(jnp.int32, (cols, 1), 0)
    first = ci == 0
    last = ci == cols - 1
    taps_y = (-1, 0, 1) if rows > 1 else (0,)

    def up(b):
        hb = h_ref[b * blk:(b + 1) * blk, :]
        a_s[b * blk:(b + 1) * blk, :] = _dot(hb, wa_b[...])
        gpad[base + b * blk:base + (b + 1) * blk, :] = _dot(hb, wg_b[...])

    def conv_gate(b):
        for r in range(b * blk // cols, (b + 1) * blk // cols):
            off = r * cols
            for lb in range(tf // LANES):
                ls = slice(lb * LANES, (lb + 1) * LANES)
                left = mid = right = None
                for dy in taps_y:
                    r0 = base + off + dy * cols
                    kk = (dy + 1) * 3
                    tl = gpad[r0 - 1:r0 - 1 + cols, ls] * wdw[kk:kk + 1, ls]
                    tc = gpad[r0:r0 + cols, ls] * wdw[kk + 1:kk + 2, ls]
                    tr = gpad[r0 + 1:r0 + 1 + cols, ls] * wdw[kk + 2:kk + 3, ls]
                    left = tl if left is None else left + tl
                    mid = tc if mid is None else mid + tc
                    right = tr if right is None else right + tr
                y = mid + jnp.where(first, 0.0, left) + jnp.where(last, 0.0, right) + bdw[:, ls]
                u_s[off:off + cols, ls] = (_gelu_tanh(y) * a_s[off:off + cols, ls]).astype(BF16)

    def down(b):
        out_ref[b * blk:(b + 1) * blk, :] += _dot(u_s[b * blk:(b + 1) * blk, :], wd_b[...])

    for step in range(nb + 2):
        if step < nb:
            up(step)
        if 1 <= step <= nb:
            conv_gate(step - 1)
        if step >= 2:
            down(step - 2)

    @pl.when(j == pl.num_programs(1) - 1)
    def _():
        gate = mod_ref[5:6, :]
        lng = lng_ref[...]
        lnb = lnb_ref[...]

        def ln_body(r, carry):
            sl = pl.ds(pl.multiple_of(r * LN_ROWS, LN_ROWS), LN_ROWS)
            z = alpha * x1_ref[sl, :] + gate * out_ref[sl, :]
            out_ref[sl, :] = _layer_norm(z) * lng + lnb
            return carry

        lax.fori_loop(0, tokens // LN_ROWS, ln_body, 0)


def _ffn(h2, x1, mod, cond_row, layer, w_up, w_dw, b_dw, w_down, ln_g, ln_b, rows, cols, alpha):
    total, d = x1.shape
    d_ff = w_down.shape[1]
    tm = FFN_TOKENS
    tf = FFN_COLS
    nj = d_ff // tf
    kern = functools.partial(_ffn_kernel, rows=rows, cols=cols, tokens=tm, alpha=alpha)
    return pl.pallas_call(
        kern,
        out_shape=jax.ShapeDtypeStruct((total, d), F32),
        grid=(total // tm, nj),
        in_specs=[
            pl.BlockSpec((tm, d), lambda i, j: (i, 0)),
            pl.BlockSpec((tm, d), lambda i, j: (i, 0)),
            pl.BlockSpec((None, 6, d), lambda i, j: (cond_row(i * tm), 0, 0)),
            pl.BlockSpec((None, d, tf), lambda i, j: (layer, 0, j)),
            pl.BlockSpec((None, d, tf), lambda i, j: (layer, 0, nj + j)),
            pl.BlockSpec((9, tf), lambda i, j: (0, j)),
            pl.BlockSpec((1, tf), lambda i, j: (0, j)),
            pl.BlockSpec((None, tf, d), lambda i, j: (layer, j, 0)),
            pl.BlockSpec((1, d), lambda i, j: (0, 0)),
            pl.BlockSpec((1, d), lambda i, j: (0, 0)),
        ],
        out_specs=pl.BlockSpec((tm, d), lambda i, j: (i, 0)),
        scratch_shapes=[
            pltpu.VMEM((tm + 2 * ((cols if rows > 1 else 0) + SUBLANES), tf), F32),
            pltpu.VMEM((tm, tf), F32),
            pltpu.VMEM((tm, tf), BF16),
            pltpu.VMEM((d, tf), BF16), pltpu.VMEM((d, tf), BF16), pltpu.VMEM((tf, d), BF16),
        ],
        compiler_params=_params("parallel", "arbitrary"),
        name="ffn",
    )(h2, x1, mod, w_up, w_up, w_dw, b_dw, w_down, ln_g, ln_b)


def _trunk_layer(x, batch, rows, cols, mod, cond_row, p, pmat, alpha, dh, **state_kw):
    seq = rows * cols
    hp = _mlstm_heads_per_step(seq, dh)
    proj, v_t, gates_t = _inproj(x, mod, cond_row, p["w_in_main"], p["w_v_t"], p["w_gate_t"][hp], p["b_gate"][hp])
    pool_out = _pool(proj, pmat, p["w_pool"], p["pool_scale"], rows, cols, min(FFN_TOKENS, batch * seq))
    ml_out, states = _mlstm(proj, v_t, gates_t, p["w_qk_conv"], p["hn_g"], batch, seq, N_POOL_GROUPS * LANES, dh,
                            **state_kw)
    x1, h2 = _outproj(pool_out, ml_out, x, mod, cond_row, p["w_out"], p["ln1_g"], p["ln1_b"], alpha)
    x2 = _ffn(h2, x1, mod, cond_row, p["layer"], p["w_up"], p["w_dw"], p["b_dw"], p["w_down"], p["ln2_g"],
              p["ln2_b"], rows, cols, alpha)
    return x2, states


def kernel(x_prompt, x_sample, c, state_C, state_n, state_m, c_ctx, w_ada, b_ada, w_in, b_gates, w_qk_conv, hn_g,
           w_pool, pool_scale, w_out, ln1_g, ln1_b, w_up, w_dw, b_dw, w_down, ln2_g, ln2_b):
    b_p, l_p, d = x_prompt.shape
    b_s, l_s, _ = x_sample.shape
    depth = w_ada.shape[0]
    heads = MLSTM_HEADS
    dh = state_C.shape[-1]
    pool_w = N_POOL_GROUPS * LANES
    mw = heads * dh
    n_main = pool_w + 4 * mw
    v0 = pool_w + 2 * mw
    d_ff = w_down.shape[1]
    alpha = (2.0 * depth) ** 0.25
    rows_s = l_s // GRID_W
    assert b_s + 1 <= COND_ROWS and l_p % POOL_TILE == 0 and POOL_TILE % GRID_W == 0

    cond = jnp.concatenate([c_ctx[None, :], c, jnp.zeros((COND_ROWS - 1 - b_s, d), F32)], axis=0)
    ada = _ada(cond, w_ada, b_ada).reshape(depth, COND_ROWS, 6, d)

    pmat_p = _pool_col_matrices(l_p)
    pmat_s = _pool_col_matrices(GRID_W)
    assert min(FFN_TOKENS, b_s * l_s) == l_s
    hps = sorted({_mlstm_heads_per_step(l_p, dh), _mlstm_heads_per_step(l_s, dh)})

    y_p = x_prompt.reshape(b_p * l_p, d)
    y_s = x_sample.reshape(b_s * l_s, d)
    states = None
    for l in range(depth):
        w_gate = w_in[l, :, n_main:]
        p = dict(
            w_in_main=jnp.concatenate([w_in[l, :, :v0], w_in[l, :, v0 + mw:n_main]], axis=1).astype(BF16),
            w_v_t=w_in[l, :, v0:v0 + mw].T.astype(BF16),
            w_gate_t={hp: w_gate[:, _gate_row_order(hp)].T.astype(BF16) for hp in hps},
            b_gate={hp: b_gates[l][_gate_row_order(hp)][:, None] for hp in hps},
            w_qk_conv=w_qk_conv[l], hn_g=hn_g[l][None, :],
            w_pool=w_pool[l].astype(BF16), pool_scale=pool_scale[l][None, :],
            w_out=w_out[l].astype(BF16), ln1_g=ln1_g[l][None, :], ln1_b=ln1_b[l][None, :],
            layer=l, w_up=w_up,
            w_dw=w_dw[l].reshape(9, d_ff), b_dw=b_dw[l][None, :],
            w_down=w_down, ln2_g=ln2_g[l][None, :], ln2_b=ln2_b[l][None, :])
        y_p, states = _trunk_layer(y_p, b_p, 1, l_p, ada[l], lambda t: 0, p, pmat_p, alpha, dh,
                                   layer=l, depth=depth, prev=states)
        init = (state_C[:, l], state_n[:, l].reshape(b_s, 2, heads, 1, dh), state_m[:, l].reshape(b_s, 2, heads, 1, 1))
        y_s, _ = _trunk_layer(y_s, b_s, rows_s, GRID_W, ada[l], lambda t: 1 + t // l_s, p, pmat_s, alpha, dh,
                              init_state=init)
    new_c, new_n, new_m = states
    return (y_p.reshape(b_p, l_p, d), y_s.reshape(b_s, l_s, d), new_c, new_n.reshape(b_p, depth, 2, heads, dh),
            new_m.reshape(b_p, depth, 2, heads))
```

```python
import functools
import math

import numpy as np
import jax
import jax.numpy as jnp
from jax import lax
from jax.experimental import pallas as pl
from jax.experimental.pallas import tpu as pltpu

F32 = jnp.float32
BF16 = jnp.bfloat16

GRID_W = 64
POOL_WINDOWS = (2, 4, 8, 16)
N_POOL_GROUPS = len(POOL_WINDOWS)
MLSTM_HEADS = 4
LN_EPS = 1e-5

LANES = 128
SUBLANES = 8
BF16_ROWS = 16
VMEM_LIMIT_BYTES = 56 * 1024 * 1024

COND_ROWS = SUBLANES
MLSTM_CHUNK = LANES
MLSTM_UNROLL_CHUNKS = 2
MLSTM_STEP_BYTES =2 * 1024 * 1024
POOL_TILE = 256
TOKEN_TILE = 1024
TOKEN_SUB = 256
FFN_TOKENS = 2048
FFN_BLOCK = 512
FFN_COLS = 256
LN_ROWS = 256


def _params(*sem):
    return pltpu.CompilerParams(dimension_semantics=sem, vmem_limit_bytes=VMEM_LIMIT_BYTES)


def _layer_norm(x):
    mu = jnp.mean(x, axis=-1, keepdims=True)
    xc = x - mu
    var = jnp.mean(xc * xc, axis=-1, keepdims=True)
    return xc * lax.rsqrt(var + LN_EPS)


def _dot(a, b):
    return jnp.dot(a, b, preferred_element_type=F32)


def _dot_nt(a, b):
    return lax.dot_general(a, b, (((1,), (1,)), ((), ())), preferred_element_type=F32)


def _ada_kernel(cond_ref, w_ref, b_ref, out_ref):
    cnd = cond_ref[...]
    act = (cnd * jax.nn.sigmoid(cnd)).astype(BF16)
    out_ref[...] = _dot(act, w_ref[...].astype(BF16)) + b_ref[...]


def _ada(cond, w_ada, b_ada):
    depth, d, n = w_ada.shape
    tn = 1536
    return pl.pallas_call(
        _ada_kernel,
        out_shape=jax.ShapeDtypeStruct((depth, COND_ROWS, n), F32),
        grid=(depth, n // tn),
        in_specs=[
            pl.BlockSpec((COND_ROWS, d), lambda l, j: (0, 0)),
            pl.BlockSpec((None, d, tn), lambda l, j: (l, 0, j)),
            pl.BlockSpec((None, 1, tn), lambda l, j: (l, 0, j)),
        ],
        out_specs=pl.BlockSpec((None, COND_ROWS, tn), lambda l, j: (l, 0, j)),
        compiler_params=_params("parallel", "parallel"),
        name="ada",
    )(cond, w_ada, b_ada.reshape(depth, 1, n))


def _inproj_kernel(x_ref, mod_ref, w_ref, wvt_ref, wgt_ref, bg_ref, proj_ref, vt_ref, gates_ref):
    sub = TOKEN_SUB
    for s in range(x_ref.shape[0] // sub):
        rs = slice(s * sub, (s + 1) * sub)
        h = _layer_norm(x_ref[rs, :]) * (1.0 + mod_ref[1:2, :]) + mod_ref[0:1, :]
        h = h.astype(BF16)
        proj_ref[rs, :] = _dot(h, w_ref[...])
        v_t = _dot_nt(wvt_ref[...], h).astype(BF16)
        gates_t = _dot_nt(wgt_ref[...], h) + bg_ref[...]
        for k in range(sub // LANES):
            vt_ref[s * (sub // LANES) + k] = v_t[:, k * LANES:(k + 1) * LANES]
            gates_ref[s * (sub // LANES) + k] = gates_t[:, k * LANES:(k + 1) * LANES]


def _inproj(x, mod, cond_row, layer, w_main, w_v_t, w_gate_t, b_gate):
    tokens, d = x.shape
    n = w_main.shape[2]
    nv = w_v_t.shape[1]
    ng = w_gate_t.shape[1]
    tm = TOKEN_TILE
    return pl.pallas_call(
        _inproj_kernel,
        out_shape=(jax.ShapeDtypeStruct((tokens, n), F32), jax.ShapeDtypeStruct((tokens // LANES, nv, LANES), BF16),
                   jax.ShapeDtypeStruct((tokens // LANES, ng, LANES), F32)),
        grid=(tokens // tm,),
        in_specs=[
            pl.BlockSpec((tm, d), lambda i: (i, 0)),
            pl.BlockSpec((None, None, 6, d), lambda i: (layer, cond_row(i * tm), 0, 0)),
            pl.BlockSpec((None, d, n), lambda i: (layer, 0, 0)),
            pl.BlockSpec((None, nv, d), lambda i: (layer, 0, 0)),
            pl.BlockSpec((None, ng, d), lambda i: (layer, 0, 0)),
            pl.BlockSpec((None, ng, 1), lambda i: (layer, 0, 0)),
        ],
        out_specs=(pl.BlockSpec((tm, n), lambda i: (i, 0)),
                   pl.BlockSpec((tm // LANES, nv, LANES), lambda i: (i, 0, 0)),
                   pl.BlockSpec((tm // LANES, ng, LANES), lambda i: (i, 0, 0))),
        compiler_params=_params("parallel"),
        name="inproj",
    )(x, mod, w_main, w_v_t, w_gate_t, b_gate)


def _pool_col_matrices(cols):
    t = np.arange(POOL_TILE)
    row, col = t // cols, t % cols
    mats = []
    for win in POOL_WINDOWS:
        hw = win // 2
        lo = np.maximum(col - hw, 0)
        hi = np.minimum(col + hw, cols)
        same_row = row[:, None] == row[None, :]
        inside = (col[None, :] >= lo[:, None]) & (col[None, :] < hi[:, None])
        mats.append((same_row & inside).astype(np.float32))
    return jnp.asarray(np.stack(mats), dtype=BF16)


def _pool_kernel(xp_ref, pmat_ref, wp_ref, ps_ref, out_ref, pad_ref, hl_s, y_s, *, rows, cols, tokens):
    gdim = LANES
    tile = POOL_TILE
    halo = (max(POOL_WINDOWS) // 2) * cols if rows > 1 else 0
    t = lax.broadcasted_iota(jnp.int32, (tile, 1), 0)
    col = t & (cols - 1)
    if rows > 1:
        zeros = jnp.zeros((halo, gdim), F32)
        pad_ref[0:halo, :] = zeros
        pad_ref[halo + tokens:halo + tokens + halo, :] = zeros
    for gi, win in enumerate(POOL_WINDOWS):
        hw = win // 2
        gs = slice(gi * gdim, (gi + 1) * gdim)
        if rows > 1:
            pad_ref[halo:halo + tokens, :] = xp_ref[:, gs]
        cnt_c = (jnp.minimum(col + hw, cols) - jnp.maximum(col - hw, 0)).astype(F32)
        pm = pmat_ref[gi]
        wp = wp_ref[gi]
        scale = ps_ref[:, gs]

        nt = tokens // tile
        for tt in range(nt):
            off = tt * tile
            if rows > 1:
                acc = None
                for dr in range(-hw, hw):
                    term = pad_ref[off + halo + dr * cols:off + halo + dr * cols + tile, :]
                    acc = term if acc is None else acc + term
                rw = (off + t) >> int(math.log2(cols))
                cnt_r = (jnp.minimum(rw + hw, rows) - jnp.maximum(rw - hw, 0)).astype(F32)
                m1 = acc / cnt_r
            else:
                m1 = xp_ref[off:off + tile, gs]
            hi = m1.astype(BF16)
            hl_s[gi * nt + tt] = jnp.concatenate([hi, (m1 - hi.astype(F32)).astype(BF16)], axis=1)
        for tt in range(nt):
            off = tt * tile
            both = _dot(pm, hl_s[gi * nt + tt])
            m2 = (both[:, :gdim] + both[:, gdim:]) / cnt_c
            y_s[gi, off:off + tile, :] = (m2 - xp_ref[off:off + tile, gs]).astype(BF16)
        out_ref[:, gs] = (_dot(y_s[gi], wp) * scale).astype(BF16)


def _pool(proj, pmat, layer, w_pool, pool_scale, rows, cols, step_tokens):
    total = proj.shape[0]
    pool_w = N_POOL_GROUPS * LANES
    halo = (max(POOL_WINDOWS) // 2) * cols if rows > 1 else 0
    kern = functools.partial(_pool_kernel, rows=rows, cols=cols, tokens=step_tokens)
    return pl.pallas_call(
        kern,
        out_shape=jax.ShapeDtypeStruct((total, pool_w), BF16),
        grid=(total // step_tokens,),
        in_specs=[
            pl.BlockSpec((step_tokens, pool_w), lambda i: (i, 0)),
            pl.BlockSpec(pmat.shape, lambda i: (0, 0, 0)),
            pl.BlockSpec((None,) + w_pool.shape[1:], lambda i: (layer, 0, 0, 0)),
            pl.BlockSpec((None, 1, pool_w), lambda i: (layer, 0, 0)),
        ],
        out_specs=pl.BlockSpec((step_tokens, pool_w), lambda i: (i, 0)),
        scratch_shapes=[
            pltpu.VMEM((step_tokens + 2 * halo if rows > 1 else SUBLANES, LANES), F32),
            pltpu.VMEM((N_POOL_GROUPS * (step_tokens // POOL_TILE), POOL_TILE, 2 * LANES), BF16),
            pltpu.VMEM((N_POOL_GROUPS, step_tokens, LANES), BF16),
        ],
        compiler_params=_params("parallel"),
        name="pool",
    )(proj, pmat, w_pool, pool_scale)


def _log_sigmoid(x):
    return jnp.minimum(x, 0.0) - jnp.log(1.0 + jnp.exp(-jnp.abs(x)))


def _split3(x):
    hi = x.astype(BF16)
    r1 = x - hi.astype(F32)
    mid = r1.astype(BF16)
    lo = (r1 - mid.astype(F32)).astype(BF16)
    return lo, mid, hi


def _mlstm_kernel(*refs, seq, chunk, hp, zero_init, emit_state, n_unused, state_slot):
    refs = list(refs)
    q_ref, k_ref, vt_ref, o_ref, g_ref, wq_ref, wk_ref, hng_ref = refs[:8]
    pos = 8
    if not zero_init:
        c0_ref, n0_ref, m0_ref = refs[pos:pos + 3]
        pos += 3
    pos += n_unused
    out_ref = refs[pos]
    pos += 1
    if emit_state:
        c_ref, n_ref, m_ref = refs[pos:pos + 3]
        pos += 3
    qpad, kpad, qs, ks, ht_f, ht_b, rcb_s, br_s, rc_s, ct_s, dt_s, mrow_s, z1_s, rhs_s, z2_s = refs[pos:]
    dh = q_ref.shape[1] // hp
    nc = seq // chunk
    npair = 2 * hp
    aug = dh + BF16_ROWS
    margin = SUBLANES
    unroll = nc <= MLSTM_UNROLL_CHUNKS

    for pad_ref, src_ref in ((qpad, q_ref), (kpad, k_ref)):
        pad_ref[0:margin, :] = jnp.zeros((margin, hp * dh), F32)
        pad_ref[margin + seq:2 * margin + seq, :] = jnp.zeros((margin, hp * dh), F32)
        pad_ref[margin:margin + seq, :] = src_ref[...]

    def conv_silu(pad_ref, w_ref, c, hs):
        w = w_ref[:, hs]
        start = c * chunk if unroll else pl.multiple_of(c * chunk, chunk)
        win = pad_ref[pl.ds(start, chunk + 2 * margin), hs]
        y = (pltpu.roll(win, 1, 0) * w[0:1, :] + win * w[1:2, :]
             + pltpu.roll(win, chunk + 2 * margin - 1, 0) * w[2:3, :])[margin:margin + chunk]
        return y * jax.nn.sigmoid(y)

    ri = lax.broadcasted_iota(jnp.int32, (chunk, chunk), 0)
    ci = lax.broadcasted_iota(jnp.int32, (chunk, chunk), 1)
    lower = ci <= ri
    upper = ci >= ri
    upper_b = jnp.where(upper, 1.0, 0.0).astype(BF16)
    row_is_fwd = lax.broadcasted_iota(jnp.int32, (npair, chunk), 0) < hp
    ones_row = jnp.where(lax.broadcasted_iota(jnp.int32, (BF16_ROWS, chunk), 0) == 0, 1.0, 0.0).astype(BF16)

    gates = g_ref[...]
    lf = _log_sigmoid(gates[:, npair:, :])
    lf3 = _split3(lf.reshape(nc * npair, chunk))
    prefix = (_dot(lf3[0], upper_b) + _dot(lf3[1], upper_b) + _dot(lf3[2], upper_b)).reshape(nc, npair, chunk)
    suffix = prefix[:, :, chunk - 1:chunk] - prefix + lf
    br = jnp.where(row_is_fwd, prefix, suffix)
    br_s[...] = br
    rc_s[...] = gates[:, :npair, :] - br

    def rows_of(c):
        return slice(c * chunk, (c + 1) * chunk) if unroll else pl.ds(pl.multiple_of(c * chunk, chunk), chunk)

    def prep(c, carry):
        sl = rows_of(c)
        rc = rc_s[c]
        for p in range(npair):
            rcb_s[sl, p * LANES:(p + 1) * LANES] = jnp.transpose(jnp.broadcast_to(rc[p:p + 1, :], (LANES, chunk)))
        for hh in range(hp):
            hs = slice(hh * dh, (hh + 1) * dh)
            qs[sl, hs] = conv_silu(qpad, wq_ref, c, hs).astype(BF16)
            ks[sl, hs] = (conv_silu(kpad, wk_ref, c, hs) * (dh ** -0.5)).astype(BF16)
        return carry

    n_rows =lax.broadcasted_iota(jnp.int32, (BF16_ROWS, dh), 0) == 0
    ms = []
    for p in range(npair):
        di, hh = divmod(p, hp)
        if zero_init:
            ct_s[p] = jnp.zeros((aug, dh), F32)
            ms.append(jnp.zeros((1, 1), F32))
        else:
            ct_s[p, 0:dh, :] = jnp.transpose(c0_ref[di, hh])
            ct_s[p, dh:aug, :] = jnp.where(n_rows, n0_ref[di, hh], 0.0)
            ms.append(m0_ref[di, hh])

    def pair_args(p, i):
        di, hh = divmod(p, hp)
        fwd = di == 0
        c = i if fwd else nc - 1 - i
        last = chunk - 1 if fwd else 0
        sl = rows_of(c)
        return fwd, hh, c, last, sl, slice(hh * dh, (hh + 1) * dh)

    def body(i, ms):
        for p in range(npair):
            fwd, hh, c, last, sl, hs = pair_args(p, i)
            rcb = rcb_s[sl, p * LANES:(p + 1) * LANES]
            xt = jnp.where(upper if fwd else lower, rcb, -jnp.inf)
            mrow = jnp.maximum(jnp.max(xt, axis=0, keepdims=True), ms[p])
            dt_s[p] = jnp.exp(xt - mrow)
            mrow_s[p] = jnp.broadcast_to(mrow, (SUBLANES, chunk))
            w = jnp.exp(rcb - mrow[:, last:last + 1])
            rhs_s[p, :, chunk:] = (w * ks[sl, hs].astype(F32)).astype(BF16)
        zeros_q = jnp.zeros((chunk, dh), BF16)
        for pk in range(npair // 2):
            _, _, _, _, sl_a, hs_a = pair_args(2 * pk, i)
            _, _, _, _, sl_b, hs_b = pair_args(2 * pk + 1, i)
            lhs = jnp.concatenate([
                jnp.concatenate([ks[sl_a, hs_a], ks[sl_b, hs_b]], axis=1),
                jnp.concatenate([ct_s[2 * pk].astype(BF16), ct_s[2 * pk + 1].astype(BF16)], axis=1)], axis=0)
            q_diag = jnp.concatenate([
                jnp.concatenate([qs[sl_a, hs_a], zeros_q], axis=1),
                jnp.concatenate([zeros_q, qs[sl_b, hs_b]], axis=1)], axis=0)
            z1_s[pk] = _dot_nt(lhs, q_diag)
        for p in range(npair):
            lanes = slice((p % 2) * chunk, (p % 2 + 1) * chunk)
            rhs_s[p, :, 0:chunk] = (z1_s[p // 2, 0:chunk, lanes] * dt_s[p]).astype(BF16)
        for p in range(npair):
            fwd, hh, c, last, sl, hs = pair_args(p, i)
            z2_s[p] = _dot(jnp.concatenate([vt_ref[c, hs, :], ones_row], axis=0), rhs_s[p])
        new_ms = []
        for p in range(npair):
            fwd, hh, c, last, sl, hs = pair_args(p, i)
            m = ms[p]
            mrow = mrow_s[p, 0:1, :]
            lanes = slice((p % 2) * chunk, (p % 2 + 1) * chunk)
            num = z1_s[p // 2, chunk:, lanes] * jnp.exp(m - mrow) + z2_s[p, :, 0:chunk]
            b_r = br_s[c, p:p + 1, :]
            scale = 1.0 / jnp.maximum(jnp.abs(num[dh:dh + 1, :]), jnp.exp(-(b_r + mrow)))
            h_t = num[0:dh, :] * scale
            if fwd:
                ht_f[hh * nc + c] = h_t
            else:
                ht_b[hh * nc + c] = h_t
            mlast = mrow[:, last:last + 1]
            ct_s[p] = jnp.exp(m - mlast) * ct_s[p] + z2_s[p, :, chunk:]
            new_ms.append(b_r[:, last:last + 1] + mlast)
        return tuple(new_ms)

    def finish(c, carry):
        sl = rows_of(c)
        for hh in range(hp):
            hs = slice(hh * dh, (hh + 1) * dh)
            h_t = ht_f[hh * nc + c] + ht_b[hh * nc + c]
            mu = jnp.mean(h_t, axis=0, keepdims=True)
            hc = h_t - mu
            var = jnp.mean(hc * hc, axis=0, keepdims=True)
            hn = jnp.transpose(hc * lax.rsqrt(var + LN_EPS)) * hng_ref[:, hs]
            out_ref[sl, hs] = (jax.nn.sigmoid(o_ref[sl, hs]) * hn).astype(BF16)
        return carry

    if unroll:
        for c in range(nc):
            prep(c, 0)
        ms = tuple(ms)
        for i in range(nc):
            ms = body(i, ms)
        for c in range(nc):
            finish(c, 0)
    else:
        half = nc // 2
        prep(0, 0)
        prep(nc - 1, 0)

        def first_half(i, ms):
            ms = body(i, ms)
            prep(i + 1, 0)
            prep(nc - 2 - i, 0)
            return ms

        def second_half(i, ms):
            ms = body(i, ms)
            finish(i, 0)
            finish(nc - 1 - i, 0)
            return ms

        ms = lax.fori_loop(0, half, first_half, tuple(ms))
        ms = lax.fori_loop(half, nc, second_half, ms)

    if emit_state:
        for slot in range(c_ref.shape[0]):
            if slot != state_slot:
                c_ref[slot] = jnp.zeros(c_ref.shape[1:], F32)
                n_ref[slot] = jnp.zeros(n_ref.shape[1:], F32)
                m_ref[slot] = jnp.zeros(m_ref.shape[1:], F32)
        for p in range(npair):
            di, hh = divmod(p, hp)
            c_ref[state_slot, di, hh] = jnp.transpose(ct_s[p, 0:dh, :])
            n_ref[state_slot, di, hh] = ct_s[p, dh:dh + 1, :]
            m_ref[state_slot, di, hh] = ms[p]


def _mlstm_heads_per_step(seq, dh):
    return max(1, min(MLSTM_HEADS, MLSTM_STEP_BYTES // (seq * dh * 4)))


def _gate_row_order(hp):
    heads = MLSTM_HEADS
    order = []
    for g in range(heads // hp):
        for kind in range(2):
            for di in range(2):
                for hh in range(hp):
                    order.append((di * 2 + kind) * heads + g * hp + hh)
    return np.asarray(order)


def _mlstm(proj, v_t, gates_t, wl, w_qk_conv, hn_g, batch, seq, col0, dh, init_state=None, layer=None, depth=None,
           prev=None):
    heads = MLSTM_HEADS
    chunk = MLSTM_CHUNK
    assert chunk == LANES and seq % chunk == 0 and (seq // chunk <= MLSTM_UNROLL_CHUNKS or seq // chunk % 2 == 0)
    nc = seq // chunk
    hp = _mlstm_heads_per_step(seq, dh)
    hg = heads // hp
    npair = 2 * hp
    aug = dh + BF16_ROWS
    width = hp * dh
    cb = col0 // width
    gb = heads * dh // width
    zero_init = init_state is None
    emit_state = layer is not None
    prev = () if prev is None else tuple(prev)
    slots, first_slot = (1, layer) if prev else (depth, 0)
    kern = functools.partial(_mlstm_kernel, seq=seq, chunk=chunk, hp=hp, zero_init=zero_init, emit_state=emit_state,
                             n_unused=len(prev), state_slot=(layer - first_slot) if emit_state else None)

    def tok_spec(group):
        return pl.BlockSpec((seq, width), lambda b, g: (b, cb + group * gb + g))

    in_specs = [
        tok_spec(0), tok_spec(1), pl.BlockSpec((nc, width, chunk), lambda b, g: (b, g, 0)), tok_spec(2),
        pl.BlockSpec((nc, 2 * npair, chunk), lambda b, g: (b, g, 0)),
        pl.BlockSpec((None, 3, width), lambda b, g: (wl, 0, g)),
        pl.BlockSpec((None, 3, width), lambda b, g: (wl, 0, gb + g)),
        pl.BlockSpec((None, 1, width), lambda b, g: (wl, 0, g)),
    ]
    args = [proj, proj, v_t, proj, gates_t, w_qk_conv, w_qk_conv, hn_g]
    if not zero_init:
        in_specs += [pl.BlockSpec((None, None, 2, hp, dh, dh), lambda b, g: (b, wl, 0, g, 0, 0)),
                     pl.BlockSpec((None, None, 2, hp, 1, dh), lambda b, g: (b, wl, 0, g, 0, 0)),
                     pl.BlockSpec((None, None, 2, hp, 1, 1), lambda b, g: (b, wl, 0, g, 0, 0))]
        args += list(init_state)
    aliases = {}
    for k, arr in enumerate(prev):
        aliases[len(args)] = 1 + k
        in_specs.append(pl.BlockSpec(memory_space=pl.ANY))
        args.append(arr)
    out_shape = [jax.ShapeDtypeStruct((batch * seq, heads * dh), BF16)]
    out_specs = [pl.BlockSpec((seq, width), lambda b, g: (b, g))]
    if emit_state:
        out_shape += [jax.ShapeDtypeStruct((batch, depth, 2, heads, dh, dh), F32),
                      jax.ShapeDtypeStruct((batch, depth, 2, heads, 1, dh), F32),
                      jax.ShapeDtypeStruct((batch, depth, 2, heads, 1, 1), F32)]
        out_specs += [pl.BlockSpec((None, slots, 2, hp, dh, dh), lambda b, g: (b, first_slot, 0, g, 0, 0)),
                      pl.BlockSpec((None, slots, 2, hp, 1, dh), lambda b, g: (b, first_slot, 0, g, 0, 0)),
                      pl.BlockSpec((None, slots, 2, hp, 1, 1), lambda b, g: (b, first_slot, 0, g, 0, 0))]
    res = pl.pallas_call(
        kern,
        out_shape=tuple(out_shape),
        grid=(batch, hg),
        in_specs=in_specs,
        out_specs=tuple(out_specs),
        scratch_shapes=[
            pltpu.VMEM((seq + 2 * SUBLANES, width), F32), pltpu.VMEM((seq + 2 * SUBLANES, width), F32),
            pltpu.VMEM((seq, width), BF16), pltpu.VMEM((seq, width), BF16),
            pltpu.VMEM((hp * nc, dh, chunk), F32), pltpu.VMEM((hp * nc, dh, chunk), F32),
            pltpu.VMEM((seq, npair * LANES), F32), pltpu.VMEM((nc, npair, chunk), F32),
            pltpu.VMEM((nc, npair, chunk), F32),
            pltpu.VMEM((npair, aug, dh), F32),
            pltpu.VMEM((npair, chunk, chunk), F32), pltpu.VMEM((npair, SUBLANES, chunk), F32),
            pltpu.VMEM((npair // 2, chunk + aug, 2 * chunk), F32), pltpu.VMEM((npair, chunk, chunk + dh), BF16),
            pltpu.VMEM((npair, aug, chunk + dh), F32),
        ],
        input_output_aliases=aliases,
        compiler_params=_params("parallel", "parallel"),
        name="mlstm",
    )(*args)
    return res[0], tuple(res[1:])


def _outproj_kernel(pool_ref, ml_ref, x_ref, mod_ref, wp_ref, wm_ref, g_ref, b_ref, x1_ref, h2_ref, *, alpha):
    sub = TOKEN_SUB
    for s in range(x_ref.shape[0] // sub):
        rs = slice(s * sub, (s + 1) * sub)
        mix = _dot(pool_ref[rs, :], wp_ref[...]) + _dot(ml_ref[rs, :], wm_ref[...])
        x1 = _layer_norm(alpha * x_ref[rs, :] + mod_ref[2:3, :] * mix) * g_ref[...] + b_ref[...]
        x1_ref[rs, :] = x1
        h2 = _layer_norm(x1) * (1.0 + mod_ref[4:5, :]) + mod_ref[3:4, :]
        h2_ref[rs, :] = h2.astype(BF16)


def _outproj(pool_out, ml_out, x, mod, cond_row, layer, w_out, ln_g, ln_b, alpha):
    tokens, d = x.shape
    half = pool_out.shape[1]
    tm = TOKEN_TILE
    kern = functools.partial(_outproj_kernel, alpha=alpha)
    return pl.pallas_call(
        kern,
        out_shape=(jax.ShapeDtypeStruct((tokens, d), F32), jax.ShapeDtypeStruct((tokens, d), BF16)),
        grid=(tokens // tm,),
        in_specs=[
            pl.BlockSpec((tm, half), lambda i: (i, 0)),
            pl.BlockSpec((tm, half), lambda i: (i, 0)),
            pl.BlockSpec((tm, d), lambda i: (i, 0)),
            pl.BlockSpec((None, None, 6, d), lambda i: (layer, cond_row(i * tm), 0, 0)),
            pl.BlockSpec((None, half, d), lambda i: (layer, 0, 0)),
            pl.BlockSpec((None, half, d), lambda i: (layer, 1, 0)),
            pl.BlockSpec((None, 1, d), lambda i: (layer, 0, 0)),
            pl.BlockSpec((None, 1, d), lambda i: (layer, 0, 0)),
        ],
        out_specs=(pl.BlockSpec((tm, d), lambda i: (i, 0)), pl.BlockSpec((tm, d), lambda i: (i, 0))),
        compiler_params=_params("parallel"),
        name="outproj",
    )(pool_out, ml_out, x, mod, w_out, w_out, ln_g, ln_b)


def _gelu_tanh(x):
    return 0.5 * x * (1.0 + jnp.tanh(math.sqrt(2.0 / math.pi) * (x + 0.044715 * (x * x * x))))


def _ffn_kernel(h_ref, x1_ref, mod_ref, wa_ref, wg_ref, wdw_ref, bdw_ref, wd_ref, lng_ref, lnb_ref, out_ref,
                gpad, a_s, u_s, wa_b, wg_b, wd_b, *, rows, cols, tokens, alpha):
    j = pl.program_id(1)
    tf = wa_ref.shape[1]
    blk = FFN_BLOCK
    nb = tokens // blk
    base = (cols if rows > 1 else 0) + SUBLANES

    @pl.when(j == 0)
    def _():
        out_ref[...] = jnp.zeros_like(out_ref)

    wa_b[...] = wa_ref[...].astype(BF16)
    wg_b[...] = wg_ref[...].astype(BF16)
    wd_b[...] = wd_ref[...].astype(BF16)
    gpad[0:base, :] = jnp.zeros((base, tf), F32)
    gpad[base + tokens:base + tokens + base, :] = jnp.zeros((base, tf), F32)

    wdw = wdw_ref[...]
    bdw = bdw_ref[...]
    ci = lax.broadcasted_iota(jnp.int32, (cols, 1), 0)
    first = ci == 0
    last = ci == cols - 1
    taps_y = (-1, 0, 1) if rows > 1 else (0,)

    def up(b):
        hb = h_ref[b * blk:(b + 1) * blk, :]
        a_s[b * blk:(b + 1) * blk, :] = _dot(hb, wa_b[...])
        gpad[base + b * blk:base + (b + 1) * blk, :] = _dot(hb, wg_b[...])

    def conv_gate(b):
        for r in range(b * blk // cols, (b + 1) * blk // cols):
            off = r * cols
            for lb in range(tf // LANES):
                ls = slice(lb * LANES, (lb + 1) * LANES)
                left = mid = right = None
                for dy in taps_y:
                    r0 = base + off + dy * cols
                    kk = (dy + 1) * 3
                    tl = gpad[r0 - 1:r0 - 1 + cols, ls] * wdw[kk:kk + 1, ls]
                    tc = gpad[r0:r0 + cols, ls] * wdw[kk + 1:kk + 2, ls]
                    tr = gpad[r0 + 1:r0 + 1 + cols, ls] * wdw[kk + 2:kk + 3, ls]
                    left = tl if left is None else left + tl
                    mid = tc if mid is None else mid + tc
                    right = tr if right is None else right + tr
                y = mid + jnp.where(first, 0.0, left) + jnp.where(last, 0.0, right) + bdw[:, ls]
                u_s[off:off + cols, ls] = (_gelu_tanh(y) * a_s[off:off + cols, ls]).astype(BF16)

    def down(b):
        out_ref[b * blk:(b + 1) * blk, :] += _dot(u_s[b * blk:(b + 1) * blk, :], wd_b[...])

    for step in range(nb + 2):
        if step < nb:
            up(step)
        if 1 <= step <= nb:
            conv_gate(step - 1)
        if step >= 2:
            down(step - 2)

    @pl.when(j == pl.num_programs(1) - 1)
    def _():
        gate = mod_ref[5:6, :]
        lng = lng_ref[...]
        lnb = lnb_ref[...]

        def ln_body(r, carry):
            sl = pl.ds(pl.multiple_of(r * LN_ROWS, LN_ROWS), LN_ROWS)
            z = alpha * x1_ref[sl, :] + gate * out_ref[sl, :]
            out_ref[sl, :] = _layer_norm(z) * lng + lnb
            return carry

        lax.fori_loop(0, tokens // LN_ROWS, ln_body, 0)


def _ffn(h2, x1, mod, cond_row, layer, w_up, w_dw, b_dw, w_down, ln_g, ln_b, rows, cols, alpha):
    total, d = x1.shape
    d_ff = w_down.shape[1]
    tm = FFN_TOKENS
    tf = FFN_COLS
    nj = d_ff // tf
    kern = functools.partial(_ffn_kernel, rows=rows, cols=cols, tokens=tm, alpha=alpha)
    return pl.pallas_call(
        kern,
        out_shape=jax.ShapeDtypeStruct((total, d), F32),
        grid=(total // tm, nj),
        in_specs=[
            pl.BlockSpec((tm, d), lambda i, j: (i, 0)),
            pl.BlockSpec((tm, d), lambda i, j: (i, 0)),
            pl.BlockSpec((None, None, 6, d), lambda i, j: (layer, cond_row(i * tm), 0, 0)),
            pl.BlockSpec((None, d, tf), lambda i, j: (layer, 0, j)),
            pl.BlockSpec((None, d, tf), lambda i, j: (layer, 0, nj + j)),
            pl.BlockSpec((None, 9, tf), lambda i, j: (layer, 0, j)),
            pl.BlockSpec((None, 1, tf), lambda i, j: (layer, 0, j)),
            pl.BlockSpec((None, tf, d), lambda i, j: (layer, j, 0)),
            pl.BlockSpec((None, 1, d), lambda i, j: (layer, 0, 0)),
            pl.BlockSpec((None, 1, d), lambda i, j: (layer, 0, 0)),
        ],
        out_specs=pl.BlockSpec((tm, d), lambda i, j: (i, 0)),
        scratch_shapes=[
            pltpu.VMEM((tm + 2 * ((cols if rows > 1 else 0) + SUBLANES), tf), F32),
            pltpu.VMEM((tm, tf), F32),
            pltpu.VMEM((tm, tf), BF16),
            pltpu.VMEM((d, tf), BF16), pltpu.VMEM((d, tf), BF16), pltpu.VMEM((tf, d), BF16),
        ],
        compiler_params=_params("parallel", "arbitrary"),
        name="ffn",
    )(h2, x1, mod, w_up, w_up, w_dw, b_dw, w_down, ln_g, ln_b)


def _trunk_layer(x, batch, rows, cols, cond_row, layer, p, pmat, alpha, dh, state_kw):
    seq = rows * cols
    hp = _mlstm_heads_per_step(seq, dh)
    mod = p["ada"]
    proj, v_t, gates_t = _inproj(x, mod, cond_row, layer, p["w_in_main"], p["w_v_t"], p["w_gate_t"][hp],
                                 p["b_gate"][hp])
    pool_out = _pool(proj, pmat, layer, p["w_pool"], p["pool_scale"], rows, cols, min(FFN_TOKENS, batch * seq))
    ml_out, states = _mlstm(proj, v_t, gates_t, layer, p["w_qk_conv"], p["hn_g"], batch, seq, N_POOL_GROUPS * LANES,
                            dh, **state_kw)
    x1, h2 = _outproj(pool_out, ml_out, x, mod, cond_row, layer, p["w_out"], p["ln1_g"], p["ln1_b"], alpha)
    x2 = _ffn(h2, x1, mod, cond_row, layer, p["w_up"], p["w_dw"], p["b_dw"], p["w_down"], p["ln2_g"], p["ln2_b"],
              rows, cols, alpha)
    return x2, states


def kernel(x_prompt, x_sample, c, state_C, state_n, state_m, c_ctx, w_ada, b_ada, w_in, b_gates, w_qk_conv, hn_g,
           w_pool, pool_scale, w_out, ln1_g, ln1_b, w_up, w_dw, b_dw, w_down, ln2_g, ln2_b):
    b_p, l_p, d = x_prompt.shape
    b_s, l_s, _ = x_sample.shape
    depth = w_ada.shape[0]
    heads = MLSTM_HEADS
    dh = state_C.shape[-1]
    pool_w = N_POOL_GROUPS * LANES
    mw = heads * dh
    n_main = pool_w + 4 * mw
    v0 = pool_w + 2 * mw
    d_ff = w_down.shape[1]
    alpha = (2.0 * depth) ** 0.25
    rows_s = l_s // GRID_W
    assert b_s + 1 <= COND_ROWS and l_p % POOL_TILE == 0 and POOL_TILE % GRID_W == 0

    cond = jnp.concatenate([c_ctx[None, :], c, jnp.zeros((COND_ROWS - 1 - b_s, d), F32)], axis=0)

    pmat_p = _pool_col_matrices(l_p)
    pmat_s = _pool_col_matrices(GRID_W)
    assert min(FFN_TOKENS, b_s * l_s) == l_s
    hps = sorted({_mlstm_heads_per_step(l_p, dh), _mlstm_heads_per_step(l_s, dh)})

    w_gate = w_in[:, :, n_main:]
    p = dict(
        ada=_ada(cond, w_ada, b_ada).reshape(depth, COND_ROWS, 6, d),
        w_in_main=jnp.concatenate([w_in[:, :, :v0], w_in[:, :, v0 + mw:n_main]], axis=2).astype(BF16),
        w_v_t=w_in[:, :, v0:v0 + mw].transpose(0, 2, 1).astype(BF16),
        w_gate_t={hp: w_gate[:, :, _gate_row_order(hp)].transpose(0, 2, 1).astype(BF16) for hp in hps},
        b_gate={hp: b_gates[:, _gate_row_order(hp)][:, :, None] for hp in hps},
        w_qk_conv=w_qk_conv, hn_g=hn_g[:, None, :],
        w_pool=w_pool.astype(BF16), pool_scale=pool_scale[:, None, :],
        w_out=w_out.astype(BF16), ln1_g=ln1_g[:, None, :], ln1_b=ln1_b[:, None, :],
        w_up=w_up, w_dw=w_dw.reshape(depth, 9, d_ff), b_dw=b_dw[:, None, :],
        w_down=w_down, ln2_g=ln2_g[:, None, :], ln2_b=ln2_b[:, None, :])
    init = (state_C, state_n.reshape(b_s, depth, 2, heads, 1, dh), state_m.reshape(b_s, depth, 2, heads, 1, 1))

    y_p = x_prompt.reshape(b_p * l_p, d)
    y_s = x_sample.reshape(b_s * l_s, d)
    states = None
    for l in range(depth):
        y_p, states = _trunk_layer(y_p, b_p, 1, l_p, lambda t: 0, l, p, pmat_p, alpha, dh,
                                   dict(layer=l, depth=depth, prev=states))
        y_s, _ = _trunk_layer(y_s, b_s, rows_s, GRID_W, lambda t: 1 + t // l_s, l, p, pmat_s, alpha, dh,
                              dict(init_state=init))
    new_c, new_n, new_m = states
    return (y_p.reshape(b_p, l_p, d), y_s.reshape(b_s, l_s, d), new_c, new_n.reshape(b_p, depth, 2, heads, dh),
            new_m.reshape(b_p, depth, 2, heads))
```

```python
import functools
import math

import numpy as np
import jax
import jax.numpy as jnp
from jax import lax
from jax.experimental import pallas as pl
from jax.experimental.pallas import tpu as pltpu

F32 = jnp.float32
BF16 = jnp.bfloat16

GRID_W = 64
POOL_WINDOWS = (2, 4, 8, 16)
N_POOL_GROUPS = len(POOL_WINDOWS)
MLSTM_HEADS = 4
LN_EPS = 1e-5

LANES = 128
SUBLANES = 8
BF16_ROWS = 16
VMEM_LIMIT_BYTES = 56 * 1024 * 1024

COND_ROWS = SUBLANES
MLSTM_CHUNK = LANES
MLSTM_UNROLL_CHUNKS = 2
MLSTM_STEP_BYTES = 2 * 1024 * 1024
POOL_TILE = 256
TOKEN_TILE = 1024
TOKEN_SUB = 256
FFN_TOKENS = 2048
FFN_BLOCK = 512
FFN_COLS = 256
LN_ROWS = 256


def _params(*sem):
    return pltpu.CompilerParams(dimension_semantics=sem, vmem_limit_bytes=VMEM_LIMIT_BYTES)


def _layer_norm(x):
    mu = jnp.mean(x, axis=-1, keepdims=True)
    xc = x - mu
    var = jnp.mean(xc * xc, axis=-1, keepdims=True)
    return xc * lax.rsqrt(var + LN_EPS)


def _dot(a, b):
    return jnp.dot(a, b, preferred_element_type=F32)


def _dot_nt(a, b):
    return lax.dot_general(a, b, (((1,), (1,)), ((), ())), preferred_element_type=F32)


def _ada_kernel(cond_ref, w_ref, b_ref, out_ref):
    cnd = cond_ref[...]
    act = (cnd * jax.nn.sigmoid(cnd)).astype(BF16)
    out_ref[...] = _dot(act, w_ref[...].astype(BF16)) + b_ref[...]


def _ada(cond, w_ada, b_ada):
    depth, d, n = w_ada.shape
    tn = 1536
    return pl.pallas_call(
        _ada_kernel,
        out_shape=jax.ShapeDtypeStruct((depth, COND_ROWS, n), F32),
        grid=(depth, n // tn),
        in_specs=[
            pl.BlockSpec((COND_ROWS, d), lambda l, j: (0, 0)),
            pl.BlockSpec((None, d, tn), lambda l, j: (l, 0, j)),
            pl.BlockSpec((None, 1, tn), lambda l, j: (l, 0, j)),
        ],
        out_specs=pl.BlockSpec((None, COND_ROWS, tn), lambda l, j: (l, 0, j)),
        compiler_params=_params("parallel", "parallel"),
        name="ada",
    )(cond, w_ada, b_ada.reshape(depth, 1, n))


def _inproj_kernel(x_ref, mod_ref, w_ref, wvt_ref, wgt_ref, bg_ref, proj_ref, vt_ref, gates_ref):
    sub = TOKEN_SUB
    for s in range(x_ref.shape[0] // sub):
        rs = slice(s * sub, (s + 1) * sub)
        h = _layer_norm(x_ref[rs, :]) * (1.0 + mod_ref[1:2, :]) + mod_ref[0:1, :]
        h = h.astype(BF16)
        proj_ref[rs, :] = _dot(h, w_ref[...])
        v_t = _dot_nt(wvt_ref[...], h).astype(BF16)
        gates_t = _dot_nt(wgt_ref[...], h) + bg_ref[...]
        for k in range(sub // LANES):
            vt_ref[s * (sub // LANES) + k] = v_t[:, k * LANES:(k + 1) * LANES]
            gates_ref[s * (sub // LANES) + k] = gates_t[:, k * LANES:(k + 1) * LANES]


def _inproj(x, mod, cond_row, layer, w_main, w_v_t, w_gate_t, b_gate):
    tokens, d = x.shape
    n = w_main.shape[2]
    nv = w_v_t.shape[1]
    ng = w_gate_t.shape[1]
    tm = TOKEN_TILE
    return pl.pallas_call(
        _inproj_kernel,
        out_shape=(jax.ShapeDtypeStruct((tokens, n), F32), jax.ShapeDtypeStruct((tokens // LANES, nv, LANES), BF16),
                   jax.ShapeDtypeStruct((tokens // LANES, ng, LANES), F32)),
        grid=(tokens // tm,),
        in_specs=[
            pl.BlockSpec((tm, d), lambda i: (i, 0)),
            pl.BlockSpec((None, None, 6, d), lambda i: (layer, cond_row(i * tm), 0, 0)),
            pl.BlockSpec((None, d, n), lambda i: (layer, 0, 0)),
            pl.BlockSpec((None, nv, d), lambda i: (layer, 0, 0)),
            pl.BlockSpec((None, ng, d), lambda i: (layer, 0, 0)),
            pl.BlockSpec((None, ng, 1), lambda i: (layer, 0, 0)),
        ],
        out_specs=(pl.BlockSpec((tm, n), lambda i: (i, 0)),
                   pl.BlockSpec((tm // LANES, nv, LANES), lambda i: (i, 0, 0)),
                   pl.BlockSpec((tm // LANES, ng, LANES), lambda i: (i, 0, 0))),
        compiler_params=_params("parallel"),
        name="inproj",
    )(x, mod, w_main, w_v_t, w_gate_t, b_gate)


def _pool_col_matrices(cols):
    t = np.arange(POOL_TILE)
    row, col = t // cols, t % cols
    mats = []
    for win in POOL_WINDOWS:
        hw = win // 2
        lo = np.maximum(col - hw, 0)
        hi = np.minimum(col + hw, cols)
        same_row = row[:, None] == row[None, :]
        inside = (col[None, :] >= lo[:, None]) & (col[None, :] < hi[:, None])
        mats.append((same_row & inside).astype(np.float32))
    return jnp.asarray(np.stack(mats), dtype=BF16)


def _pool_kernel(xp_ref, pmat_ref, wp_ref, ps_ref, out_ref, pad_ref, hl_s, y_s, *, rows, cols, tokens):
    gdim = LANES
    tile = POOL_TILE
    halo = (max(POOL_WINDOWS) // 2) * cols if rows > 1 else 0
    t = lax.broadcasted_iota(jnp.int32, (tile, 1), 0)
    col = t & (cols - 1)
    if rows > 1:
        zeros = jnp.zeros((halo, gdim), F32)
        pad_ref[0:halo, :] = zeros
        pad_ref[halo + tokens:halo + tokens + halo, :] = zeros
    for gi, win in enumerate(POOL_WINDOWS):
        hw = win // 2
        gs = slice(gi * gdim, (gi + 1) * gdim)
        if rows > 1:
            pad_ref[halo:halo + tokens, :] = xp_ref[:, gs]
        cnt_c = (jnp.minimum(col + hw, cols) - jnp.maximum(col - hw, 0)).astype(F32)
        pm = pmat_ref[gi]
        wp = wp_ref[gi]
        scale = ps_ref[:, gs]

        nt = tokens // tile
        for tt in range(nt):
            off = tt * tile
            if rows > 1:
                acc = None
                for dr in range(-hw, hw):
                    term = pad_ref[off + halo + dr * cols:off + halo + dr * cols + tile, :]
                    acc = term if acc is None else acc + term
                rw = (off + t) >> int(math.log2(cols))
                cnt_r = (jnp.minimum(rw + hw, rows) - jnp.maximum(rw - hw, 0)).astype(F32)
                m1 = acc / cnt_r
            else:
                m1 = xp_ref[off:off + tile, gs]
            hi = m1.astype(BF16)
            hl_s[gi * nt + tt] = jnp.concatenate([hi, (m1 - hi.astype(F32)).astype(BF16)], axis=1)
        for tt in range(nt):
            off = tt * tile
            both = _dot(pm, hl_s[gi * nt + tt])
            m2 = (both[:, :gdim] + both[:, gdim:]) / cnt_c
            y_s[gi, off:off + tile, :] = (m2 - xp_ref[off:off + tile, gs]).astype(BF16)
        out_ref[:, gs] = (_dot(y_s[gi], wp) * scale).astype(BF16)


def _pool(proj, pmat, layer, w_pool, pool_scale, rows, cols, step_tokens):
    total = proj.shape[0]
    pool_w = N_POOL_GROUPS * LANES
    halo = (max(POOL_WINDOWS) // 2) * cols if rows > 1 else 0
    kern = functools.partial(_pool_kernel, rows=rows, cols=cols, tokens=step_tokens)
    return pl.pallas_call(
        kern,
        out_shape=jax.ShapeDtypeStruct((total, pool_w), BF16),
        grid=(total // step_tokens,),
        in_specs=[
            pl.BlockSpec((step_tokens, pool_w), lambda i: (i, 0)),
            pl.BlockSpec(pmat.shape, lambda i: (0, 0, 0)),
            pl.BlockSpec((None,) + w_pool.shape[1:], lambda i: (layer, 0, 0, 0)),
            pl.BlockSpec((None, 1, pool_w), lambda i: (layer, 0, 0)),
        ],
        out_specs=pl.BlockSpec((step_tokens, pool_w), lambda i: (i, 0)),
        scratch_shapes=[
            pltpu.VMEM((step_tokens + 2 * halo if rows > 1 else SUBLANES, LANES), F32),
            pltpu.VMEM((N_POOL_GROUPS * (step_tokens // POOL_TILE), POOL_TILE, 2 * LANES), BF16),
            pltpu.VMEM((N_POOL_GROUPS, step_tokens, LANES), BF16),
        ],
        compiler_params=_params("parallel"),
        name="pool",
    )(proj, pmat, w_pool, pool_scale)


def _log_sigmoid(x):
    return jnp.minimum(x, 0.0) - jnp.log(1.0 + jnp.exp(-jnp.abs(x)))


def _split3(x):
    hi = x.astype(BF16)
    r1 = x - hi.astype(F32)
    mid = r1.astype(BF16)
    lo = (r1 - mid.astype(F32)).astype(BF16)
    return lo, mid, hi


def _mlstm_kernel(*refs, seq, chunk, hp, zero_init, emit_state, n_unused, state_slot):
    refs = list(refs)
    q_ref, k_ref, vt_ref, o_ref, g_ref, wq_ref, wk_ref, hng_ref = refs[:8]
    pos = 8
    if not zero_init:
        c0_ref, n0_ref, m0_ref = refs[pos:pos + 3]
        pos += 3
    pos += n_unused
    out_ref = refs[pos]
    pos += 1
    if emit_state:
        c_ref, n_ref, m_ref = refs[pos:pos + 3]
        pos += 3
    qpad, kpad, qs, ks, ht_f, ht_b, rcb_s, br_s, rc_s, ct_s, dt_s, mrow_s, z1_s, rhs_s, z2_s = refs[pos:]
    dh = q_ref.shape[1] // hp
    nc = seq // chunk
    npair = 2 * hp
    aug = dh + BF16_ROWS
    margin = SUBLANES
    unroll = nc <= MLSTM_UNROLL_CHUNKS

    for pad_ref, src_ref in ((qpad, q_ref), (kpad, k_ref)):
        pad_ref[0:margin, :] = jnp.zeros((margin, hp * dh), F32)
        pad_ref[margin + seq:2 * margin + seq, :] = jnp.zeros((margin, hp * dh), F32)
        pad_ref[margin:margin + seq, :] = src_ref[...]

    def conv_silu(pad_ref, w_ref, c, hs):
        w = w_ref[:, hs]
        start = c * chunk if unroll else pl.multiple_of(c * chunk, chunk)
        win = pad_ref[pl.ds(start, chunk + 2 * margin), hs]
        y = (pltpu.roll(win, 1, 0) * w[0:1, :] + win * w[1:2, :]
             + pltpu.roll(win, chunk + 2 * margin - 1, 0) * w[2:3, :])[margin:margin + chunk]
        return y * jax.nn.sigmoid(y)

    ri = lax.broadcasted_iota(jnp.int32, (chunk, chunk), 0)
    ci = lax.broadcasted_iota(jnp.int32, (chunk, chunk), 1)
    lower = ci <= ri
    upper = ci >= ri
    upper_b = jnp.where(upper, 1.0, 0.0).astype(BF16)
    row_is_fwd = lax.broadcasted_iota(jnp.int32, (npair, chunk), 0) < hp
    ones_row = jnp.where(lax.broadcasted_iota(jnp.int32, (BF16_ROWS, chunk), 0) == 0, 1.0, 0.0).astype(BF16)

    gates = g_ref[...]
    lf = _log_sigmoid(gates[:, npair:, :])
    lf3 = _split3(lf.reshape(nc * npair, chunk))
    prefix = (_dot(lf3[0], upper_b) + _dot(lf3[1], upper_b) + _dot(lf3[2], upper_b)).reshape(nc, npair, chunk)
    suffix = prefix[:, :, chunk - 1:chunk] - prefix + lf
    br = jnp.where(row_is_fwd, prefix, suffix)
    br_s[...] = br
    rc_s[...] = gates[:, :npair, :] - br

    def rows_of(c):
        return slice(c * chunk, (c + 1) * chunk) if unroll else pl.ds(pl.multiple_of(c * chunk, chunk), chunk)

    def prep(c, carry):
        sl = rows_of(c)
        rc = rc_s[c]
        for p in range(npair):
            rcb_s[sl, p * LANES:(p + 1) * LANES] = jnp.transpose(jnp.broadcast_to(rc[p:p + 1, :], (LANES, chunk)))
        for hh in range(hp):
            hs = slice(hh * dh, (hh + 1) * dh)
            qs[sl, hs] = conv_silu(qpad, wq_ref, c, hs).astype(BF16)
            ks[sl, hs] = (conv_silu(kpad, wk_ref, c, hs) * (dh ** -0.5)).astype(BF16)
        return carry

    n_rows = lax.broadcasted_iota(jnp.int32, (BF16_ROWS, dh), 0) == 0
    ms = []
    for p in range(npair):
        di, hh = divmod(p, hp)
        if zero_init:
            ct_s[p] = jnp.zeros((aug, dh), F32)
            ms.append(jnp.zeros((1, 1), F32))
        else:
            ct_s[p, 0:dh, :] = jnp.transpose(c0_ref[di, hh])
            ct_s[p, dh:aug, :] = jnp.where(n_rows, n0_ref[di, hh], 0.0)
            ms.append(m0_ref[di, hh])

    def pair_args(p, i):
        di, hh = divmod(p, hp)
        fwd = di == 0
        c = i if fwd else nc - 1 - i
        last = chunk - 1 if fwd else 0
        sl = rows_of(c)
        return fwd, hh, c, last, sl, slice(hh * dh, (hh + 1) * dh)

    def body(i, ms):
        for p in range(npair):
            fwd, hh, c, last, sl, hs = pair_args(p, i)
            rcb = rcb_s[sl, p * LANES:(p + 1) * LANES]
            xt = jnp.where(upper if fwd else lower, rcb, -jnp.inf)
            mrow = jnp.maximum(jnp.max(xt, axis=0, keepdims=True), ms[p])
            dt_s[p] = jnp.exp(xt - mrow)
            mrow_s[p] = jnp.broadcast_to(mrow, (SUBLANES, chunk))
            w = jnp.exp(rcb - mrow[:, last:last + 1])
            rhs_s[p, :, chunk:] = (w * ks[sl, hs].astype(F32)).astype(BF16)
        zeros_q = jnp.zeros((chunk, dh), BF16)
        for pk in range(npair // 2):
            _, _, _, _, sl_a, hs_a = pair_args(2 * pk, i)
            _, _, _, _, sl_b, hs_b = pair_args(2 * pk + 1, i)
            lhs = jnp.concatenate([
                jnp.concatenate([ks[sl_a, hs_a], ks[sl_b, hs_b]], axis=1),
                jnp.concatenate([ct_s[2 * pk].astype(BF16), ct_s[2 * pk + 1].astype(BF16)], axis=1)], axis=0)
            q_diag = jnp.concatenate([
                jnp.concatenate([qs[sl_a, hs_a], zeros_q], axis=1),
                jnp.concatenate([zeros_q, qs[sl_b, hs_b]], axis=1)], axis=0)
            z1_s[pk] = _dot_nt(lhs, q_diag)
        for p in range(npair):
            lanes = slice((p % 2) * chunk, (p % 2 + 1) * chunk)
            rhs_s[p, :, 0:chunk] = (z1_s[p // 2, 0:chunk, lanes] * dt_s[p]).astype(BF16)
        for p in range(npair):
            fwd, hh, c, last, sl, hs = pair_args(p, i)
            z2_s[p] = _dot(jnp.concatenate([vt_ref[c, hs, :], ones_row], axis=0), rhs_s[p])
        new_ms = []
        for p in range(npair):
            fwd, hh, c, last, sl, hs = pair_args(p, i)
            m = ms[p]
            mrow = mrow_s[p, 0:1, :]
            lanes = slice((p % 2) * chunk, (p % 2 + 1) * chunk)
            num = z1_s[p // 2, chunk:, lanes] * jnp.exp(m - mrow) + z2_s[p, :, 0:chunk]
            b_r = br_s[c, p:p + 1, :]
            scale = 1.0 / jnp.maximum(jnp.abs(num[dh:dh + 1, :]), jnp.exp(-(b_r + mrow)))
            h_t = num[0:dh, :] * scale
            if fwd:
                ht_f[hh * nc + c] = h_t
            else:
                ht_b[hh * nc + c] = h_t
            mlast = mrow[:, last:last + 1]
            ct_s[p] = jnp.exp(m - mlast) * ct_s[p] + z2_s[p, :, chunk:]
            new_ms.append(b_r[:, last:last + 1] + mlast)
        return tuple(new_ms)

    def finish(c, carry):
        sl = rows_of(c)
        for hh in range(hp):
            hs = slice(hh * dh, (hh + 1) * dh)
            h_t = ht_f[hh * nc + c] + ht_b[hh * nc + c]
            mu = jnp.mean(h_t, axis=0, keepdims=True)
            hc = h_t - mu
            var = jnp.mean(hc * hc, axis=0, keepdims=True)
            hn = jnp.transpose(hc * lax.rsqrt(var + LN_EPS)) * hng_ref[:, hs]
            out_ref[sl, hs] = (jax.nn.sigmoid(o_ref[sl, hs]) * hn).astype(BF16)
        return carry

    if unroll:
        for c in range(nc):
            prep(c, 0)
        ms = tuple(ms)
        for i in range(nc):
            ms = body(i, ms)
        for c in range(nc):
            finish(c, 0)
    else:
        half = nc // 2
        prep(0, 0)
        prep(nc - 1, 0)

        def first_half(i, ms):
            ms = body(i, ms)
            prep(i + 1, 0)
            prep(nc - 2 - i, 0)
            return ms

        def second_half(i, ms):
            ms = body(i, ms)
            finish(i, 0)
            finish(nc - 1 - i, 0)
            return ms

        ms = lax.fori_loop(0, half, first_half, tuple(ms), unroll=True)
        ms = lax.fori_loop(half, nc, second_half, ms, unroll=True)

    if emit_state:
        for slot in range(c_ref.shape[0]):
            if slot != state_slot:
                c_ref[slot] = jnp.zeros(c_ref.shape[1:], F32)
                n_ref[slot] = jnp.zeros(n_ref.shape[1:], F32)
                m_ref[slot] = jnp.zeros(m_ref.shape[1:], F32)
        for p in range(npair):
            di, hh = divmod(p, hp)
            c_ref[state_slot, di, hh] = jnp.transpose(ct_s[p, 0:dh, :])
            n_ref[state_slot, di, hh] = ct_s[p, dh:dh + 1, :]
            m_ref[state_slot, di, hh] = ms[p]


def _mlstm_heads_per_step(seq, dh):
    return max(1, min(MLSTM_HEADS, MLSTM_STEP_BYTES // (seq * dh * 4)))


def _gate_row_order(hp):
    heads = MLSTM_HEADS
    order = []
    for g in range(heads // hp):
        for kind in range(2):
            for di in range(2):
                for hh in range(hp):
                    order.append((di * 2 + kind) * heads + g * hp + hh)
    return np.asarray(order)


def _mlstm(proj, v_t, gates_t, wl, w_qk_conv, hn_g, batch, seq, col0, dh, init_state=None, layer=None, depth=None,
           prev=None):
    heads = MLSTM_HEADS
    chunk = MLSTM_CHUNK
    assert chunk == LANES and seq % chunk == 0 and (seq // chunk <= MLSTM_UNROLL_CHUNKS or seq // chunk % 2 == 0)
    nc = seq // chunk
    hp = _mlstm_heads_per_step(seq, dh)
    hg = heads // hp
    npair = 2 * hp
    aug = dh + BF16_ROWS
    width = hp * dh
    cb = col0 // width
    gb = heads * dh // width
    zero_init = init_state is None
    emit_state = layer is not None
    prev = () if prev is None else tuple(prev)
    slots, first_slot = (1, layer) if prev else (depth, 0)
    kern = functools.partial(_mlstm_kernel, seq=seq, chunk=chunk, hp=hp, zero_init=zero_init, emit_state=emit_state,
                             n_unused=len(prev), state_slot=(layer - first_slot) if emit_state else None)

    def tok_spec(group):
        return pl.BlockSpec((seq, width), lambda b, g: (b, cb + group * gb + g))

    in_specs = [
        tok_spec(0), tok_spec(1), pl.BlockSpec((nc, width, chunk), lambda b, g: (b, g, 0)), tok_spec(2),
        pl.BlockSpec((nc, 2 * npair, chunk), lambda b, g: (b, g, 0)),
        pl.BlockSpec((None, 3, width), lambda b, g: (wl, 0, g)),
        pl.BlockSpec((None, 3, width), lambda b, g: (wl, 0, gb + g)),
        pl.BlockSpec((None, 1, width), lambda b, g: (wl, 0, g)),
    ]
    args = [proj, proj, v_t, proj, gates_t, w_qk_conv, w_qk_conv, hn_g]
    if not zero_init:
        in_specs += [pl.BlockSpec((None, None, 2, hp, dh, dh), lambda b, g: (b, wl, 0, g, 0, 0)),
                     pl.BlockSpec((None, None, 2, hp, 1, dh), lambda b, g: (b, wl, 0, g, 0, 0)),
                     pl.BlockSpec((None, None, 2, hp, 1, 1), lambda b, g: (b, wl, 0, g, 0, 0))]
        args += list(init_state)
    aliases = {}
    for k, arr in enumerate(prev):
        aliases[len(args)] = 1 + k
        in_specs.append(pl.BlockSpec(memory_space=pl.ANY))
        args.append(arr)
    out_shape = [jax.ShapeDtypeStruct((batch * seq, heads * dh), BF16)]
    out_specs = [pl.BlockSpec((seq, width), lambda b, g: (b, g))]
    if emit_state:
        out_shape += [jax.ShapeDtypeStruct((batch, depth, 2, heads, dh, dh), F32),
                      jax.ShapeDtypeStruct((batch, depth, 2, heads, 1, dh), F32),
                      jax.ShapeDtypeStruct((batch, depth, 2, heads, 1, 1), F32)]
        out_specs += [pl.BlockSpec((None, slots, 2, hp, dh, dh), lambda b, g: (b, first_slot, 0, g, 0, 0)),
                      pl.BlockSpec((None, slots, 2, hp, 1, dh), lambda b, g: (b, first_slot, 0, g, 0, 0)),
                      pl.BlockSpec((None, slots, 2, hp, 1, 1), lambda b, g: (b, first_slot, 0, g, 0, 0))]
    res = pl.pallas_call(
        kern,
        out_shape=tuple(out_shape),
        grid=(batch, hg),
        in_specs=in_specs,
        out_specs=tuple(out_specs),
        scratch_shapes=[
            pltpu.VMEM((seq + 2 * SUBLANES, width), F32), pltpu.VMEM((seq + 2 * SUBLANES, width), F32),
            pltpu.VMEM((seq, width), BF16), pltpu.VMEM((seq, width), BF16),
            pltpu.VMEM((hp * nc, dh, chunk), F32), pltpu.VMEM((hp * nc, dh, chunk), F32),
            pltpu.VMEM((seq, npair * LANES), F32), pltpu.VMEM((nc, npair, chunk), F32),
            pltpu.VMEM((nc, npair, chunk), F32),
            pltpu.VMEM((npair, aug, dh), F32),
            pltpu.VMEM((npair, chunk, chunk), F32), pltpu.VMEM((npair, SUBLANES, chunk), F32),
            pltpu.VMEM((npair // 2, chunk + aug, 2 * chunk), F32), pltpu.VMEM((npair, chunk, chunk + dh), BF16),
            pltpu.VMEM((npair, aug, chunk + dh), F32),
        ],
        input_output_aliases=aliases,
        compiler_params=_params("parallel", "parallel"),
        name="mlstm",
    )(*args)
    return res[0], tuple(res[1:])


def _outproj_kernel(pool_ref, ml_ref, x_ref, mod_ref, wp_ref, wm_ref, g_ref, b_ref, x1_ref, h2_ref, *, alpha):
    sub = TOKEN_SUB
    for s in range(x_ref.shape[0] // sub):
        rs = slice(s * sub, (s + 1) * sub)
        mix = _dot(pool_ref[rs, :], wp_ref[...]) + _dot(ml_ref[rs, :], wm_ref[...])
        x1 = _layer_norm(alpha * x_ref[rs, :] + mod_ref[2:3, :] * mix) * g_ref[...] + b_ref[...]
        x1_ref[rs, :] = x1
        h2 = _layer_norm(x1) * (1.0 + mod_ref[4:5, :]) + mod_ref[3:4, :]
        h2_ref[rs, :] = h2.astype(BF16)


def _outproj(pool_out, ml_out, x, mod, cond_row, layer, w_out, ln_g, ln_b, alpha):
    tokens, d = x.shape
    half = pool_out.shape[1]
    tm = TOKEN_TILE
    kern = functools.partial(_outproj_kernel, alpha=alpha)
    return pl.pallas_call(
        kern,
        out_shape=(jax.ShapeDtypeStruct((tokens, d), F32), jax.ShapeDtypeStruct((tokens, d), BF16)),
        grid=(tokens // tm,),
        in_specs=[
            pl.BlockSpec((tm, half), lambda i: (i, 0)),
            pl.BlockSpec((tm, half), lambda i: (i, 0)),
            pl.BlockSpec((tm, d), lambda i: (i, 0)),
            pl.BlockSpec((None, None, 6, d), lambda i: (layer, cond_row(i * tm), 0, 0)),
            pl.BlockSpec((None, half, d), lambda i: (layer, 0, 0)),
            pl.BlockSpec((None, half, d), lambda i: (layer, 1, 0)),
            pl.BlockSpec((None, 1, d), lambda i: (layer, 0, 0)),
            pl.BlockSpec((None, 1, d), lambda i: (layer, 0, 0)),
        ],
        out_specs=(pl.BlockSpec((tm, d), lambda i: (i, 0)), pl.BlockSpec((tm, d), lambda i: (i, 0))),
        compiler_params=_params("parallel"),
        name="outproj",
    )(pool_out, ml_out, x, mod, w_out, w_out, ln_g, ln_b)


def _gelu_tanh(x):
    return 0.5 * x * (1.0 + jnp.tanh(math.sqrt(2.0 / math.pi) * (x + 0.044715 * (x * x * x))))


def _ffn_kernel(h_ref, x1_ref, mod_ref, wa_ref, wg_ref, wdw_ref, bdw_ref, wd_ref, lng_ref, lnb_ref, out_ref,
                gpad, a_s, u_s, wa_b, wg_b, wd_b, *, rows, cols, tokens, alpha):
    j = pl.program_id(1)
    tf = wa_ref.shape[1]
    blk = FFN_BLOCK
    nb = tokens // blk
    base = (cols if rows > 1 else 0) + SUBLANES

    @pl.when(j == 0)
    def _():
        out_ref[...] = jnp.zeros_like(out_ref)

    wa_b[...] = wa_ref[...].astype(BF16)
    wg_b[...] = wg_ref[...].astype(BF16)
    wd_b[...] = wd_ref[...].astype(BF16)
    gpad[0:base, :] = jnp.zeros((base, tf), F32)
    gpad[base + tokens:base + tokens + base, :] = jnp.zeros((base, tf), F32)

    wdw = wdw_ref[...]
    bdw = bdw_ref[...]
    ci = lax.broadcasted_iota(jnp.int32, (cols, 1), 0)
    first = ci == 0
    last = ci == cols - 1
    taps_y = (-1, 0, 1) if rows > 1 else (0,)

    def up(b):
        hb = h_ref[b * blk:(b + 1) * blk, :]
        a_s[b * blk:(b + 1) * blk, :] = _dot(hb, wa_b[...])
        gpad[base + b * blk:base + (b + 1) * blk, :] = _dot(hb, wg_b[...])

    def conv_gate(b):
        for r in range(b * blk // cols, (b + 1) * blk // cols):
            off = r * cols
            for lb in range(tf // LANES):
                ls = slice(lb * LANES, (lb + 1) * LANES)
                left = mid = right = None
                for dy in taps_y:
                    r0 = base + off + dy * cols
                    kk = (dy + 1) * 3
                    tl = gpad[r0 - 1:r0 - 1 + cols, ls] * wdw[kk:kk + 1, ls]
                    tc = gpad[r0:r0 + cols, ls] * wdw[kk + 1:kk + 2, ls]
                    tr = gpad[r0 + 1:r0 + 1 + cols, ls] * wdw[kk + 2:kk + 3, ls]
                    left = tl if left is None else left + tl
                    mid = tc if mid is None else mid + tc
                    right = tr if right is None else right + tr
                y = mid + jnp.where(first, 0.0, left) + jnp.where(last, 0.0, right) + bdw[:, ls]
                u_s[off:off + cols, ls] = (_gelu_tanh(y) * a_s[off:off + cols, ls]).astype(BF16)

    def down(b):
        out_ref[b * blk:(b + 1) * blk, :] += _dot(u_s[b * blk:(b + 1) * blk, :], wd_b[...])

    for step in range(nb + 2):
        if step < nb:
            up(step)
        if 1 <= step <= nb:
            conv_gate(step - 1)
        if step >= 2:
            down(step - 2)

    @pl.when(j == pl.num_programs(1) - 1)
    def _():
        gate = mod_ref[5:6, :]
        lng = lng_ref[...]
        lnb = lnb_ref[...]

        def ln_body(r, carry):
            sl = pl.ds(pl.multiple_of(r * LN_ROWS, LN_ROWS), LN_ROWS)
            z = alpha * x1_ref[sl, :] + gate * out_ref[sl, :]
            out_ref[sl, :] = _layer_norm(z) * lng + lnb
            return carry

        lax.fori_loop(0, tokens // LN_ROWS, ln_body, 0)


def _ffn(h2, x1, mod, cond_row, layer, w_up, w_dw, b_dw, w_down, ln_g, ln_b, rows, cols, alpha):
    total, d = x1.shape
    d_ff = w_down.shape[1]
    tm = FFN_TOKENS
    tf = FFN_COLS
    nj = d_ff // tf
    kern = functools.partial(_ffn_kernel, rows=rows, cols=cols, tokens=tm, alpha=alpha)
    return pl.pallas_call(
        kern,
        out_shape=jax.ShapeDtypeStruct((total, d), F32),
        grid=(total // tm, nj),
        in_specs=[
            pl.BlockSpec((tm, d), lambda i, j: (i, 0)),
            pl.BlockSpec((tm, d), lambda i, j: (i, 0)),
            pl.BlockSpec((None, None, 6, d), lambda i, j: (layer, cond_row(i * tm), 0, 0)),
            pl.BlockSpec((None, d, tf), lambda i, j: (layer, 0, j)),
            pl.BlockSpec((None, d, tf), lambda i, j: (layer, 0, nj + j)),
            pl.BlockSpec((None, 9, tf), lambda i, j: (layer, 0, j)),
            pl.BlockSpec((None, 1, tf), lambda i, j: (layer, 0, j)),
            pl.BlockSpec((None, tf, d), lambda i, j: (layer, j, 0)),
            pl.BlockSpec((None, 1, d), lambda i, j: (layer, 0, 0)),
            pl.BlockSpec((None, 1, d), lambda i, j: (layer, 0, 0)),
        ],
        out_specs=pl.BlockSpec((tm, d), lambda i, j: (i, 0)),
        scratch_shapes=[
            pltpu.VMEM((tm + 2 * ((cols if rows > 1 else 0) + SUBLANES), tf), F32),
            pltpu.VMEM((tm, tf), F32),
            pltpu.VMEM((tm, tf), BF16),
            pltpu.VMEM((d, tf), BF16), pltpu.VMEM((d, tf), BF16), pltpu.VMEM((tf, d), BF16),
        ],
        compiler_params=_params("parallel", "arbitrary"),
        name="ffn",
    )(h2, x1, mod, w_up, w_up, w_dw, b_dw, w_down, ln_g, ln_b)


def _trunk_layer(x, batch, rows, cols, cond_row, layer, p, pmat, alpha, dh, state_kw):
    seq = rows * cols
    hp = _mlstm_heads_per_step(seq, dh)
    mod = p["ada"]
    proj, v_t, gates_t = _inproj(x, mod, cond_row, layer, p["w_in_main"], p["w_v_t"], p["w_gate_t"][hp],
                                 p["b_gate"][hp])
    pool_out = _pool(proj, pmat, layer, p["w_pool"], p["pool_scale"], rows, cols, min(FFN_TOKENS, batch * seq))
    ml_out, states = _mlstm(proj, v_t, gates_t, layer, p["w_qk_conv"], p["hn_g"], batch, seq, N_POOL_GROUPS * LANES,
                            dh, **state_kw)
    x1, h2 = _outproj(pool_out, ml_out, x, mod, cond_row, layer, p["w_out"], p["ln1_g"], p["ln1_b"], alpha)
    x2 = _ffn(h2, x1, mod, cond_row, layer, p["w_up"], p["w_dw"], p["b_dw"], p["w_down"], p["ln2_g"], p["ln2_b"],
              rows, cols, alpha)
    return x2, states


def kernel(x_prompt, x_sample, c, state_C, state_n, state_m, c_ctx, w_ada, b_ada, w_in, b_gates, w_qk_conv, hn_g,
           w_pool, pool_scale, w_out, ln1_g, ln1_b, w_up, w_dw, b_dw, w_down, ln2_g, ln2_b):
    b_p, l_p, d = x_prompt.shape
    b_s, l_s, _ = x_sample.shape
    depth = w_ada.shape[0]
    heads = MLSTM_HEADS
    dh = state_C.shape[-1]
    pool_w = N_POOL_GROUPS * LANES
    mw = heads * dh
    n_main = pool_w + 4 * mw
    v0 = pool_w + 2 * mw
    d_ff = w_down.shape[1]
    alpha = (2.0 * depth) ** 0.25
    rows_s = l_s // GRID_W
    assert b_s + 1 <= COND_ROWS and l_p % POOL_TILE == 0 and POOL_TILE % GRID_W == 0

    cond = jnp.concatenate([c_ctx[None, :], c, jnp.zeros((COND_ROWS - 1 - b_s, d), F32)], axis=0)

    pmat_p = _pool_col_matrices(l_p)
    pmat_s = _pool_col_matrices(GRID_W)
    assert min(FFN_TOKENS, b_s * l_s) == l_s
    hps = sorted({_mlstm_heads_per_step(l_p, dh), _mlstm_heads_per_step(l_s, dh)})

    w_gate = w_in[:, :, n_main:]
    p = dict(
        ada=_ada(cond, w_ada, b_ada).reshape(depth, COND_ROWS, 6, d),
        w_in_main=jnp.concatenate([w_in[:, :, :v0], w_in[:, :, v0 + mw:n_main]], axis=2).astype(BF16),
        w_v_t=w_in[:, :, v0:v0 + mw].transpose(0, 2, 1).astype(BF16),
        w_gate_t={hp: w_gate[:, :, _gate_row_order(hp)].transpose(0, 2, 1).astype(BF16) for hp in hps},
        b_gate={hp: b_gates[:, _gate_row_order(hp)][:, :, None] for hp in hps},
        w_qk_conv=w_qk_conv, hn_g=hn_g[:, None, :],
        w_pool=w_pool.astype(BF16), pool_scale=pool_scale[:, None, :],
        w_out=w_out.astype(BF16), ln1_g=ln1_g[:, None, :], ln1_b=ln1_b[:, None, :],
        w_up=w_up, w_dw=w_dw.reshape(depth, 9, d_ff), b_dw=b_dw[:, None, :],
        w_down=w_down, ln2_g=ln2_g[:, None, :], ln2_b=ln2_b[:, None, :])
    init = (state_C, state_n.reshape(b_s, depth, 2, heads, 1, dh), state_m.reshape(b_s, depth, 2, heads, 1, 1))

    y_p = x_prompt.reshape(b_p * l_p, d)
    y_s = x_sample.reshape(b_s * l_s, d)
    states = None
    for l in range(depth):
        y_p, states = _trunk_layer(y_p, b_p, 1, l_p, lambda t: 0, l, p, pmat_p, alpha, dh,
                                   dict(layer=l, depth=depth, prev=states))
        y_s, _ = _trunk_layer(y_s, b_s, rows_s, GRID_W, lambda t: 1 + t // l_s, l, p, pmat_s, alpha, dh,
                              dict(init_state=init))
    new_c, new_n, new_m = states
    return (y_p.reshape(b_p, l_p, d), y_s.reshape(b_s, l_s, d), new_c, new_n.reshape(b_p, depth, 2, heads, dh),
            new_m.reshape(b_p, depth, 2, heads))
```

```python
import functools
import math

import numpy as np
import jax
import jax.numpy as jnp
from jax import lax
from jax.experimental import pallas as pl
from jax.experimental.pallas import tpu as pltpu

F32 = jnp.float32
BF16 = jnp.bfloat16

GRID_W = 64
POOL_WINDOWS = (2, 4, 8, 16)
N_POOL_GROUPS = len(POOL_WINDOWS)
MLSTM_HEADS = 4
LN_EPS = 1e-5

LANES = 128
SUBLANES = 8
BF16_ROWS = 16
VMEM_LIMIT_BYTES = 56 * 1024 * 1024

COND_ROWS = SUBLANES
MLSTM_CHUNK = LANES
MLSTM_UNROLL_CHUNKS = 2
MLSTM_SHORT_SEQS_PER_STEP = 2
MLSTM_STEP_BYTES = 2 * 1024 * 1024
POOL_TILE = 256
TOKEN_TILE = 1024
TOKEN_SUB = 256
FFN_TOKENS = 2048
FFN_BLOCK = 512
FFN_COLS = 256
FFN_CONV_ROWS = 256
LN_ROWS = 256


def _params(*sem):
    return pltpu.CompilerParams(dimension_semantics=sem, vmem_limit_bytes=VMEM_LIMIT_BYTES)


def _layer_norm(x):
    mu = jnp.mean(x, axis=-1, keepdims=True)
    xc = x - mu
    var = jnp.mean(xc * xc, axis=-1, keepdims=True)
    return xc * lax.rsqrt(var + LN_EPS)


def _dot(a, b):
    return jnp.dot(a, b, preferred_element_type=F32)


def _dot_nt(a, b):
    return lax.dot_general(a, b, (((1,), (1,)), ((), ())), preferred_element_type=F32)


def _ada_kernel(cond_ref, w_ref, b_ref, out_ref):
    cnd = cond_ref[...]
    act = (cnd * jax.nn.sigmoid(cnd)).astype(BF16)
    out_ref[...] = _dot(act, w_ref[...].astype(BF16)) + b_ref[...]


def _ada(cond, w_ada, b_ada):
    depth, d, n = w_ada.shape
    tn = 1536
    return pl.pallas_call(
        _ada_kernel,
        out_shape=jax.ShapeDtypeStruct((depth, COND_ROWS, n), F32),
        grid=(depth, n // tn),
        in_specs=[
            pl.BlockSpec((COND_ROWS, d), lambda l, j: (0, 0)),
            pl.BlockSpec((None, d, tn), lambda l, j: (l, 0, j)),
            pl.BlockSpec((None, 1, tn), lambda l, j: (l, 0, j)),
        ],
        out_specs=pl.BlockSpec((None, COND_ROWS, tn), lambda l, j: (l, 0, j)),
        compiler_params=_params("parallel", "parallel"),
        name="ada",
    )(cond, w_ada, b_ada.reshape(depth, 1, n))


def _inproj_kernel(x_ref, mod_ref, w_ref, wvt_ref, wgt_ref, bg_ref, proj_ref, vt_ref, gates_ref):
    sub = TOKEN_SUB
    for s in range(x_ref.shape[0] // sub):
        rs = slice(s * sub, (s + 1) * sub)
        h = _layer_norm(x_ref[rs, :]) * (1.0 + mod_ref[1:2, :]) + mod_ref[0:1, :]
        h = h.astype(BF16)
        proj_ref[rs, :] = _dot(h, w_ref[...])
        v_t = _dot_nt(wvt_ref[...], h).astype(BF16)
        gates_t = _dot_nt(wgt_ref[...], h) + bg_ref[...]
        for k in range(sub // LANES):
            vt_ref[s * (sub // LANES) + k] = v_t[:, k * LANES:(k + 1) * LANES]
            gates_ref[s * (sub // LANES) + k] = gates_t[:, k * LANES:(k + 1) * LANES]


def _inproj(x, mod, cond_row, layer, w_main, w_v_t, w_gate_t, b_gate):
    tokens, d = x.shape
    n = w_main.shape[2]
    nv = w_v_t.shape[1]
    ng = w_gate_t.shape[1]
    tm = TOKEN_TILE
    return pl.pallas_call(
        _inproj_kernel,
        out_shape=(jax.ShapeDtypeStruct((tokens, n), F32), jax.ShapeDtypeStruct((tokens // LANES, nv, LANES), BF16),
                   jax.ShapeDtypeStruct((tokens // LANES, ng, LANES), F32)),
        grid=(tokens // tm,),
        in_specs=[
            pl.BlockSpec((tm, d), lambda i: (i, 0)),
            pl.BlockSpec((None, None, 6, d), lambda i: (layer, cond_row(i * tm), 0, 0)),
            pl.BlockSpec((None, d, n), lambda i: (layer, 0, 0)),
            pl.BlockSpec((None, nv, d), lambda i: (layer, 0, 0)),
            pl.BlockSpec((None, ng, d), lambda i: (layer, 0, 0)),
            pl.BlockSpec((None, ng, 1), lambda i: (layer, 0, 0)),
        ],
        out_specs=(pl.BlockSpec((tm, n), lambda i: (i, 0)),
                   pl.BlockSpec((tm // LANES, nv, LANES), lambda i: (i, 0, 0)),
                   pl.BlockSpec((tm // LANES, ng, LANES), lambda i: (i, 0, 0))),
        compiler_params=_params("parallel"),
        name="inproj",
    )(x, mod, w_main, w_v_t, w_gate_t, b_gate)


def _pool_col_matrices(cols):
    t = np.arange(POOL_TILE)
    row, col = t // cols, t % cols
    mats = []
    for win in POOL_WINDOWS:
        hw = win // 2
        lo = np.maximum(col - hw, 0)
        hi = np.minimum(col + hw, cols)
        same_row = row[:, None] == row[None, :]
        inside = (col[None, :] >= lo[:, None]) & (col[None, :] < hi[:, None])
        mats.append((same_row & inside).astype(np.float32))
    return jnp.asarray(np.stack(mats), dtype=BF16)


def _pool_kernel(xp_ref, pmat_ref, wp_ref, ps_ref, out_ref, pad_ref, hl_s, y_s, *, rows, cols, tokens):
    gdim = LANES
    tile = POOL_TILE
    halo = (max(POOL_WINDOWS) // 2) * cols if rows > 1 else 0
    t = lax.broadcasted_iota(jnp.int32, (tile, 1), 0)
    col = t & (cols - 1)
    if rows > 1:
        zeros = jnp.zeros((halo, gdim), F32)
        pad_ref[0:halo, :] = zeros
        pad_ref[halo + tokens:halo + tokens + halo, :] = zeros
    for gi, win in enumerate(POOL_WINDOWS):
        hw = win // 2
        gs = slice(gi * gdim, (gi + 1) * gdim)
        if rows > 1:
            pad_ref[halo:halo + tokens, :] = xp_ref[:, gs]
        cnt_c = (jnp.minimum(col + hw, cols) - jnp.maximum(col - hw, 0)).astype(F32)
        pm = pmat_ref[gi]
        wp = wp_ref[gi]
        scale = ps_ref[:, gs]

        nt = tokens // tile
        for tt in range(nt):
            off = tt * tile
            if rows > 1:
                acc = None
                for dr in range(-hw, hw):
                    term = pad_ref[off + halo + dr * cols:off + halo + dr * cols + tile, :]
                    acc = term if acc is None else acc + term
                rw = (off + t) >> int(math.log2(cols))
                cnt_r = (jnp.minimum(rw + hw, rows) - jnp.maximum(rw - hw, 0)).astype(F32)
                m1 = acc / cnt_r
            else:
                m1 = xp_ref[off:off + tile, gs]
            hi = m1.astype(BF16)
            hl_s[gi * nt + tt] = jnp.concatenate([hi, (m1 - hi.astype(F32)).astype(BF16)], axis=1)
        for tt in range(nt):
            off = tt * tile
            both = _dot(pm, hl_s[gi * nt + tt])
            m2 = (both[:, :gdim] + both[:, gdim:]) / cnt_c
            y_s[gi, off:off + tile, :] = (m2 - xp_ref[off:off + tile, gs]).astype(BF16)
        out_ref[:, gs] = (_dot(y_s[gi], wp) * scale).astype(BF16)


def _pool(proj, pmat, layer, w_pool, pool_scale, rows, cols, step_tokens):
    total = proj.shape[0]
    pool_w = N_POOL_GROUPS * LANES
    halo = (max(POOL_WINDOWS) // 2) * cols if rows > 1 else 0
    kern = functools.partial(_pool_kernel, rows=rows, cols=cols, tokens=step_tokens)
    return pl.pallas_call(
        kern,
        out_shape=jax.ShapeDtypeStruct((total, pool_w), BF16),
        grid=(total // step_tokens,),
        in_specs=[
            pl.BlockSpec((step_tokens, pool_w), lambda i: (i, 0)),
            pl.BlockSpec(pmat.shape, lambda i: (0, 0, 0)),
            pl.BlockSpec((None,) + w_pool.shape[1:], lambda i: (layer, 0, 0, 0)),
            pl.BlockSpec((None, 1, pool_w), lambda i: (layer, 0, 0)),
        ],
        out_specs=pl.BlockSpec((step_tokens, pool_w), lambda i: (i, 0)),
        scratch_shapes=[
            pltpu.VMEM((step_tokens + 2 * halo if rows > 1 else SUBLANES, LANES), F32),
            pltpu.VMEM((N_POOL_GROUPS * (step_tokens // POOL_TILE), POOL_TILE, 2 * LANES), BF16),
            pltpu.VMEM((N_POOL_GROUPS, step_tokens, LANES), BF16),
        ],
        compiler_params=_params("parallel"),
        name="pool",
    )(proj, pmat, w_pool, pool_scale)


def _log_sigmoid(x):
    return jnp.minimum(x, 0.0) - jnp.log(1.0 + jnp.exp(-jnp.abs(x)))


def _split3(x):
    hi = x.astype(BF16)
    r1 = x - hi.astype(F32)
    mid = r1.astype(BF16)
    lo = (r1 - mid.astype(F32)).astype(BF16)
    return lo, mid, hi


def _mlstm_kernel(*refs, bp, seq, chunk, hp, zero_init, emit_state, n_unused, state_slot):
    nc = seq // chunk
    for e in range(bp):
        def tok(ref):
            return ref.at[pl.ds(e * seq, seq)]

        def chk(ref):
            return ref.at[pl.ds(e * nc, nc)]

        views = [tok(refs[0]), tok(refs[1]), chk(refs[2]), tok(refs[3]), chk(refs[4]), refs[5], refs[6], refs[7]]
        pos = 8
        if not zero_init:
            views += [r.at[e] for r in refs[pos:pos + 3]]
            pos += 3
        views += list(refs[pos:pos + n_unused])
        pos += n_unused
        views.append(tok(refs[pos]))
        pos += 1
        if emit_state:
            views += [r.at[e] for r in refs[pos:pos + 3]]
            pos += 3
        views += [r.at[e] for r in refs[pos:]]
        _mlstm_sequence(*views, seq=seq, chunk=chunk, hp=hp, zero_init=zero_init, emit_state=emit_state,
                        n_unused=n_unused, state_slot=state_slot)


def _mlstm_sequence(*refs, seq, chunk, hp, zero_init, emit_state, n_unused, state_slot):
    refs = list(refs)
    q_ref, k_ref, vt_ref, o_ref, g_ref, wq_ref, wk_ref, hng_ref = refs[:8]
    pos = 8
    if not zero_init:
        c0_ref, n0_ref, m0_ref = refs[pos:pos + 3]
        pos += 3
    pos += n_unused
    out_ref = refs[pos]
    pos += 1
    if emit_state:
        c_ref, n_ref, m_ref = refs[pos:pos + 3]
        pos += 3
    qpad, kpad, qs, ks, ht_f, ht_b, rcb_s, br_s, rc_s, ct_s, dt_s, mrow_s, z1_s, rhs_s, z2_s = refs[pos:]
    dh = q_ref.shape[1] // hp
    nc = seq // chunk
    npair = 2 * hp
    aug = dh + BF16_ROWS
    margin = SUBLANES
    unroll = nc <= MLSTM_UNROLL_CHUNKS

    for pad_ref, src_ref in ((qpad, q_ref), (kpad, k_ref)):
        pad_ref[0:margin, :] = jnp.zeros((margin, hp * dh), F32)
        pad_ref[margin + seq:2 * margin + seq, :] = jnp.zeros((margin, hp * dh), F32)
        pad_ref[margin:margin + seq, :] = src_ref[...]

    def conv_silu(pad_ref, w_ref, c, hs):
        w = w_ref[:, hs]
        start = c * chunk if unroll else pl.multiple_of(c * chunk, chunk)
        win = pad_ref[pl.ds(start, chunk + 2 * margin), hs]
        y = (pltpu.roll(win, 1, 0) * w[0:1, :] + win * w[1:2, :]
             + pltpu.roll(win, chunk + 2 * margin - 1, 0) * w[2:3, :])[margin:margin + chunk]
        return y * jax.nn.sigmoid(y)

    ri = lax.broadcasted_iota(jnp.int32, (chunk, chunk), 0)
    ci = lax.broadcasted_iota(jnp.int32, (chunk, chunk), 1)
    lower = ci <= ri
    upper = ci >= ri
    upper_b = jnp.where(upper, 1.0, 0.0).astype(BF16)
    row_is_fwd = lax.broadcasted_iota(jnp.int32, (npair, chunk), 0) < hp
    ones_row = jnp.where(lax.broadcasted_iota(jnp.int32, (BF16_ROWS, chunk), 0) == 0, 1.0, 0.0).astype(BF16)

    gates = g_ref[...]
    lf = _log_sigmoid(gates[:, npair:, :])
    lf3 = _split3(lf.reshape(nc * npair, chunk))
    prefix = (_dot(lf3[0], upper_b) + _dot(lf3[1], upper_b) + _dot(lf3[2], upper_b)).reshape(nc, npair, chunk)
    suffix = prefix[:, :, chunk - 1:chunk] - prefix + lf
    br = jnp.where(row_is_fwd, prefix, suffix)
    br_s[...] = br
    rc_s[...] = gates[:, :npair, :] - br

    def rows_of(c):
        return slice(c * chunk, (c + 1) * chunk) if unroll else pl.ds(pl.multiple_of(c * chunk, chunk), chunk)

    def prep(c, carry):
        sl = rows_of(c)
        rc = rc_s[c]
        for p in range(npair):
            rcb_s[sl, p * LANES:(p + 1) * LANES] = jnp.transpose(jnp.broadcast_to(rc[p:p + 1, :], (LANES, chunk)))
        for hh in range(hp):
            hs = slice(hh * dh, (hh + 1) * dh)
            qs[sl, hs] = conv_silu(qpad, wq_ref, c, hs).astype(BF16)
            ks[sl, hs] = (conv_silu(kpad, wk_ref, c, hs) * (dh ** -0.5)).astype(BF16)
        return carry

    n_rows = lax.broadcasted_iota(jnp.int32, (BF16_ROWS, dh), 0) == 0
    ms = []
    for p in range(npair):
        di, hh = divmod(p, hp)
        if zero_init:
            ct_s[p] = jnp.zeros((aug, dh), F32)
            ms.append(jnp.zeros((1, 1), F32))
        else:
            ct_s[p, 0:dh, :] = jnp.transpose(c0_ref[di, hh])
            ct_s[p, dh:aug, :] = jnp.where(n_rows, n0_ref[di, hh], 0.0)
            ms.append(m0_ref[di, hh])

    def pair_args(p, i):
        di, hh = divmod(p, hp)
        fwd = di == 0
        c = i if fwd else nc - 1 - i
        last = chunk - 1 if fwd else 0
        sl = rows_of(c)
        return fwd, hh, c, last, sl, slice(hh * dh, (hh + 1) * dh)

    def body(i, ms):
        for p in range(npair):
            fwd, hh, c, last, sl, hs = pair_args(p, i)
            rcb = rcb_s[sl, p * LANES:(p + 1) * LANES]
            xt = jnp.where(upper if fwd else lower, rcb, -jnp.inf)
            mrow = jnp.maximum(jnp.max(xt, axis=0, keepdims=True), ms[p])
            dt_s[p] = jnp.exp(xt - mrow)
            mrow_s[p] = jnp.broadcast_to(mrow, (SUBLANES, chunk))
            w = jnp.exp(rcb - mrow[:, last:last + 1])
            rhs_s[p, :, chunk:] = (w * ks[sl, hs].astype(F32)).astype(BF16)
        zeros_q = jnp.zeros((chunk, dh), BF16)
        for pk in range(npair // 2):
            _, _, _, _, sl_a, hs_a = pair_args(2 * pk, i)
            _, _, _, _, sl_b, hs_b = pair_args(2 * pk + 1, i)
            lhs = jnp.concatenate([
                jnp.concatenate([ks[sl_a, hs_a], ks[sl_b, hs_b]], axis=1),
                jnp.concatenate([ct_s[2 * pk].astype(BF16), ct_s[2 * pk + 1].astype(BF16)], axis=1)], axis=0)
            q_diag = jnp.concatenate([
                jnp.concatenate([qs[sl_a, hs_a], zeros_q], axis=1),
                jnp.concatenate([zeros_q, qs[sl_b, hs_b]], axis=1)], axis=0)
            z1_s[pk] = _dot_nt(lhs, q_diag)
        for p in range(npair):
            lanes = slice((p % 2) * chunk, (p % 2 + 1) * chunk)
            rhs_s[p, :, 0:chunk] = (z1_s[p // 2, 0:chunk, lanes] * dt_s[p]).astype(BF16)
        for p in range(npair):
            fwd, hh, c, last, sl, hs = pair_args(p, i)
            z2_s[p] = _dot(jnp.concatenate([vt_ref[c, hs, :], ones_row], axis=0), rhs_s[p])
        new_ms = []
        for p in range(npair):
            fwd, hh, c, last, sl, hs = pair_args(p, i)
            m = ms[p]
            mrow = mrow_s[p, 0:1, :]
            lanes = slice((p % 2) * chunk, (p % 2 + 1) * chunk)
            num = z1_s[p // 2, chunk:, lanes] * jnp.exp(m - mrow) + z2_s[p, :, 0:chunk]
            b_r = br_s[c, p:p + 1, :]
            scale = 1.0 / jnp.maximum(jnp.abs(num[dh:dh + 1, :]), jnp.exp(-(b_r + mrow)))
            h_t = num[0:dh, :] * scale
            if fwd:
                ht_f[hh * nc + c] = h_t
            else:
                ht_b[hh * nc + c] = h_t
            mlast = mrow[:, last:last + 1]
            ct_s[p] = jnp.exp(m - mlast) * ct_s[p] + z2_s[p, :, chunk:]
            new_ms.append(b_r[:, last:last + 1] + mlast)
        return tuple(new_ms)

    def finish(c, carry):
        sl = rows_of(c)
        for hh in range(hp):
            hs = slice(hh * dh, (hh + 1) * dh)
            h_t = ht_f[hh * nc + c] + ht_b[hh * nc + c]
            mu = jnp.mean(h_t, axis=0, keepdims=True)
            hc = h_t - mu
            var = jnp.mean(hc * hc, axis=0, keepdims=True)
            hn = jnp.transpose(hc * lax.rsqrt(var + LN_EPS)) * hng_ref[:, hs]
            out_ref[sl, hs] = (jax.nn.sigmoid(o_ref[sl, hs]) * hn).astype(BF16)
        return carry

    if unroll:
        for c in range(nc):
            prep(c, 0)
        ms = tuple(ms)
        for i in range(nc):
            ms = body(i, ms)
        for c in range(nc):
            finish(c, 0)
    else:
        half = nc // 2
        prep(0, 0)
        prep(nc - 1, 0)

        def first_half(i, ms):
            ms = body(i, ms)
            prep(i + 1, 0)
            prep(nc - 2 - i, 0)
            return ms

        def second_half(i, ms):
            ms = body(i, ms)
            finish(i, 0)
            finish(nc - 1 - i, 0)
            return ms

        ms = lax.fori_loop(0, half, first_half, tuple(ms), unroll=True)
        ms = lax.fori_loop(half, nc, second_half, ms, unroll=True)

    if emit_state:
        for slot in range(c_ref.shape[0]):
            if slot != state_slot:
                c_ref[slot] = jnp.zeros(c_ref.shape[1:], F32)
                n_ref[slot] = jnp.zeros(n_ref.shape[1:], F32)
                m_ref[slot] = jnp.zeros(m_ref.shape[1:], F32)
        for p in range(npair):
            di, hh = divmod(p, hp)
            c_ref[state_slot, di, hh] = jnp.transpose(ct_s[p, 0:dh, :])
            n_ref[state_slot, di, hh] = ct_s[p, dh:dh + 1, :]
            m_ref[state_slot, di, hh] = ms[p]


def _mlstm_heads_per_step(seq, dh):
    return max(1, min(MLSTM_HEADS, MLSTM_STEP_BYTES // (seq * dh * 4)))


def _gate_row_order(hp):
    heads = MLSTM_HEADS
    order = []
    for g in range(heads // hp):
        for kind in range(2):
            for di in range(2):
                for hh in range(hp):
                    order.append((di * 2 + kind) * heads + g * hp + hh)
    return np.asarray(order)


def _mlstm(proj, v_t, gates_t, wl, w_qk_conv, hn_g, batch, seq, col0, dh, init_state=None, layer=None, depth=None,
           prev=None):
    heads = MLSTM_HEADS
    chunk = MLSTM_CHUNK
    assert chunk == LANES and seq % chunk == 0 and (seq // chunk <= MLSTM_UNROLL_CHUNKS or seq // chunk % 2 == 0)
    nc = seq // chunk
    hp = _mlstm_heads_per_step(seq, dh)
    hg = heads // hp
    npair = 2 * hp
    aug = dh + BF16_ROWS
    width = hp * dh
    cb = col0 // width
    gb = heads * dh // width
    zero_init = init_state is None
    emit_state = layer is not None
    prev = () if prev is None else tuple(prev)
    slots, first_slot = (1, layer) if prev else (depth, 0)
    bp = MLSTM_SHORT_SEQS_PER_STEP if nc <= MLSTM_UNROLL_CHUNKS and batch % MLSTM_SHORT_SEQS_PER_STEP == 0 else 1
    kern = functools.partial(_mlstm_kernel, bp=bp, seq=seq, chunk=chunk, hp=hp, zero_init=zero_init,
                             emit_state=emit_state, n_unused=len(prev),
                             state_slot=(layer - first_slot) if emit_state else None)

    def tok_spec(group):
        return pl.BlockSpec((bp * seq, width), lambda b, g: (b, cb + group * gb + g))

    in_specs = [
        tok_spec(0), tok_spec(1), pl.BlockSpec((bp * nc, width, chunk), lambda b, g: (b, g, 0)), tok_spec(2),
        pl.BlockSpec((bp * nc, 2 * npair, chunk), lambda b, g: (b, g, 0)),
        pl.BlockSpec((None, 3, width), lambda b, g: (wl, 0, g)),
        pl.BlockSpec((None, 3, width), lambda b, g: (wl, 0, gb + g)),
        pl.BlockSpec((None, 1, width), lambda b, g: (wl, 0, g)),
    ]
    args = [proj, proj, v_t, proj, gates_t, w_qk_conv, w_qk_conv, hn_g]
    if not zero_init:
        in_specs += [pl.BlockSpec((bp, None, 2, hp, dh, dh), lambda b, g: (b, wl, 0, g, 0, 0)),
                     pl.BlockSpec((bp, None, 2, hp, 1, dh), lambda b, g: (b, wl, 0, g, 0, 0)),
                     pl.BlockSpec((bp, None, 2, hp, 1, 1), lambda b, g: (b, wl, 0, g, 0, 0))]
        args += list(init_state)
    aliases = {}
    for k, arr in enumerate(prev):
        aliases[len(args)] = 1 + k
        in_specs.append(pl.BlockSpec(memory_space=pl.ANY))
        args.append(arr)
    out_shape = [jax.ShapeDtypeStruct((batch * seq, heads * dh), BF16)]
    out_specs = [pl.BlockSpec((bp * seq, width), lambda b, g: (b, g))]
    if emit_state:
        out_shape += [jax.ShapeDtypeStruct((batch, depth, 2, heads, dh, dh), F32),
                      jax.ShapeDtypeStruct((batch, depth, 2, heads, 1, dh), F32),
                      jax.ShapeDtypeStruct((batch, depth, 2, heads, 1, 1), F32)]
        out_specs += [pl.BlockSpec((bp, slots, 2, hp, dh, dh), lambda b, g: (b, first_slot, 0, g, 0, 0)),
                      pl.BlockSpec((bp, slots, 2, hp, 1, dh), lambda b, g: (b, first_slot, 0, g, 0, 0)),
                      pl.BlockSpec((bp, slots, 2, hp, 1, 1), lambda b, g: (b, first_slot, 0, g, 0, 0))]
    res = pl.pallas_call(
        kern,
        out_shape=tuple(out_shape),
        grid=(batch // bp, hg),
        in_specs=in_specs,
        out_specs=tuple(out_specs),
        scratch_shapes=[pltpu.VMEM((bp,) + shape, dtype) for shape, dtype in (
            ((seq + 2 * SUBLANES, width), F32), ((seq + 2 * SUBLANES, width), F32),
            ((seq, width), BF16), ((seq, width), BF16),
            ((hp * nc, dh, chunk), F32), ((hp * nc, dh, chunk), F32),
            ((seq, npair * LANES), F32), ((nc, npair, chunk), F32),
            ((nc, npair, chunk), F32),
            ((npair, aug, dh), F32),
            ((npair, chunk, chunk), F32), ((npair, SUBLANES, chunk), F32),
            ((npair // 2, chunk + aug, 2 * chunk), F32), ((npair, chunk, chunk + dh), BF16),
            ((npair, aug, chunk + dh), F32))],
        input_output_aliases=aliases,
        compiler_params=_params("parallel", "parallel"),
        name="mlstm",
    )(*args)
    return res[0], tuple(res[1:])


def _outproj_kernel(pool_ref, ml_ref, x_ref, mod_ref, wp_ref, wm_ref, g_ref, b_ref, x1_ref, h2_ref, *, alpha):
    sub = TOKEN_SUB
    for s in range(x_ref.shape[0] // sub):
        rs = slice(s * sub, (s + 1) * sub)
        mix = _dot(pool_ref[rs, :], wp_ref[...]) + _dot(ml_ref[rs, :], wm_ref[...])
        x1 = _layer_norm(alpha * x_ref[rs, :] + mod_ref[2:3, :] * mix) * g_ref[...] + b_ref[...]
        x1_ref[rs, :] = x1
        h2 = _layer_norm(x1) * (1.0 + mod_ref[4:5, :]) + mod_ref[3:4, :]
        h2_ref[rs, :] = h2.astype(BF16)


def _outproj(pool_out, ml_out, x, mod, cond_row, layer, w_out, ln_g, ln_b, alpha):
    tokens, d = x.shape
    half = pool_out.shape[1]
    tm = TOKEN_TILE
    kern = functools.partial(_outproj_kernel, alpha=alpha)
    return pl.pallas_call(
        kern,
        out_shape=(jax.ShapeDtypeStruct((tokens, d), F32), jax.ShapeDtypeStruct((tokens, d), BF16)),
        grid=(tokens // tm,),
        in_specs=[
            pl.BlockSpec((tm, half), lambda i: (i, 0)),
            pl.BlockSpec((tm, half), lambda i: (i, 0)),
            pl.BlockSpec((tm, d), lambda i: (i, 0)),
            pl.BlockSpec((None, None, 6, d), lambda i: (layer, cond_row(i * tm), 0, 0)),
            pl.BlockSpec((None, half, d), lambda i: (layer, 0, 0)),
            pl.BlockSpec((None, half, d), lambda i: (layer, 1, 0)),
            pl.BlockSpec((None, 1, d), lambda i: (layer, 0, 0)),
            pl.BlockSpec((None, 1, d), lambda i: (layer, 0, 0)),
        ],
        out_specs=(pl.BlockSpec((tm, d), lambda i: (i, 0)), pl.BlockSpec((tm, d), lambda i: (i, 0))),
        compiler_params=_params("parallel"),
        name="outproj",
    )(pool_out, ml_out, x, mod, w_out, w_out, ln_g, ln_b)


def _gelu_tanh(x):
    return 0.5 * x * (1.0 + jnp.tanh(math.sqrt(2.0 / math.pi) * (x + 0.044715 * (x * x * x))))


def _ffn_kernel(h_ref, x1_ref, mod_ref, wa_ref, wg_ref, wdw_ref, bdw_ref, wd_ref, lng_ref, lnb_ref, out_ref,
                gpad, a_s, u_s, wa_b, wg_b, wd_b, *, rows, cols, tokens, alpha):
    j = pl.program_id(1)
    tf = wa_ref.shape[1]
    blk = FFN_BLOCK
    nb = tokens // blk
    base = (cols if rows > 1 else 0) + SUBLANES

    @pl.when(j == 0)
    def _():
        out_ref[...] = jnp.zeros_like(out_ref)

    wa_b[...] = wa_ref[...].astype(BF16)
    wg_b[...] = wg_ref[...].astype(BF16)
    wd_b[...] = wd_ref[...].astype(BF16)
    gpad[0:base, :] = jnp.zeros((base, tf), F32)
    gpad[base + tokens:base + tokens + base, :] = jnp.zeros((base, tf), F32)

    wdw = wdw_ref[...]
    bdw = bdw_ref[...]
    sub = max(cols, FFN_CONV_ROWS)
    ci = lax.broadcasted_iota(jnp.int32, (sub, 1), 0) & (cols - 1)
    first = ci == 0
    last = ci == cols - 1
    taps_y = (-1, 0, 1) if rows > 1 else (0,)

    def up(b):
        hb = h_ref[b * blk:(b + 1) * blk, :]
        a_s[b * blk:(b + 1) * blk, :] = _dot(hb, wa_b[...])
        gpad[base + b * blk:base + (b + 1) * blk, :] = _dot(hb, wg_b[...])

    def conv_gate(b):
        for r in range(b * blk // sub, (b + 1) * blk // sub):
            off = r * sub
            for lb in range(tf // LANES):
                ls = slice(lb * LANES, (lb + 1) * LANES)
                left = mid = right = None
                for dy in taps_y:
                    r0 = base + off + dy * cols
                    kk = (dy + 1) * 3
                    tl = gpad[r0 - 1:r0 - 1 + sub, ls] * wdw[kk:kk + 1, ls]
                    tc = gpad[r0:r0 + sub, ls] * wdw[kk + 1:kk + 2, ls]
                    tr = gpad[r0 + 1:r0 + 1 + sub, ls] * wdw[kk + 2:kk + 3, ls]
                    left = tl if left is None else left + tl
                    mid = tc if mid is None else mid + tc
                    right = tr if right is None else right + tr
                y = mid + jnp.where(first, 0.0, left) + jnp.where(last, 0.0, right) + bdw[:, ls]
                u_s[off:off + sub, ls] = (_gelu_tanh(y) * a_s[off:off + sub, ls]).astype(BF16)

    def down(b):
        out_ref[b * blk:(b + 1) * blk, :] += _dot(u_s[b * blk:(b + 1) * blk, :], wd_b[...])

    for step in range(nb + 2):
        if step < nb:
            up(step)
        if 1 <= step <= nb:
            conv_gate(step - 1)
        if step >= 2:
            down(step - 2)

    @pl.when(j == pl.num_programs(1) - 1)
    def _():
        gate = mod_ref[5:6, :]
        lng = lng_ref[...]
        lnb = lnb_ref[...]

        def ln_body(r, carry):
            sl = pl.ds(pl.multiple_of(r * LN_ROWS, LN_ROWS), LN_ROWS)
            z = alpha * x1_ref[sl, :] + gate * out_ref[sl, :]
            out_ref[sl, :] = _layer_norm(z) * lng + lnb
            return carry

        lax.fori_loop(0, tokens // LN_ROWS, ln_body, 0)


def _ffn(h2, x1, mod, cond_row, layer, w_up, w_dw, b_dw, w_down, ln_g, ln_b, rows, cols, alpha):
    total, d = x1.shape
    d_ff = w_down.shape[1]
    tm = FFN_TOKENS
    tf = FFN_COLS
    nj = d_ff // tf
    kern = functools.partial(_ffn_kernel, rows=rows, cols=cols, tokens=tm, alpha=alpha)
    return pl.pallas_call(
        kern,
        out_shape=jax.ShapeDtypeStruct((total, d), F32),
        grid=(total // tm, nj),
        in_specs=[
            pl.BlockSpec((tm, d), lambda i, j: (i, 0)),
            pl.BlockSpec((tm, d), lambda i, j: (i, 0)),
            pl.BlockSpec((None, None, 6, d), lambda i, j: (layer, cond_row(i * tm), 0, 0)),
            pl.BlockSpec((None, d, tf), lambda i, j: (layer, 0, j)),
            pl.BlockSpec((None, d, tf), lambda i, j: (layer, 0, nj + j)),
            pl.BlockSpec((None, 9, tf), lambda i, j: (layer, 0, j)),
            pl.BlockSpec((None, 1, tf), lambda i, j: (layer, 0, j)),
            pl.BlockSpec((None, tf, d), lambda i, j: (layer, j, 0)),
            pl.BlockSpec((None, 1, d), lambda i, j: (layer, 0, 0)),
            pl.BlockSpec((None, 1, d), lambda i, j: (layer, 0, 0)),
        ],
        out_specs=pl.BlockSpec((tm, d), lambda i, j: (i, 0)),
        scratch_shapes=[
            pltpu.VMEM((tm + 2 * ((cols if rows > 1 else 0) + SUBLANES), tf), F32),
            pltpu.VMEM((tm, tf), F32),
            pltpu.VMEM((tm, tf), BF16),
            pltpu.VMEM((d, tf), BF16), pltpu.VMEM((d, tf), BF16), pltpu.VMEM((tf, d), BF16),
        ],
        compiler_params=_params("parallel", "arbitrary"),
        name="ffn",
    )(h2, x1, mod, w_up, w_up, w_dw, b_dw, w_down, ln_g, ln_b)


def _trunk_layer(x, batch, rows, cols, cond_row, layer, p, pmat, alpha, dh, state_kw):
    seq = rows * cols
    hp = _mlstm_heads_per_step(seq, dh)
    mod = p["ada"]
    proj, v_t, gates_t = _inproj(x, mod, cond_row, layer, p["w_in_main"], p["w_v_t"], p["w_gate_t"][hp],
                                 p["b_gate"][hp])
    pool_out = _pool(proj, pmat, layer, p["w_pool"], p["pool_scale"], rows, cols, min(FFN_TOKENS, batch * seq))
    ml_out, states = _mlstm(proj, v_t, gates_t, layer, p["w_qk_conv"], p["hn_g"], batch, seq, N_POOL_GROUPS * LANES,
                            dh, **state_kw)
    x1, h2 = _outproj(pool_out, ml_out, x, mod, cond_row, layer, p["w_out"], p["ln1_g"], p["ln1_b"], alpha)
    x2 = _ffn(h2, x1, mod, cond_row, layer, p["w_up"], p["w_dw"], p["b_dw"], p["w_down"], p["ln2_g"], p["ln2_b"],
              rows, cols, alpha)
    return x2, states


def kernel(x_prompt, x_sample, c, state_C, state_n, state_m, c_ctx, w_ada, b_ada, w_in, b_gates, w_qk_conv, hn_g,
           w_pool, pool_scale, w_out, ln1_g, ln1_b, w_up, w_dw, b_dw, w_down, ln2_g, ln2_b):
    b_p, l_p, d = x_prompt.shape
    b_s, l_s, _ = x_sample.shape
    depth = w_ada.shape[0]
    heads = MLSTM_HEADS
    dh = state_C.shape[-1]
    pool_w = N_POOL_GROUPS * LANES
    mw = heads * dh
    n_main = pool_w + 4 * mw
    v0 = pool_w + 2 * mw
    d_ff = w_down.shape[1]
    alpha = (2.0 * depth) ** 0.25
    rows_s = l_s // GRID_W
    assert b_s + 1 <= COND_ROWS and l_p % POOL_TILE == 0 and POOL_TILE % GRID_W == 0

    cond = jnp.concatenate([c_ctx[None, :], c, jnp.zeros((COND_ROWS - 1 - b_s, d), F32)], axis=0)

    pmat_p = _pool_col_matrices(l_p)
    pmat_s = _pool_col_matrices(GRID_W)
    assert min(FFN_TOKENS, b_s * l_s) == l_s
    hps = sorted({_mlstm_heads_per_step(l_p, dh), _mlstm_heads_per_step(l_s, dh)})

    w_gate = w_in[:, :, n_main:]
    p = dict(
        ada=_ada(cond, w_ada, b_ada).reshape(depth, COND_ROWS, 6, d),
        w_in_main=jnp.concatenate([w_in[:, :, :v0], w_in[:, :, v0 + mw:n_main]], axis=2).astype(BF16),
        w_v_t=w_in[:, :, v0:v0 + mw].transpose(0, 2, 1).astype(BF16),
        w_gate_t={hp: w_gate[:, :, _gate_row_order(hp)].transpose(0, 2, 1).astype(BF16) for hp in hps},
        b_gate={hp: b_gates[:, _gate_row_order(hp)][:, :, None] for hp in hps},
        w_qk_conv=w_qk_conv, hn_g=hn_g[:, None, :],
        w_pool=w_pool.astype(BF16), pool_scale=pool_scale[:, None, :],
        w_out=w_out.astype(BF16), ln1_g=ln1_g[:, None, :], ln1_b=ln1_b[:, None, :],
        w_up=w_up, w_dw=w_dw.reshape(depth, 9, d_ff), b_dw=b_dw[:, None, :],
        w_down=w_down, ln2_g=ln2_g[:, None, :], ln2_b=ln2_b[:, None, :])
    init = (state_C, state_n.reshape(b_s, depth, 2, heads, 1, dh), state_m.reshape(b_s, depth, 2, heads, 1, 1))

    y_p = x_prompt.reshape(b_p * l_p, d)
    y_s = x_sample.reshape(b_s * l_s, d)
    states = None
    for l in range(depth):
        y_p, states = _trunk_layer(y_p, b_p, 1, l_p, lambda t: 0, l, p, pmat_p, alpha, dh,
                                   dict(layer=l, depth=depth, prev=states))
        y_s, _ = _trunk_layer(y_s, b_s, rows_s, GRID_W, lambda t: 1 + t // l_s, l, p, pmat_s, alpha, dh,
                              dict(init_state=init))
    new_c, new_n, new_m = states
    return (y_p.reshape(b_p, l_p, d), y_s.reshape(b_s, l_s, d), new_c, new_n.reshape(b_p, depth, 2, heads, dh),
            new_m.reshape(b_p, depth, 2, heads))
```

```python
import functools
import math

import numpy as np
import jax
import jax.numpy as jnp
from jax import lax
from jax.experimental import pallas as pl
from jax.experimental.pallas import tpu as pltpu

F32 = jnp.float32
BF16 = jnp.bfloat16

GRID_W = 64
POOL_WINDOWS = (2, 4, 8, 16)
N_POOL_GROUPS = len(POOL_WINDOWS)
MLSTM_HEADS = 4
LN_EPS = 1e-5

LANES = 128
SUBLANES = 8
BF16_ROWS = 16
VMEM_LIMIT_BYTES = 56 * 1024 * 1024

COND_ROWS = SUBLANES
MLSTM_CHUNK = LANES
MLSTM_UNROLL_CHUNKS = 2
MLSTM_STEP_BYTES = 2 * 1024 * 1024
POOL_TILE = 256
TOKEN_TILE = 1024
TOKEN_SUB = 256
FFN_TOKENS = 2048
FFN_BLOCK = 512
FFN_COLS = 256
FFN_CONV_ROWS = 256
LN_ROWS = 256


def _params(*sem):
    return pltpu.CompilerParams(dimension_semantics=sem, vmem_limit_bytes=VMEM_LIMIT_BYTES)


def _layer_norm(x):
    mu = jnp.mean(x, axis=-1, keepdims=True)
    xc = x - mu
    var = jnp.mean(xc * xc, axis=-1, keepdims=True)
    return xc * lax.rsqrt(var + LN_EPS)


def _dot(a, b):
    return jnp.dot(a, b, preferred_element_type=F32)


def _dot_nt(a, b):
    return lax.dot_general(a, b, (((1,), (1,)), ((), ())), preferred_element_type=F32)


def _ada_kernel(cond_ref, w_ref, b_ref, out_ref):
    cnd = cond_ref[...]
    act = (cnd * jax.nn.sigmoid(cnd)).astype(BF16)
    out_ref[...] = _dot(act, w_ref[...].astype(BF16)) + b_ref[...]


def _ada(cond, w_ada, b_ada):
    depth, d, n = w_ada.shape
    tn = 1536
    return pl.pallas_call(
        _ada_kernel,
        out_shape=jax.ShapeDtypeStruct((depth, COND_ROWS, n), F32),
        grid=(depth, n // tn),
        in_specs=[
            pl.BlockSpec((COND_ROWS, d), lambda l, j: (0, 0)),
            pl.BlockSpec((None, d, tn), lambda l, j: (l, 0, j)),
            pl.BlockSpec((None, 1, tn), lambda l, j: (l, 0, j)),
        ],
        out_specs=pl.BlockSpec((None, COND_ROWS, tn), lambda l, j: (l, 0, j)),
        compiler_params=_params("parallel", "parallel"),
        name="ada",
    )(cond, w_ada, b_ada.reshape(depth, 1, n))


def _inproj_kernel(x_ref, mod_ref, w_ref, wvt_ref, wgt_ref, bg_ref, proj_ref, vt_ref, gates_ref):
    sub = TOKEN_SUB
    for s in range(x_ref.shape[0] // sub):
        rs = slice(s * sub, (s + 1) * sub)
        h = _layer_norm(x_ref[rs, :]) * (1.0 + mod_ref[1:2, :]) + mod_ref[0:1, :]
        h = h.astype(BF16)
        proj_ref[rs, :] = _dot(h, w_ref[...])
        v_t = _dot_nt(wvt_ref[...], h).astype(BF16)
        gates_t = _dot_nt(wgt_ref[...], h) + bg_ref[...]
        for k in range(sub // LANES):
            vt_ref[s * (sub // LANES) + k] = v_t[:, k * LANES:(k + 1) * LANES]
            gates_ref[s * (sub // LANES) + k] = gates_t[:, k * LANES:(k + 1) * LANES]


def _inproj(x, mod, cond_row, layer, w_main, w_v_t, w_gate_t, b_gate):
    tokens, d = x.shape
    n = w_main.shape[2]
    nv = w_v_t.shape[1]
    ng = w_gate_t.shape[1]
    tm = TOKEN_TILE
    return pl.pallas_call(
        _inproj_kernel,
        out_shape=(jax.ShapeDtypeStruct((tokens, n), F32), jax.ShapeDtypeStruct((tokens // LANES, nv, LANES), BF16),
                   jax.ShapeDtypeStruct((tokens // LANES, ng, LANES), F32)),
        grid=(tokens // tm,),
        in_specs=[
            pl.BlockSpec((tm, d), lambda i: (i, 0)),
            pl.BlockSpec((None, None, 6, d), lambda i: (layer, cond_row(i * tm), 0, 0)),
            pl.BlockSpec((None, d, n), lambda i: (layer, 0, 0)),
            pl.BlockSpec((None, nv, d), lambda i: (layer, 0, 0)),
            pl.BlockSpec((None, ng, d), lambda i: (layer, 0, 0)),
            pl.BlockSpec((None, ng, 1), lambda i: (layer, 0, 0)),
        ],
        out_specs=(pl.BlockSpec((tm, n), lambda i: (i, 0)),
                   pl.BlockSpec((tm // LANES, nv, LANES), lambda i: (i, 0, 0)),
                   pl.BlockSpec((tm // LANES, ng, LANES), lambda i: (i, 0, 0))),
        compiler_params=_params("parallel"),
        name="inproj",
    )(x, mod, w_main, w_v_t, w_gate_t, b_gate)


def _pool_col_matrices(cols):
    t = np.arange(POOL_TILE)
    row, col = t // cols, t % cols
    mats = []
    for win in POOL_WINDOWS:
        hw = win // 2
        lo = np.maximum(col - hw, 0)
        hi = np.minimum(col + hw, cols)
        same_row = row[:, None] == row[None, :]
        inside = (col[None, :] >= lo[:, None]) & (col[None, :] < hi[:, None])
        mats.append((same_row & inside).astype(np.float32))
    return jnp.asarray(np.stack(mats), dtype=BF16)


def _pool_kernel(xp_ref, pmat_ref, wp_ref, ps_ref, out_ref, pad_ref, hl_s, y_s, *, rows, cols, tokens):
    gdim = LANES
    tile = POOL_TILE
    halo = (max(POOL_WINDOWS) // 2) * cols if rows > 1 else 0
    t = lax.broadcasted_iota(jnp.int32, (tile, 1), 0)
    col = t & (cols - 1)
    if rows > 1:
        zeros = jnp.zeros((halo, gdim), F32)
        pad_ref[0:halo, :] = zeros
        pad_ref[halo + tokens:halo + tokens + halo, :] = zeros
    for gi, win in enumerate(POOL_WINDOWS):
        hw = win // 2
        gs = slice(gi * gdim, (gi + 1) * gdim)
        if rows > 1:
            pad_ref[halo:halo + tokens, :] = xp_ref[:, gs]
        cnt_c = (jnp.minimum(col + hw, cols) - jnp.maximum(col - hw, 0)).astype(F32)
        pm = pmat_ref[gi]
        wp = wp_ref[gi]
        scale = ps_ref[:, gs]

        nt = tokens // tile
        for tt in range(nt):
            off = tt * tile
            if rows > 1:
                acc = None
                for dr in range(-hw, hw):
                    term = pad_ref[off + halo + dr * cols:off + halo + dr * cols + tile, :]
                    acc = term if acc is None else acc + term
                rw = (off + t) >> int(math.log2(cols))
                cnt_r = (jnp.minimum(rw + hw, rows) - jnp.maximum(rw - hw, 0)).astype(F32)
                m1 = acc / cnt_r
            else:
                m1 = xp_ref[off:off + tile, gs]
            hi = m1.astype(BF16)
            hl_s[gi * nt + tt] = jnp.concatenate([hi, (m1 - hi.astype(F32)).astype(BF16)], axis=1)
        for tt in range(nt):
            off = tt * tile
            both = _dot(pm, hl_s[gi * nt + tt])
            m2 = (both[:, :gdim] + both[:, gdim:]) / cnt_c
            y_s[gi, off:off + tile, :] = (m2 - xp_ref[off:off + tile, gs]).astype(BF16)
        out_ref[:, gs] = (_dot(y_s[gi], wp) * scale).astype(BF16)


def _pool(proj, pmat, layer, w_pool, pool_scale, rows, cols, step_tokens):
    total = proj.shape[0]
    pool_w = N_POOL_GROUPS * LANES
    halo = (max(POOL_WINDOWS) // 2) * cols if rows > 1 else 0
    kern = functools.partial(_pool_kernel, rows=rows, cols=cols, tokens=step_tokens)
    return pl.pallas_call(
        kern,
        out_shape=jax.ShapeDtypeStruct((total, pool_w), BF16),
        grid=(total // step_tokens,),
        in_specs=[
            pl.BlockSpec((step_tokens, pool_w), lambda i: (i, 0)),
            pl.BlockSpec(pmat.shape, lambda i: (0, 0, 0)),
            pl.BlockSpec((None,) + w_pool.shape[1:], lambda i: (layer, 0, 0, 0)),
            pl.BlockSpec((None, 1, pool_w), lambda i: (layer, 0, 0)),
        ],
        out_specs=pl.BlockSpec((step_tokens, pool_w), lambda i: (i, 0)),
        scratch_shapes=[
            pltpu.VMEM((step_tokens + 2 * halo if rows > 1 else SUBLANES, LANES), F32),
            pltpu.VMEM((N_POOL_GROUPS * (step_tokens // POOL_TILE), POOL_TILE, 2 * LANES), BF16),
            pltpu.VMEM((N_POOL_GROUPS, step_tokens, LANES), BF16),
        ],
        compiler_params=_params("parallel"),
        name="pool",
    )(proj, pmat, w_pool, pool_scale)


def _log_sigmoid(x):
    return jnp.minimum(x, 0.0) - jnp.log(1.0 + jnp.exp(-jnp.abs(x)))


def _split3(x):
    hi = x.astype(BF16)
    r1 = x - hi.astype(F32)
    mid = r1.astype(BF16)
    lo = (r1 - mid.astype(F32)).astype(BF16)
    return lo, mid, hi


def _mlstm_kernel(*refs, seq, chunk, hp, zero_init, emit_state, n_unused, state_slot):
    refs = list(refs)
    q_ref, k_ref, vt_ref, o_ref, g_ref, wq_ref, wk_ref, hng_ref = refs[:8]
    pos = 8
    if not zero_init:
        c0_ref, n0_ref, m0_ref = refs[pos:pos + 3]
        pos += 3
    pos += n_unused
    out_ref = refs[pos]
    pos += 1
    if emit_state:
        c_ref, n_ref, m_ref = refs[pos:pos + 3]
        pos += 3
    qpad, kpad, qs, ks, ht_f, ht_b, rcb_s, br_s, rc_s, ct_s, dt_s, mrow_s, z1_s, rhs_s, z2_s = refs[pos:]
    dh = q_ref.shape[1] // hp
    nc = seq // chunk
    npair = 2 * hp
    aug = dh + BF16_ROWS
    margin = SUBLANES
    unroll = nc <= MLSTM_UNROLL_CHUNKS

    for pad_ref, src_ref in ((qpad, q_ref), (kpad, k_ref)):
        pad_ref[0:margin, :] = jnp.zeros((margin, hp * dh), F32)
        pad_ref[margin + seq:2 * margin + seq, :] = jnp.zeros((margin, hp * dh), F32)
        pad_ref[margin:margin + seq, :] = src_ref[...]

    def conv_silu(pad_ref, w_ref, c, hs):
        w = w_ref[:, hs]
        start = c * chunk if unroll else pl.multiple_of(c * chunk, chunk)
        win = pad_ref[pl.ds(start, chunk + 2 * margin), hs]
        y = (pltpu.roll(win, 1, 0) * w[0:1, :] + win * w[1:2, :]
             + pltpu.roll(win, chunk + 2 * margin - 1, 0) * w[2:3, :])[margin:margin + chunk]
        return y * jax.nn.sigmoid(y)

    ri = lax.broadcasted_iota(jnp.int32, (chunk, chunk), 0)
    ci = lax.broadcasted_iota(jnp.int32, (chunk, chunk), 1)
    lower = ci <= ri
    upper = ci >= ri
    upper_b = jnp.where(upper, 1.0, 0.0).astype(BF16)
    row_is_fwd = lax.broadcasted_iota(jnp.int32, (npair, chunk), 0) < hp
    ones_row = jnp.where(lax.broadcasted_iota(jnp.int32, (BF16_ROWS, chunk), 0) == 0, 1.0, 0.0).astype(BF16)

    gates = g_ref[...]
    lf = _log_sigmoid(gates[:, npair:, :])
    lf3 = _split3(lf.reshape(nc * npair, chunk))
    prefix = (_dot(lf3[0], upper_b) + _dot(lf3[1], upper_b) + _dot(lf3[2], upper_b)).reshape(nc, npair, chunk)
    suffix = prefix[:, :, chunk - 1:chunk] - prefix + lf
    br = jnp.where(row_is_fwd, prefix, suffix)
    br_s[...] = br
    rc_s[...] = gates[:, :npair, :] - br

    def rows_of(c):
        return slice(c * chunk, (c + 1) * chunk) if unroll else pl.ds(pl.multiple_of(c * chunk, chunk), chunk)

    def prep(c, carry):
        sl = rows_of(c)
        rc = rc_s[c]
        for p in range(npair):
            rcb_s[sl, p * LANES:(p + 1) * LANES] = jnp.transpose(jnp.broadcast_to(rc[p:p + 1, :], (LANES, chunk)))
        for hh in range(hp):
            hs = slice(hh * dh, (hh + 1) * dh)
            qs[sl, hs] = conv_silu(qpad, wq_ref, c, hs).astype(BF16)
            ks[sl, hs] = (conv_silu(kpad, wk_ref, c, hs) * (dh ** -0.5)).astype(BF16)
        return carry

    n_rows = lax.broadcasted_iota(jnp.int32, (BF16_ROWS, dh), 0) == 0
    ms = []
    for p in range(npair):
        di, hh = divmod(p, hp)
        if zero_init:
            ct_s[p] = jnp.zeros((aug, dh), F32)
            ms.append(jnp.zeros((1, 1), F32))
        else:
            ct_s[p, 0:dh, :] = jnp.transpose(c0_ref[di, hh])
            ct_s[p, dh:aug, :] = jnp.where(n_rows, n0_ref[di, hh], 0.0)
            ms.append(m0_ref[di, hh])

    def pair_args(p, i):
        di, hh = divmod(p, hp)
        fwd = di == 0
        c = i if fwd else nc - 1 - i
        last = chunk - 1 if fwd else 0
        sl = rows_of(c)
        return fwd, hh, c, last, sl, slice(hh * dh, (hh + 1) * dh)

    def body(i, ms):
        for p in range(npair):
            fwd, hh, c, last, sl, hs = pair_args(p, i)
            rcb = rcb_s[sl, p * LANES:(p + 1) * LANES]
            xt = jnp.where(upper if fwd else lower, rcb, -jnp.inf)
            mrow = jnp.maximum(jnp.max(xt, axis=0, keepdims=True), ms[p])
            dt_s[p] = jnp.exp(xt - mrow)
            mrow_s[p] = jnp.broadcast_to(mrow, (SUBLANES, chunk))
            w = jnp.exp(rcb - mrow[:, last:last + 1])
            rhs_s[p, :, chunk:] = (w * ks[sl, hs].astype(F32)).astype(BF16)
        zeros_q = jnp.zeros((chunk, dh), BF16)
        for pk in range(npair // 2):
            _, _, _, _, sl_a, hs_a = pair_args(2 * pk, i)
            _, _, _, _, sl_b, hs_b = pair_args(2 * pk + 1, i)
            lhs = jnp.concatenate([
                jnp.concatenate([ks[sl_a, hs_a], ks[sl_b, hs_b]], axis=1),
                jnp.concatenate([ct_s[2 * pk].astype(BF16), ct_s[2 * pk + 1].astype(BF16)], axis=1)], axis=0)
            q_diag = jnp.concatenate([
                jnp.concatenate([qs[sl_a, hs_a], zeros_q], axis=1),
                jnp.concatenate([zeros_q, qs[sl_b, hs_b]], axis=1)], axis=0)
            z1_s[pk] = _dot_nt(lhs, q_diag)
        for p in range(npair):
            lanes = slice((p % 2) * chunk, (p % 2 + 1) * chunk)
            rhs_s[p, :, 0:chunk] = (z1_s[p // 2, 0:chunk, lanes] * dt_s[p]).astype(BF16)
        for p in range(npair):
            fwd, hh, c, last, sl, hs = pair_args(p, i)
            z2_s[p] = _dot(jnp.concatenate([vt_ref[c, hs, :], ones_row], axis=0), rhs_s[p])
        new_ms = []
        for p in range(npair):
            fwd, hh, c, last, sl, hs = pair_args(p, i)
            m = ms[p]
            mrow = mrow_s[p, 0:1, :]
            lanes = slice((p % 2) * chunk, (p % 2 + 1) * chunk)
            num = z1_s[p // 2, chunk:, lanes] * jnp.exp(m - mrow) + z2_s[p, :, 0:chunk]
            b_r = br_s[c, p:p + 1, :]
            scale = 1.0 / jnp.maximum(jnp.abs(num[dh:dh + 1, :]), jnp.exp(-(b_r + mrow)))
            h_t = num[0:dh, :] * scale
            if fwd:
                ht_f[hh * nc + c] = h_t
            else:
                ht_b[hh * nc + c] = h_t
            mlast = mrow[:, last:last + 1]
            ct_s[p] = jnp.exp(m - mlast) * ct_s[p] + z2_s[p, :, chunk:]
            new_ms.append(b_r[:, last:last + 1] + mlast)
        return tuple(new_ms)

    def finish(c, carry):
        sl = rows_of(c)
        for hh in range(hp):
            hs = slice(hh * dh, (hh + 1) * dh)
            h_t = ht_f[hh * nc + c] + ht_b[hh * nc + c]
            mu = jnp.mean(h_t, axis=0, keepdims=True)
            hc = h_t - mu
            var = jnp.mean(hc * hc, axis=0, keepdims=True)
            hn = jnp.transpose(hc * lax.rsqrt(var + LN_EPS)) * hng_ref[:, hs]
            out_ref[sl, hs] = (jax.nn.sigmoid(o_ref[sl, hs]) * hn).astype(BF16)
        return carry

    if unroll:
        for c in range(nc):
            prep(c, 0)
        ms = tuple(ms)
        for i in range(nc):
            ms = body(i, ms)
        for c in range(nc):
            finish(c, 0)
    else:
        half = nc // 2
        prep(0, 0)
        prep(nc - 1, 0)

        def first_half(i, ms):
            ms = body(i, ms)
            prep(i + 1, 0)
            prep(nc - 2 - i, 0)
            return ms

        def second_half(i, ms):
            ms = body(i, ms)
            finish(i, 0)
            finish(nc - 1 - i, 0)
            return ms

        ms = lax.fori_loop(0, half, first_half, tuple(ms), unroll=True)
        ms = lax.fori_loop(half, nc, second_half, ms, unroll=True)

    if emit_state:
        for slot in range(c_ref.shape[0]):
            if slot != state_slot:
                c_ref[slot] = jnp.zeros(c_ref.shape[1:], F32)
                n_ref[slot] = jnp.zeros(n_ref.shape[1:], F32)
                m_ref[slot] = jnp.zeros(m_ref.shape[1:], F32)
        for p in range(npair):
            di, hh = divmod(p, hp)
            c_ref[state_slot, di, hh] = jnp.transpose(ct_s[p, 0:dh, :])
            n_ref[state_slot, di, hh] = ct_s[p, dh:dh + 1, :]
            m_ref[state_slot, di, hh] = ms[p]


def _mlstm_heads_per_step(seq, dh):
    return max(1, min(MLSTM_HEADS, MLSTM_STEP_BYTES // (seq * dh * 4)))


def _gate_row_order(hp):
    heads = MLSTM_HEADS
    order = []
    for g in range(heads // hp):
        for kind in range(2):
            for di in range(2):
                for hh in range(hp):
                    order.append((di * 2 + kind) * heads + g * hp + hh)
    return np.asarray(order)


def _mlstm(proj, v_t, gates_t, wl, w_qk_conv, hn_g, batch, seq, col0, dh, init_state=None, layer=None, depth=None,
           prev=None):
    heads = MLSTM_HEADS
    chunk = MLSTM_CHUNK
    assert chunk == LANES and seq % chunk == 0 and (seq // chunk <= MLSTM_UNROLL_CHUNKS or seq // chunk % 2 == 0)
    nc = seq // chunk
    hp = _mlstm_heads_per_step(seq, dh)
    hg = heads // hp
    npair = 2 * hp
    aug = dh + BF16_ROWS
    width = hp * dh
    cb = col0 // width
    gb = heads * dh // width
    zero_init = init_state is None
    emit_state = layer is not None
    prev = () if prev is None else tuple(prev)
    slots, first_slot = (1, layer) if prev else (depth, 0)
    kern = functools.partial(_mlstm_kernel, seq=seq, chunk=chunk, hp=hp, zero_init=zero_init, emit_state=emit_state,
                             n_unused=len(prev), state_slot=(layer - first_slot) if emit_state else None)

    def tok_spec(group):
        return pl.BlockSpec((seq, width), lambda b, g: (b, cb + group * gb + g))

    in_specs = [
        tok_spec(0), tok_spec(1), pl.BlockSpec((nc, width, chunk), lambda b, g: (b, g, 0)), tok_spec(2),
        pl.BlockSpec((nc, 2 * npair, chunk), lambda b, g: (b, g, 0)),
        pl.BlockSpec((None, 3, width), lambda b, g: (wl, 0, g)),
        pl.BlockSpec((None, 3, width), lambda b, g: (wl, 0, gb + g)),
        pl.BlockSpec((None, 1, width), lambda b, g: (wl, 0, g)),
    ]
    args = [proj, proj, v_t, proj, gates_t, w_qk_conv, w_qk_conv, hn_g]
    if not zero_init:
        in_specs += [pl.BlockSpec((None, None, 2, hp, dh, dh), lambda b, g: (b, wl, 0, g, 0, 0)),
                     pl.BlockSpec((None, None, 2, hp, 1, dh), lambda b, g: (b, wl, 0, g, 0, 0)),
                     pl.BlockSpec((None, None, 2, hp, 1, 1), lambda b, g: (b, wl, 0, g, 0, 0))]
        args += list(init_state)
    aliases = {}
    for k, arr in enumerate(prev):
        aliases[len(args)] = 1 + k
        in_specs.append(pl.BlockSpec(memory_space=pl.ANY))
        args.append(arr)
    out_shape = [jax.ShapeDtypeStruct((batch * seq, heads * dh), BF16)]
    out_specs = [pl.BlockSpec((seq, width), lambda b, g: (b, g))]
    if emit_state:
        out_shape += [jax.ShapeDtypeStruct((batch, depth, 2, heads, dh, dh), F32),
                      jax.ShapeDtypeStruct((batch, depth, 2, heads, 1, dh), F32),
                      jax.ShapeDtypeStruct((batch, depth, 2, heads, 1, 1), F32)]
        out_specs += [pl.BlockSpec((None, slots, 2, hp, dh, dh), lambda b, g: (b, first_slot, 0, g, 0, 0)),
                      pl.BlockSpec((None, slots, 2, hp, 1, dh), lambda b, g: (b, first_slot, 0, g, 0, 0)),
                      pl.BlockSpec((None, slots, 2, hp, 1, 1), lambda b, g: (b, first_slot, 0, g, 0, 0))]
    res = pl.pallas_call(
        kern,
        out_shape=tuple(out_shape),
        grid=(batch, hg),
        in_specs=in_specs,
        out_specs=tuple(out_specs),
        scratch_shapes=[
            pltpu.VMEM((seq + 2 * SUBLANES, width), F32), pltpu.VMEM((seq + 2 * SUBLANES, width), F32),
            pltpu.VMEM((seq, width), BF16), pltpu.VMEM((seq, width), BF16),
            pltpu.VMEM((hp * nc, dh, chunk), F32), pltpu.VMEM((hp * nc, dh, chunk), F32),
            pltpu.VMEM((seq, npair * LANES), F32), pltpu.VMEM((nc, npair, chunk), F32),
            pltpu.VMEM((nc, npair, chunk), F32),
            pltpu.VMEM((npair, aug, dh), F32),
            pltpu.VMEM((npair, chunk, chunk), F32), pltpu.VMEM((npair, SUBLANES, chunk), F32),
            pltpu.VMEM((npair // 2, chunk + aug, 2 * chunk), F32), pltpu.VMEM((npair, chunk, chunk + dh), BF16),
            pltpu.VMEM((npair, aug, chunk + dh), F32),
        ],
        input_output_aliases=aliases,
        compiler_params=_params("parallel", "parallel"),
        name="mlstm",
    )(*args)
    return res[0], tuple(res[1:])


def _outproj_kernel(pool_ref, ml_ref, x_ref, mod_ref, wp_ref, wm_ref, g_ref, b_ref, x1_ref, h2_ref, *, alpha):
    sub = TOKEN_SUB
    for s in range(x_ref.shape[0] // sub):
        rs = slice(s * sub, (s + 1) * sub)
        mix = _dot(pool_ref[rs, :], wp_ref[...]) + _dot(ml_ref[rs, :], wm_ref[...])
        x1 = _layer_norm(alpha * x_ref[rs, :] + mod_ref[2:3, :] * mix) * g_ref[...] + b_ref[...]
        x1_ref[rs, :] = x1
        h2 = _layer_norm(x1) * (1.0 + mod_ref[4:5, :]) + mod_ref[3:4, :]
        h2_ref[rs, :] = h2.astype(BF16)


def _outproj(pool_out, ml_out, x, mod, cond_row, layer, w_out, ln_g, ln_b, alpha):
    tokens, d = x.shape
    half = pool_out.shape[1]
    tm = TOKEN_TILE
    kern = functools.partial(_outproj_kernel, alpha=alpha)
    return pl.pallas_call(
        kern,
        out_shape=(jax.ShapeDtypeStruct((tokens, d), F32), jax.ShapeDtypeStruct((tokens, d), BF16)),
        grid=(tokens // tm,),
        in_specs=[
            pl.BlockSpec((tm, half), lambda i: (i, 0)),
            pl.BlockSpec((tm, half), lambda i: (i, 0)),
            pl.BlockSpec((tm, d), lambda i: (i, 0)),
            pl.BlockSpec((None, None, 6, d), lambda i: (layer, cond_row(i * tm), 0, 0)),
            pl.BlockSpec((None, half, d), lambda i: (layer, 0, 0)),
            pl.BlockSpec((None, half, d), lambda i: (layer, 1, 0)),
            pl.BlockSpec((None, 1, d), lambda i: (layer, 0, 0)),
            pl.BlockSpec((None, 1, d), lambda i: (layer, 0, 0)),
        ],
        out_specs=(pl.BlockSpec((tm, d), lambda i: (i, 0)), pl.BlockSpec((tm, d), lambda i: (i, 0))),
        compiler_params=_params("parallel"),
        name="outproj",
    )(pool_out, ml_out, x, mod, w_out, w_out, ln_g, ln_b)


def _gelu_tanh(x):
    return 0.5 * x * (1.0 + jnp.tanh(math.sqrt(2.0 / math.pi) * (x + 0.044715 * (x * x * x))))


def _ffn_kernel(h_ref, x1_ref, mod_ref, wa_ref, wg_ref, wdw_ref, bdw_ref, wd_ref, lng_ref, lnb_ref, out_ref,
                gpad, a_s, u_s, wa_b, wg_b, wd_b, *, rows, cols, tokens, alpha):
    j = pl.program_id(1)
    tf = wa_ref.shape[1]
    blk = FFN_BLOCK
    nb = tokens // blk
    base = (cols if rows > 1 else 0) + SUBLANES

    @pl.when(j == 0)
    def _():
        out_ref[...] = jnp.zeros_like(out_ref)

    wa_b[...] = wa_ref[...].astype(BF16)
    wg_b[...] = wg_ref[...].astype(BF16)
    wd_b[...] = wd_ref[...].astype(BF16)
    gpad[0:base, :] = jnp.zeros((base, tf), F32)
    gpad[base + tokens:base + tokens + base, :] = jnp.zeros((base, tf), F32)

    wdw = wdw_ref[...]
    bdw = bdw_ref[...]
    sub = max(cols, FFN_CONV_ROWS)
    ci = lax.broadcasted_iota(jnp.int32, (sub, 1), 0) & (cols - 1)
    first = ci == 0
    last = ci == cols - 1
    taps_y = (-1, 0, 1) if rows > 1 else (0,)

    def up(b):
        hb = h_ref[b * blk:(b + 1) * blk, :]
        a_s[b * blk:(b + 1) * blk, :] = _dot(hb, wa_b[...])
        gpad[base + b * blk:base + (b + 1) * blk, :] = _dot(hb, wg_b[...])

    def conv_gate(b):
        for r in range(b * blk // sub, (b + 1) * blk // sub):
            off = r * sub
            for lb in range(tf // LANES):
                ls = slice(lb * LANES, (lb + 1) * LANES)
                left = mid = right = None
                for dy in taps_y:
                    r0 = base + off + dy * cols
                    kk = (dy + 1) * 3
                    tl = gpad[r0 - 1:r0 - 1 + sub, ls] * wdw[kk:kk + 1, ls]
                    tc = gpad[r0:r0 + sub, ls] * wdw[kk + 1:kk + 2, ls]
                    tr = gpad[r0 + 1:r0 + 1 + sub, ls] * wdw[kk + 2:kk + 3, ls]
                    left = tl if left is None else left + tl
                    mid = tc if mid is None else mid + tc
                    right = tr if right is None else right + tr
                y = mid + jnp.where(first, 0.0, left) + jnp.where(last, 0.0, right) + bdw[:, ls]
                u_s[off:off + sub, ls] = (_gelu_tanh(y) * a_s[off:off + sub, ls]).astype(BF16)

    def down(b):
        out_ref[b * blk:(b + 1) * blk, :] += _dot(u_s[b * blk:(b + 1) * blk, :], wd_b[...])

    for step in range(nb + 2):
        if step < nb:
            up(step)
        if 1 <= step <= nb:
            conv_gate(step - 1)
        if step >= 2:
            down(step - 2)

    @pl.when(j == pl.num_programs(1) - 1)
    def _():
        gate = mod_ref[5:6, :]
        lng = lng_ref[...]
        lnb = lnb_ref[...]

        def ln_body(r, carry):
            sl = pl.ds(pl.multiple_of(r * LN_ROWS, LN_ROWS), LN_ROWS)
            z = alpha * x1_ref[sl, :] + gate * out_ref[sl, :]
            out_ref[sl, :] = _layer_norm(z) * lng + lnb
            return carry

        lax.fori_loop(0, tokens // LN_ROWS, ln_body, 0)


def _ffn(h2, x1, mod, cond_row, layer, w_up, w_dw, b_dw, w_down, ln_g, ln_b, rows, cols, alpha):
    total, d = x1.shape
    d_ff = w_down.shape[1]
    tm = FFN_TOKENS
    tf = FFN_COLS
    nj = d_ff // tf
    kern = functools.partial(_ffn_kernel, rows=rows, cols=cols, tokens=tm, alpha=alpha)
    return pl.pallas_call(
        kern,
        out_shape=jax.ShapeDtypeStruct((total, d), F32),
        grid=(total // tm, nj),
        in_specs=[
            pl.BlockSpec((tm, d), lambda i, j: (i, 0)),
            pl.BlockSpec((tm, d), lambda i, j: (i, 0)),
            pl.BlockSpec((None, None, 6, d), lambda i, j: (layer, cond_row(i * tm), 0, 0)),
            pl.BlockSpec((None, d, tf), lambda i, j: (layer, 0, j)),
            pl.BlockSpec((None, d, tf), lambda i, j: (layer, 0, nj + j)),
            pl.BlockSpec((None, 9, tf), lambda i, j: (layer, 0, j)),
            pl.BlockSpec((None, 1, tf), lambda i, j: (layer, 0, j)),
            pl.BlockSpec((None, tf, d), lambda i, j: (layer, j, 0)),
            pl.BlockSpec((None, 1, d), lambda i, j: (layer, 0, 0)),
            pl.BlockSpec((None, 1, d), lambda i, j: (layer, 0, 0)),
        ],
        out_specs=pl.BlockSpec((tm, d), lambda i, j: (i, 0)),
        scratch_shapes=[
            pltpu.VMEM((tm + 2 * ((cols if rows > 1 else 0) + SUBLANES), tf), F32),
            pltpu.VMEM((tm, tf), F32),
            pltpu.VMEM((tm, tf), BF16),
            pltpu.VMEM((d, tf), BF16), pltpu.VMEM((d, tf), BF16), pltpu.VMEM((tf, d), BF16),
        ],
        compiler_params=_params("parallel", "arbitrary"),
        name="ffn",
    )(h2, x1, mod, w_up, w_up, w_dw, b_dw, w_down, ln_g, ln_b)


def _trunk_layer(x, batch, rows, cols, cond_row, layer, p, pmat, alpha, dh, state_kw):
    seq = rows * cols
    hp = _mlstm_heads_per_step(seq, dh)
    mod = p["ada"]
    proj, v_t, gates_t = _inproj(x, mod, cond_row, layer, p["w_in_main"], p["w_v_t"], p["w_gate_t"][hp],
                                 p["b_gate"][hp])
    pool_out = _pool(proj, pmat, layer, p["w_pool"], p["pool_scale"], rows, cols, min(FFN_TOKENS, batch * seq))
    ml_out, states = _mlstm(proj, v_t, gates_t, layer, p["w_qk_conv"], p["hn_g"], batch, seq, N_POOL_GROUPS * LANES,
                            dh, **state_kw)
    x1, h2 = _outproj(pool_out, ml_out, x, mod, cond_row, layer, p["w_out"], p["ln1_g"], p["ln1_b"], alpha)
    x2 = _ffn(h2, x1, mod, cond_row, layer, p["w_up"], p["w_dw"], p["b_dw"], p["w_down"], p["ln2_g"], p["ln2_b"],
              rows, cols, alpha)
    return x2, states


def kernel(x_prompt, x_sample, c, state_C, state_n, state_m, c_ctx, w_ada, b_ada, w_in, b_gates, w_qk_conv, hn_g,
           w_pool, pool_scale, w_out, ln1_g, ln1_b, w_up, w_dw, b_dw, w_down, ln2_g, ln2_b):
    b_p, l_p, d = x_prompt.shape
    b_s, l_s, _ = x_sample.shape
    depth = w_ada.shape[0]
    heads = MLSTM_HEADS
    dh = state_C.shape[-1]
    pool_w = N_POOL_GROUPS * LANES
    mw = heads * dh
    n_main = pool_w + 4 * mw
    v0 = pool_w + 2 * mw
    d_ff = w_down.shape[1]
    alpha = (2.0 * depth) ** 0.25
    rows_s = l_s // GRID_W
    assert b_s + 1 <= COND_ROWS and l_p % POOL_TILE == 0 and POOL_TILE % GRID_W == 0

    cond = jnp.concatenate([c_ctx[None, :], c, jnp.zeros((COND_ROWS - 1 - b_s, d), F32)], axis=0)

    pmat_p = _pool_col_matrices(l_p)
    pmat_s = _pool_col_matrices(GRID_W)
    assert min(FFN_TOKENS, b_s * l_s) == l_s
    hps = sorted({_mlstm_heads_per_step(l_p, dh), _mlstm_heads_per_step(l_s, dh)})

    w_gate = w_in[:, :, n_main:]
    p = dict(
        ada=_ada(cond, w_ada, b_ada).reshape(depth, COND_ROWS, 6, d),
        w_in_main=jnp.concatenate([w_in[:, :, :v0], w_in[:, :, v0 + mw:n_main]], axis=2).astype(BF16),
        w_v_t=w_in[:, :, v0:v0 + mw].transpose(0, 2, 1).astype(BF16),
        w_gate_t={hp: w_gate[:, :, _gate_row_order(hp)].transpose(0, 2, 1).astype(BF16) for hp in hps},
        b_gate={hp: b_gates[:, _gate_row_order(hp)][:, :, None] for hp in hps},
        w_qk_conv=w_qk_conv, hn_g=hn_g[:, None, :],
        w_pool=w_pool.astype(BF16), pool_scale=pool_scale[:, None, :],
        w_out=w_out.astype(BF16), ln1_g=ln1_g[:, None, :], ln1_b=ln1_b[:, None, :],
        w_up=w_up, w_dw=w_dw.reshape(depth, 9, d_ff), b_dw=b_dw[:, None, :],
        w_down=w_down, ln2_g=ln2_g[:, None, :], ln2_b=ln2_b[:, None, :])
    init = (state_C, state_n.reshape(b_s, depth, 2, heads, 1, dh), state_m.reshape(b_s, depth, 2, heads, 1, 1))

    y_p = x_prompt.reshape(b_p * l_p, d)
    y_s = x_sample.reshape(b_s * l_s, d)
    states = None
    for l in range(depth):
        y_p, states = _trunk_layer(y_p, b_p, 1, l_p, lambda t: 0, l, p, pmat_p, alpha, dh,
                                   dict(layer=l, depth=depth, prev=states))
        y_s, _ = _trunk_layer(y_s, b_s, rows_s, GRID_W, lambda t: 1 + t // l_s, l, p, pmat_s, alpha, dh,
                              dict(init_state=init))
    new_c, new_n, new_m = states
    return (y_p.reshape(b_p, l_p, d), y_s.reshape(b_s, l_s, d), new_c, new_n.reshape(b_p, depth, 2, heads, dh),
            new_m.reshape(b_p, depth, 2, heads))
```

```python
import functools
import math

import numpy as np
import jax
import jax.numpy as jnp
from jax import lax
from jax.experimental import pallas as pl
from jax.experimental.pallas import tpu as pltpu

F32 = jnp.float32
BF16 = jnp.bfloat16

GRID_W = 64
POOL_WINDOWS = (2, 4, 8, 16)
N_POOL_GROUPS = len(POOL_WINDOWS)
MLSTM_HEADS = 4
LN_EPS = 1e-5

LANES = 128
SUBLANES = 8
BF16_ROWS = 16
VMEM_LIMIT_BYTES = 56 * 1024 * 1024

COND_ROWS = SUBLANES
ADA_COLS = 3072
MLSTM_CHUNK = LANES
MLSTM_UNROLL_CHUNKS = 2
MLSTM_STEP_BYTES = 2 * 1024 * 1024
POOL_TILE = 256
TOKEN_TILE = 1024
TOKEN_SUB = 256
FFN_TOKENS = 2048
FFN_BLOCK = 512
FFN_COLS = 256
FFN_CONV_ROWS = 256
LN_ROWS = 256


def _params(*sem):
    return pltpu.CompilerParams(dimension_semantics=sem, vmem_limit_bytes=VMEM_LIMIT_BYTES)


def _layer_norm(x):
    mu = jnp.mean(x, axis=-1, keepdims=True)
    xc = x - mu
    var = jnp.mean(xc * xc, axis=-1, keepdims=True)
    return xc * lax.rsqrt(var + LN_EPS)


def _dot(a, b):
    return jnp.dot(a, b, preferred_element_type=F32)


def _dot_nt(a, b):
    return lax.dot_general(a, b, (((1,), (1,)), ((), ())), preferred_element_type=F32)


def _ada_kernel(cond_ref, w_ref, b_ref, out_ref):
    cnd = cond_ref[...]
    act = (cnd * jax.nn.sigmoid(cnd)).astype(BF16)
    out_ref[...] = _dot(act, w_ref[...].astype(BF16)) + b_ref[...]


def _ada(cond, w_ada, b_ada):
    depth, d, n = w_ada.shape
    tn = ADA_COLS
    return pl.pallas_call(
        _ada_kernel,
        out_shape=jax.ShapeDtypeStruct((depth, COND_ROWS, n), F32),
        grid=(depth, n // tn),
        in_specs=[
            pl.BlockSpec((COND_ROWS, d), lambda l, j: (0, 0)),
            pl.BlockSpec((None, d, tn), lambda l, j: (l, 0, j)),
            pl.BlockSpec((None, 1, tn), lambda l, j: (l, 0, j)),
        ],
        out_specs=pl.BlockSpec((None, COND_ROWS, tn), lambda l, j: (l, 0, j)),
        compiler_params=_params("parallel", "parallel"),
        name="ada",
    )(cond, w_ada, b_ada.reshape(depth, 1, n))


def _inproj_kernel(x_ref, mod_ref, w_ref, wvt_ref, wgt_ref, bg_ref, proj_ref, vt_ref, gates_ref):
    sub = TOKEN_SUB
    for s in range(x_ref.shape[0] // sub):
        rs = slice(s * sub, (s + 1) * sub)
        h = _layer_norm(x_ref[rs, :]) * (1.0 + mod_ref[1:2, :]) + mod_ref[0:1, :]
        h = h.astype(BF16)
        proj_ref[rs, :] = _dot(h, w_ref[...])
        v_t = _dot_nt(wvt_ref[...], h).astype(BF16)
        gates_t = _dot_nt(wgt_ref[...], h) + bg_ref[...]
        for k in range(sub // LANES):
            vt_ref[s * (sub // LANES) + k] = v_t[:, k * LANES:(k + 1) * LANES]
            gates_ref[s * (sub // LANES) + k] = gates_t[:, k * LANES:(k + 1) * LANES]


def _inproj(x, mod, cond_row, layer, w_main, w_v_t, w_gate_t, b_gate):
    tokens, d = x.shape
    n = w_main.shape[2]
    nv = w_v_t.shape[1]
    ng = w_gate_t.shape[1]
    tm = TOKEN_TILE
    return pl.pallas_call(
        _inproj_kernel,
        out_shape=(jax.ShapeDtypeStruct((tokens, n), F32), jax.ShapeDtypeStruct((tokens // LANES, nv, LANES), BF16),
                   jax.ShapeDtypeStruct((tokens // LANES, ng, LANES), F32)),
        grid=(tokens // tm,),
        in_specs=[
            pl.BlockSpec((tm, d), lambda i: (i, 0)),
            pl.BlockSpec((None, None, 6, d), lambda i: (layer, cond_row(i * tm), 0, 0)),
            pl.BlockSpec((None, d, n), lambda i: (layer, 0, 0)),
            pl.BlockSpec((None, nv, d), lambda i: (layer, 0, 0)),
            pl.BlockSpec((None, ng, d), lambda i: (layer, 0, 0)),
            pl.BlockSpec((None, ng, 1), lambda i: (layer, 0, 0)),
        ],
        out_specs=(pl.BlockSpec((tm, n), lambda i: (i, 0)),
                   pl.BlockSpec((tm // LANES, nv, LANES), lambda i: (i, 0, 0)),
                   pl.BlockSpec((tm // LANES, ng, LANES), lambda i: (i, 0, 0))),
        compiler_params=_params("parallel"),
        name="inproj",
    )(x, mod, w_main, w_v_t, w_gate_t, b_gate)


def _pool_col_matrices(cols):
    t = np.arange(POOL_TILE)
    row, col = t // cols, t % cols
    mats = []
    for win in POOL_WINDOWS:
        hw = win // 2
        lo = np.maximum(col - hw, 0)
        hi = np.minimum(col + hw, cols)
        same_row = row[:, None] == row[None, :]
        inside = (col[None, :] >= lo[:, None]) & (col[None, :] < hi[:, None])
        mats.append((same_row & inside).astype(np.float32))
    return jnp.asarray(np.stack(mats), dtype=BF16)


def _pool_kernel(xp_ref, pmat_ref, wp_ref, ps_ref, out_ref, pad_ref, hl_s, y_s, *, rows, cols, tokens):
    gdim = LANES
    tile = POOL_TILE
    halo = (max(POOL_WINDOWS) // 2) * cols if rows > 1 else 0
    t = lax.broadcasted_iota(jnp.int32, (tile, 1), 0)
    col = t & (cols - 1)
    if rows > 1:
        zeros = jnp.zeros((halo, gdim), F32)
        pad_ref[0:halo, :] = zeros
        pad_ref[halo + tokens:halo + tokens + halo, :] = zeros
    for gi, win in enumerate(POOL_WINDOWS):
        hw = win // 2
        gs = slice(gi * gdim, (gi + 1) * gdim)
        if rows > 1:
            pad_ref[halo:halo + tokens, :] = xp_ref[:, gs]
        cnt_c = (jnp.minimum(col + hw, cols) - jnp.maximum(col - hw, 0)).astype(F32)
        pm = pmat_ref[gi]
        wp = wp_ref[gi]
        scale = ps_ref[:, gs]

        nt = tokens // tile
        for tt in range(nt):
            off = tt * tile
            if rows > 1:
                acc = None
                for dr in range(-hw, hw):
                    term = pad_ref[off + halo + dr * cols:off + halo + dr * cols + tile, :]
                    acc = term if acc is None else acc + term
                rw = (off + t) >> int(math.log2(cols))
                cnt_r = (jnp.minimum(rw + hw, rows) - jnp.maximum(rw - hw, 0)).astype(F32)
                m1 = acc / cnt_r
            else:
                m1 = xp_ref[off:off + tile, gs]
            hi = m1.astype(BF16)
            hl_s[gi * nt + tt] = jnp.concatenate([hi, (m1 - hi.astype(F32)).astype(BF16)], axis=1)
        for tt in range(nt):
            off = tt * tile
            both = _dot(pm, hl_s[gi * nt + tt])
            m2 = (both[:, :gdim] + both[:, gdim:]) / cnt_c
            y_s[gi, off:off + tile, :] = (m2 - xp_ref[off:off + tile, gs]).astype(BF16)
        out_ref[:, gs] = (_dot(y_s[gi], wp) * scale).astype(BF16)


def _pool(proj, pmat, layer, w_pool, pool_scale, rows, cols, step_tokens):
    total = proj.shape[0]
    pool_w = N_POOL_GROUPS * LANES
    halo = (max(POOL_WINDOWS) // 2) * cols if rows > 1 else 0
    kern = functools.partial(_pool_kernel, rows=rows, cols=cols, tokens=step_tokens)
    return pl.pallas_call(
        kern,
        out_shape=jax.ShapeDtypeStruct((total, pool_w), BF16),
        grid=(total // step_tokens,),
        in_specs=[
            pl.BlockSpec((step_tokens, pool_w), lambda i: (i, 0)),
            pl.BlockSpec(pmat.shape, lambda i: (0, 0, 0)),
            pl.BlockSpec((None,) + w_pool.shape[1:], lambda i: (layer, 0, 0, 0)),
            pl.BlockSpec((None, 1, pool_w), lambda i: (layer, 0, 0)),
        ],
        out_specs=pl.BlockSpec((step_tokens, pool_w), lambda i: (i, 0)),
        scratch_shapes=[
            pltpu.VMEM((step_tokens + 2 * halo if rows > 1 else SUBLANES, LANES), F32),
            pltpu.VMEM((N_POOL_GROUPS * (step_tokens // POOL_TILE), POOL_TILE, 2 * LANES), BF16),
            pltpu.VMEM((N_POOL_GROUPS, step_tokens, LANES), BF16),
        ],
        compiler_params=_params("parallel"),
        name="pool",
    )(proj, pmat, w_pool, pool_scale)


def _log_sigmoid(x):
    return jnp.minimum(x, 0.0) - jnp.log(1.0 + jnp.exp(-jnp.abs(x)))


def _split3(x):
    hi = x.astype(BF16)
    r1 = x - hi.astype(F32)
    mid = r1.astype(BF16)
    lo = (r1 - mid.astype(F32)).astype(BF16)
    return lo, mid, hi


def _mlstm_kernel(*refs, seq, chunk, hp, zero_init, emit_state, n_unused, state_slot):
    refs = list(refs)
    q_ref, k_ref, vt_ref, o_ref, g_ref, wq_ref, wk_ref, hng_ref = refs[:8]
    pos = 8
    if not zero_init:
        c0_ref, n0_ref, m0_ref = refs[pos:pos + 3]
        pos += 3
    pos += n_unused
    out_ref = refs[pos]
    pos += 1
    if emit_state:
        c_ref, n_ref, m_ref = refs[pos:pos + 3]
        pos += 3
    qpad, kpad, qs, ks, ht_f, ht_b, rcb_s, br_s, rc_s, ct_s, dt_s, mrow_s, z1_s, rhs_s, z2_s = refs[pos:]
    dh = q_ref.shape[1] // hp
    nc = seq // chunk
    npair = 2 * hp
    aug = dh + BF16_ROWS
    margin = SUBLANES
    unroll = nc <= MLSTM_UNROLL_CHUNKS

    for pad_ref, src_ref in ((qpad, q_ref), (kpad, k_ref)):
        pad_ref[0:margin, :] = jnp.zeros((margin, hp * dh), F32)
        pad_ref[margin + seq:2 * margin + seq, :] = jnp.zeros((margin, hp * dh), F32)
        pad_ref[margin:margin + seq, :] = src_ref[...]

    def conv_silu(pad_ref, w_ref, c, hs):
        w = w_ref[:, hs]
        start = c * chunk if unroll else pl.multiple_of(c * chunk, chunk)
        win = pad_ref[pl.ds(start, chunk + 2 * margin), hs]
        y = (pltpu.roll(win, 1, 0) * w[0:1, :] + win * w[1:2, :]
             + pltpu.roll(win, chunk + 2 * margin - 1, 0) * w[2:3, :])[margin:margin + chunk]
        return y * jax.nn.sigmoid(y)

    ri = lax.broadcasted_iota(jnp.int32, (chunk, chunk), 0)
    ci = lax.broadcasted_iota(jnp.int32, (chunk, chunk), 1)
    lower = ci <= ri
    upper = ci >= ri
    upper_b = jnp.where(upper, 1.0, 0.0).astype(BF16)
    row_is_fwd = lax.broadcasted_iota(jnp.int32, (npair, chunk), 0) < hp
    ones_row = jnp.where(lax.broadcasted_iota(jnp.int32, (BF16_ROWS, chunk), 0) == 0, 1.0, 0.0).astype(BF16)

    gates = g_ref[...]
    lf = _log_sigmoid(gates[:, npair:, :])
    lf3 = _split3(lf.reshape(nc * npair, chunk))
    prefix = (_dot(lf3[0], upper_b) + _dot(lf3[1], upper_b) + _dot(lf3[2], upper_b)).reshape(nc, npair, chunk)
    suffix = prefix[:, :, chunk - 1:chunk] - prefix + lf
    br = jnp.where(row_is_fwd, prefix, suffix)
    br_s[...] = br
    rc_s[...] = gates[:, :npair, :] - br

    def rows_of(c):
        return slice(c * chunk, (c + 1) * chunk) if unroll else pl.ds(pl.multiple_of(c * chunk, chunk), chunk)

    def prep(c, carry):
        sl = rows_of(c)
        rc = rc_s[c]
        for p in range(npair):
            rcb_s[sl, p * LANES:(p + 1) * LANES] = jnp.transpose(jnp.broadcast_to(rc[p:p + 1, :], (LANES, chunk)))
        for hh in range(hp):
            hs = slice(hh * dh, (hh + 1) * dh)
            qs[sl, hs] = conv_silu(qpad, wq_ref, c, hs).astype(BF16)
            ks[sl, hs] = (conv_silu(kpad, wk_ref, c, hs) * (dh ** -0.5)).astype(BF16)
        return carry

    n_rows = lax.broadcasted_iota(jnp.int32, (BF16_ROWS, dh), 0) == 0
    ms = []
    for p in range(npair):
        di, hh = divmod(p, hp)
        if zero_init:
            ct_s[p] = jnp.zeros((aug, dh), F32)
            ms.append(jnp.zeros((1, 1), F32))
        else:
            ct_s[p, 0:dh, :] = jnp.transpose(c0_ref[di, hh])
            ct_s[p, dh:aug, :] = jnp.where(n_rows, n0_ref[di, hh], 0.0)
            ms.append(m0_ref[di, hh])

    def pair_args(p, i):
        di, hh = divmod(p, hp)
        fwd = di == 0
        c = i if fwd else nc - 1 - i
        last = chunk - 1 if fwd else 0
        sl = rows_of(c)
        return fwd, hh, c, last, sl, slice(hh * dh, (hh + 1) * dh)

    def body(i, ms):
        for p in range(npair):
            fwd, hh, c, last, sl, hs = pair_args(p, i)
            rcb = rcb_s[sl, p * LANES:(p + 1) * LANES]
            xt = jnp.where(upper if fwd else lower, rcb, -jnp.inf)
            mrow = jnp.maximum(jnp.max(xt, axis=0, keepdims=True), ms[p])
            dt_s[p] = jnp.exp(xt - mrow)
            mrow_s[p] = jnp.broadcast_to(mrow, (SUBLANES, chunk))
            w = jnp.exp(rcb - mrow[:, last:last + 1])
            rhs_s[p, :, chunk:] = (w * ks[sl, hs].astype(F32)).astype(BF16)
        zeros_q = jnp.zeros((chunk, dh), BF16)
        for pk in range(npair // 2):
            _, _, _, _, sl_a, hs_a = pair_args(2 * pk, i)
            _, _, _, _, sl_b, hs_b = pair_args(2 * pk + 1, i)
            lhs = jnp.concatenate([
                jnp.concatenate([ks[sl_a, hs_a], ks[sl_b, hs_b]], axis=1),
                jnp.concatenate([ct_s[2 * pk].astype(BF16), ct_s[2 * pk + 1].astype(BF16)], axis=1)], axis=0)
            q_diag = jnp.concatenate([
                jnp.concatenate([qs[sl_a, hs_a], zeros_q], axis=1),
                jnp.concatenate([zeros_q, qs[sl_b, hs_b]], axis=1)], axis=0)
            z1_s[pk] = _dot_nt(lhs, q_diag)
        for p in range(npair):
            lanes = slice((p % 2) * chunk, (p % 2 + 1) * chunk)
            rhs_s[p, :, 0:chunk] = (z1_s[p // 2, 0:chunk, lanes] * dt_s[p]).astype(BF16)
        for p in range(npair):
            fwd, hh, c, last, sl, hs = pair_args(p, i)
            z2_s[p] = _dot(jnp.concatenate([vt_ref[c, hs, :], ones_row], axis=0), rhs_s[p])
        new_ms = []
        for p in range(npair):
            fwd, hh, c, last, sl, hs = pair_args(p, i)
            m = ms[p]
            mrow = mrow_s[p, 0:1, :]
            lanes = slice((p % 2) * chunk, (p % 2 + 1) * chunk)
            num = z1_s[p // 2, chunk:, lanes] * jnp.exp(m - mrow) + z2_s[p, :, 0:chunk]
            b_r = br_s[c, p:p + 1, :]
            scale = 1.0 / jnp.maximum(jnp.abs(num[dh:dh + 1, :]), jnp.exp(-(b_r + mrow)))
            h_t = num[0:dh, :] * scale
            if fwd:
                ht_f[hh * nc + c] = h_t
            else:
                ht_b[hh * nc + c] = h_t
            mlast = mrow[:, last:last + 1]
            ct_s[p] = jnp.exp(m - mlast) * ct_s[p] + z2_s[p, :, chunk:]
            new_ms.append(b_r[:, last:last + 1] + mlast)
        return tuple(new_ms)

    def finish(c, carry):
        sl = rows_of(c)
        for hh in range(hp):
            hs = slice(hh * dh, (hh + 1) * dh)
            h_t = ht_f[hh * nc + c] + ht_b[hh * nc + c]
            mu = jnp.mean(h_t, axis=0, keepdims=True)
            hc = h_t - mu
            var = jnp.mean(hc * hc, axis=0, keepdims=True)
            hn = jnp.transpose(hc * lax.rsqrt(var + LN_EPS)) * hng_ref[:, hs]
            out_ref[sl, hs] = (jax.nn.sigmoid(o_ref[sl, hs]) * hn).astype(BF16)
        return carry

    if unroll:
        for c in range(nc):
            prep(c, 0)
        ms = tuple(ms)
        for i in range(nc):
            ms = body(i, ms)
        for c in range(nc):
            finish(c, 0)
    else:
        half = nc // 2
        prep(0, 0)
        prep(nc - 1, 0)

        def first_half(i, ms):
            ms = body(i, ms)
            prep(i + 1, 0)
            prep(nc - 2 - i, 0)
            return ms

        def second_half(i, ms):
            ms = body(i, ms)
            finish(i, 0)
            finish(nc - 1 - i, 0)
            return ms

        ms = lax.fori_loop(0, half, first_half, tuple(ms), unroll=True)
        ms = lax.fori_loop(half, nc, second_half, ms, unroll=True)

    if emit_state:
        for slot in range(c_ref.shape[0]):
            if slot != state_slot:
                c_ref[slot] = jnp.zeros(c_ref.shape[1:], F32)
                n_ref[slot] = jnp.zeros(n_ref.shape[1:], F32)
                m_ref[slot] = jnp.zeros(m_ref.shape[1:], F32)
        for p in range(npair):
            di, hh = divmod(p, hp)
            c_ref[state_slot, di, hh] = jnp.transpose(ct_s[p, 0:dh, :])
            n_ref[state_slot, di, hh] = ct_s[p, dh:dh + 1, :]
            m_ref[state_slot, di, hh] = ms[p]


def _mlstm_heads_per_step(seq, dh):
    return max(1, min(MLSTM_HEADS, MLSTM_STEP_BYTES // (seq * dh * 4)))


def _gate_row_order(hp):
    heads = MLSTM_HEADS
    order = []
    for g in range(heads // hp):
        for kind in range(2):
            for di in range(2):
                for hh in range(hp):
                    order.append((di * 2 + kind) * heads + g * hp + hh)
    return np.asarray(order)


def _mlstm(proj, v_t, gates_t, wl, w_qk_conv, hn_g, batch, seq, col0, dh, init_state=None, layer=None, depth=None,
           prev=None):
    heads = MLSTM_HEADS
    chunk = MLSTM_CHUNK
    assert chunk == LANES and seq % chunk == 0 and (seq // chunk <= MLSTM_UNROLL_CHUNKS or seq // chunk % 2 == 0)
    nc = seq // chunk
    hp = _mlstm_heads_per_step(seq, dh)
    hg = heads // hp
    npair = 2 * hp
    aug = dh + BF16_ROWS
    width = hp * dh
    cb = col0 // width
    gb = heads * dh // width
    zero_init = init_state is None
    emit_state = layer is not None
    prev = () if prev is None else tuple(prev)
    slots, first_slot = (1, layer) if prev else (depth, 0)
    kern = functools.partial(_mlstm_kernel, seq=seq, chunk=chunk, hp=hp, zero_init=zero_init, emit_state=emit_state,
                             n_unused=len(prev), state_slot=(layer - first_slot) if emit_state else None)

    def tok_spec(group):
        return pl.BlockSpec((seq, width), lambda b, g: (b, cb + group * gb + g))

    in_specs = [
        tok_spec(0), tok_spec(1), pl.BlockSpec((nc, width, chunk), lambda b, g: (b, g, 0)), tok_spec(2),
        pl.BlockSpec((nc, 2 * npair, chunk), lambda b, g: (b, g, 0)),
        pl.BlockSpec((None, 3, width), lambda b, g: (wl, 0, g)),
        pl.BlockSpec((None, 3, width), lambda b, g: (wl, 0, gb + g)),
        pl.BlockSpec((None, 1, width), lambda b, g: (wl, 0, g)),
    ]
    args = [proj, proj, v_t, proj, gates_t, w_qk_conv, w_qk_conv, hn_g]
    if not zero_init:
        in_specs += [pl.BlockSpec((None, None, 2, hp, dh, dh), lambda b, g: (b, wl, 0, g, 0, 0)),
                     pl.BlockSpec((None, None, 2, hp, 1, dh), lambda b, g: (b, wl, 0, g, 0, 0)),
                     pl.BlockSpec((None, None, 2, hp, 1, 1), lambda b, g: (b, wl, 0, g, 0, 0))]
        args += list(init_state)
    aliases = {}
    for k, arr in enumerate(prev):
        aliases[len(args)] = 1 + k
        in_specs.append(pl.BlockSpec(memory_space=pl.ANY))
        args.append(arr)
    out_shape = [jax.ShapeDtypeStruct((batch * seq, heads * dh), BF16)]
    out_specs = [pl.BlockSpec((seq, width), lambda b, g: (b, g))]
    if emit_state:
        out_shape += [jax.ShapeDtypeStruct((batch, depth, 2, heads, dh, dh), F32),
                      jax.ShapeDtypeStruct((batch, depth, 2, heads, 1, dh), F32),
                      jax.ShapeDtypeStruct((batch, depth, 2, heads, 1, 1), F32)]
        out_specs += [pl.BlockSpec((None, slots, 2, hp, dh, dh), lambda b, g: (b, first_slot, 0, g, 0, 0)),
                      pl.BlockSpec((None, slots, 2, hp, 1, dh), lambda b, g: (b, first_slot, 0, g, 0, 0)),
                      pl.BlockSpec((None, slots, 2, hp, 1, 1), lambda b, g: (b, first_slot, 0, g, 0, 0))]
    res = pl.pallas_call(
        kern,
        out_shape=tuple(out_shape),
        grid=(batch, hg),
        in_specs=in_specs,
        out_specs=tuple(out_specs),
        scratch_shapes=[
            pltpu.VMEM((seq + 2 * SUBLANES, width), F32), pltpu.VMEM((seq + 2 * SUBLANES, width), F32),
            pltpu.VMEM((seq, width), BF16), pltpu.VMEM((seq, width), BF16),
            pltpu.VMEM((hp * nc, dh, chunk), F32), pltpu.VMEM((hp * nc, dh, chunk), F32),
            pltpu.VMEM((seq, npair * LANES), F32), pltpu.VMEM((nc, npair, chunk), F32),
            pltpu.VMEM((nc, npair, chunk), F32),
            pltpu.VMEM((npair, aug, dh), F32),
            pltpu.VMEM((npair, chunk, chunk), F32), pltpu.VMEM((npair, SUBLANES, chunk), F32),
            pltpu.VMEM((npair // 2, chunk + aug, 2 * chunk), F32), pltpu.VMEM((npair, chunk, chunk + dh), BF16),
            pltpu.VMEM((npair, aug, chunk + dh), F32),
        ],
        input_output_aliases=aliases,
        compiler_params=_params("parallel", "parallel"),
        name="mlstm",
    )(*args)
    return res[0], tuple(res[1:])


def _outproj_kernel(pool_ref, ml_ref, x_ref, mod_ref, wp_ref, wm_ref, g_ref, b_ref, x1_ref, h2_ref, *, alpha):
    sub = TOKEN_SUB
    for s in range(x_ref.shape[0] // sub):
        rs = slice(s * sub, (s + 1) * sub)
        mix = _dot(pool_ref[rs, :], wp_ref[...]) + _dot(ml_ref[rs, :], wm_ref[...])
        x1 = _layer_norm(alpha * x_ref[rs, :] + mod_ref[2:3, :] * mix) * g_ref[...] + b_ref[...]
        x1_ref[rs, :] = x1
        h2 = _layer_norm(x1) * (1.0 + mod_ref[4:5, :]) + mod_ref[3:4, :]
        h2_ref[rs, :] = h2.astype(BF16)


def _outproj(pool_out, ml_out, x, mod, cond_row, layer, w_out, ln_g, ln_b, alpha):
    tokens, d = x.shape
    half = pool_out.shape[1]
    tm = TOKEN_TILE
    kern = functools.partial(_outproj_kernel, alpha=alpha)
    return pl.pallas_call(
        kern,
        out_shape=(jax.ShapeDtypeStruct((tokens, d), F32), jax.ShapeDtypeStruct((tokens, d), BF16)),
        grid=(tokens // tm,),
        in_specs=[
            pl.BlockSpec((tm, half), lambda i: (i, 0)),
            pl.BlockSpec((tm, half), lambda i: (i, 0)),
            pl.BlockSpec((tm, d), lambda i: (i, 0)),
            pl.BlockSpec((None, None, 6, d), lambda i: (layer, cond_row(i * tm), 0, 0)),
            pl.BlockSpec((None, half, d), lambda i: (layer, 0, 0)),
            pl.BlockSpec((None, half, d), lambda i: (layer, 1, 0)),
            pl.BlockSpec((None, 1, d), lambda i: (layer, 0, 0)),
            pl.BlockSpec((None, 1, d), lambda i: (layer, 0, 0)),
        ],
        out_specs=(pl.BlockSpec((tm, d), lambda i: (i, 0)), pl.BlockSpec((tm, d), lambda i: (i, 0))),
        compiler_params=_params("parallel"),
        name="outproj",
    )(pool_out, ml_out, x, mod, w_out, w_out, ln_g, ln_b)


def _gelu_tanh(x):
    return 0.5 * x * (1.0 + jnp.tanh(math.sqrt(2.0 / math.pi) * (x + 0.044715 * (x * x * x))))


def _ffn_kernel(h_ref, x1_ref, mod_ref, wa_ref, wg_ref, wdw_ref, bdw_ref, wd_ref, lng_ref, lnb_ref, out_ref,
                gpad, a_s, u_s, wa_b, wg_b, wd_b, *, rows, cols, tokens, alpha):
    j = pl.program_id(1)
    tf = wa_ref.shape[1]
    blk = FFN_BLOCK
    nb = tokens // blk
    base = (cols if rows > 1 else 0) + SUBLANES

    @pl.when(j == 0)
    def _():
        out_ref[...] = jnp.zeros_like(out_ref)

    wa_b[...] = wa_ref[...].astype(BF16)
    wg_b[...] = wg_ref[...].astype(BF16)
    wd_b[...] = wd_ref[...].astype(BF16)
    gpad[0:base, :] = jnp.zeros((base, tf), F32)
    gpad[base + tokens:base + tokens + base, :] = jnp.zeros((base, tf), F32)

    wdw = wdw_ref[...]
    bdw = bdw_ref[...]
    sub = max(cols, FFN_CONV_ROWS)
    ci = lax.broadcasted_iota(jnp.int32, (sub, 1), 0) & (cols - 1)
    first = ci == 0
    last = ci == cols - 1
    taps_y = (-1, 0, 1) if rows > 1 else (0,)

    def up(b):
        hb = h_ref[b * blk:(b + 1) * blk, :]
        a_s[b * blk:(b + 1) * blk, :] = _dot(hb, wa_b[...])
        gpad[base + b * blk:base + (b + 1) * blk, :] = _dot(hb, wg_b[...])

    def conv_gate(b):
        for r in range(b * blk // sub, (b + 1) * blk // sub):
            off = r * sub
            for lb in range(tf // LANES):
                ls = slice(lb * LANES, (lb + 1) * LANES)
                left = mid = right = None
                for dy in taps_y:
                    r0 = base + off + dy * cols
                    kk = (dy + 1) * 3
                    tl = gpad[r0 - 1:r0 - 1 + sub, ls] * wdw[kk:kk + 1, ls]
                    tc = gpad[r0:r0 + sub, ls] * wdw[kk + 1:kk + 2, ls]
                    tr = gpad[r0 + 1:r0 + 1 + sub, ls] * wdw[kk + 2:kk + 3, ls]
                    left = tl if left is None else left + tl
                    mid = tc if mid is None else mid + tc
                    right = tr if right is None else right + tr
                y = mid + jnp.where(first, 0.0, left) + jnp.where(last, 0.0, right) + bdw[:, ls]
                u_s[off:off + sub, ls] = (_gelu_tanh(y) * a_s[off:off + sub, ls]).astype(BF16)

    def down(b):
        out_ref[b * blk:(b + 1) * blk, :] += _dot(u_s[b * blk:(b + 1) * blk, :], wd_b[...])

    for step in range(nb + 2):
        if step < nb:
            up(step)
        if 1 <= step <= nb:
            conv_gate(step - 1)
        if step >= 2:
            down(step - 2)

    @pl.when(j == pl.num_programs(1) - 1)
    def _():
        gate = mod_ref[5:6, :]
        lng = lng_ref[...]
        lnb = lnb_ref[...]

        def ln_body(r, carry):
            sl = pl.ds(pl.multiple_of(r * LN_ROWS, LN_ROWS), LN_ROWS)
            z = alpha * x1_ref[sl, :] + gate * out_ref[sl, :]
            out_ref[sl, :] = _layer_norm(z) * lng + lnb
            return carry

        lax.fori_loop(0, tokens // LN_ROWS, ln_body, 0)


def _ffn(h2, x1, mod, cond_row, layer, w_up, w_dw, b_dw, w_down, ln_g, ln_b, rows, cols, alpha):
    total, d = x1.shape
    d_ff = w_down.shape[1]
    tm = FFN_TOKENS
    tf = FFN_COLS
    nj = d_ff // tf
    kern = functools.partial(_ffn_kernel, rows=rows, cols=cols, tokens=tm, alpha=alpha)
    return pl.pallas_call(
        kern,
        out_shape=jax.ShapeDtypeStruct((total, d), F32),
        grid=(total // tm, nj),
        in_specs=[
            pl.BlockSpec((tm, d), lambda i, j: (i, 0)),
            pl.BlockSpec((tm, d), lambda i, j: (i, 0)),
            pl.BlockSpec((None, None, 6, d), lambda i, j: (layer, cond_row(i * tm), 0, 0)),
            pl.BlockSpec((None, d, tf), lambda i, j: (layer, 0, j)),
            pl.BlockSpec((None, d, tf), lambda i, j: (layer, 0, nj + j)),
            pl.BlockSpec((None, 9, tf), lambda i, j: (layer, 0, j)),
            pl.BlockSpec((None, 1, tf), lambda i, j: (layer, 0, j)),
            pl.BlockSpec((None, tf, d), lambda i, j: (layer, j, 0)),
            pl.BlockSpec((None, 1, d), lambda i, j: (layer, 0, 0)),
            pl.BlockSpec((None, 1, d), lambda i, j: (layer, 0, 0)),
        ],
        out_specs=pl.BlockSpec((tm, d), lambda i, j: (i, 0)),
        scratch_shapes=[
            pltpu.VMEM((tm + 2 * ((cols if rows > 1 else 0) + SUBLANES), tf), F32),
            pltpu.VMEM((tm, tf), F32),
            pltpu.VMEM((tm, tf), BF16),
            pltpu.VMEM((d, tf), BF16), pltpu.VMEM((d, tf), BF16), pltpu.VMEM((tf, d), BF16),
        ],
        compiler_params=_params("parallel", "arbitrary"),
        name="ffn",
    )(h2, x1, mod, w_up, w_up, w_dw, b_dw, w_down, ln_g, ln_b)


def _trunk_layer(x, batch, rows, cols, cond_row, layer, p, pmat, alpha, dh, state_kw):
    seq = rows * cols
    hp = _mlstm_heads_per_step(seq, dh)
    mod = p["ada"]
    proj, v_t, gates_t = _inproj(x, mod, cond_row, layer, p["w_in_main"], p["w_v_t"], p["w_gate_t"][hp],
                                 p["b_gate"][hp])
    pool_out = _pool(proj, pmat, layer, p["w_pool"], p["pool_scale"], rows, cols, min(FFN_TOKENS, batch * seq))
    ml_out, states = _mlstm(proj, v_t, gates_t, layer, p["w_qk_conv"], p["hn_g"], batch, seq, N_POOL_GROUPS * LANES,
                            dh, **state_kw)
    x1, h2 = _outproj(pool_out, ml_out, x, mod, cond_row, layer, p["w_out"], p["ln1_g"], p["ln1_b"], alpha)
    x2 = _ffn(h2, x1, mod, cond_row, layer, p["w_up"], p["w_dw"], p["b_dw"], p["w_down"], p["ln2_g"], p["ln2_b"],
              rows, cols, alpha)
    return x2, states


def kernel(x_prompt, x_sample, c, state_C, state_n, state_m, c_ctx, w_ada, b_ada, w_in, b_gates, w_qk_conv, hn_g,
           w_pool, pool_scale, w_out, ln1_g, ln1_b, w_up, w_dw, b_dw, w_down, ln2_g, ln2_b):
    b_p, l_p, d = x_prompt.shape
    b_s, l_s, _ = x_sample.shape
    depth = w_ada.shape[0]
    heads = MLSTM_HEADS
    dh = state_C.shape[-1]
    pool_w = N_POOL_GROUPS * LANES
    mw = heads * dh
    n_main = pool_w + 4 * mw
    v0 = pool_w + 2 * mw
    d_ff = w_down.shape[1]
    alpha = (2.0 * depth) ** 0.25
    rows_s = l_s // GRID_W
    assert b_s + 1 <= COND_ROWS and l_p % POOL_TILE == 0 and POOL_TILE % GRID_W == 0
    assert l_p & (l_p - 1) == 0 and GRID_W & (GRID_W - 1) == 0

    cond = jnp.concatenate([c_ctx[None, :], c, jnp.zeros((COND_ROWS - 1 - b_s, d), F32)], axis=0)

    pmat_p = _pool_col_matrices(l_p)
    pmat_s = _pool_col_matrices(GRID_W)
    assert min(FFN_TOKENS, b_s * l_s) == l_s
    hps = sorted({_mlstm_heads_per_step(l_p, dh), _mlstm_heads_per_step(l_s, dh)})

    w_gate = w_in[:, :, n_main:]
    p = dict(
        ada=_ada(cond, w_ada, b_ada).reshape(depth, COND_ROWS, 6, d),
        w_in_main=jnp.concatenate([w_in[:, :, :v0], w_in[:, :, v0 + mw:n_main]], axis=2).astype(BF16),
        w_v_t=w_in[:, :, v0:v0 + mw].transpose(0, 2, 1).astype(BF16),
        w_gate_t={hp: w_gate[:, :, _gate_row_order(hp)].transpose(0, 2, 1).astype(BF16) for hp in hps},
        b_gate={hp: b_gates[:, _gate_row_order(hp)][:, :, None] for hp in hps},
        w_qk_conv=w_qk_conv, hn_g=hn_g[:, None, :],
        w_pool=w_pool.astype(BF16), pool_scale=pool_scale[:, None, :],
        w_out=w_out.astype(BF16), ln1_g=ln1_g[:, None, :], ln1_b=ln1_b[:, None, :],
        w_up=w_up, w_dw=w_dw.reshape(depth, 9, d_ff), b_dw=b_dw[:, None, :],
        w_down=w_down, ln2_g=ln2_g[:, None, :], ln2_b=ln2_b[:, None, :])
    init = (state_C, state_n.reshape(b_s, depth, 2, heads, 1, dh), state_m.reshape(b_s, depth, 2, heads, 1, 1))

    y_p = x_prompt.reshape(b_p * l_p, d)
    y_s = x_sample.reshape(b_s * l_s, d)
    states = None
    for l in range(depth):
        y_p, states = _trunk_layer(y_p, b_p, 1, l_p, lambda t: 0, l, p, pmat_p, alpha, dh,
                                   dict(layer=l, depth=depth, prev=states))
        y_s, _ = _trunk_layer(y_s, b_s, rows_s, GRID_W, lambda t: 1 + t // l_s, l, p, pmat_s, alpha, dh,
                              dict(init_state=init))
    new_c, new_n, new_m = states
    return (y_p.reshape(b_p, l_p, d), y_s.reshape(b_s, l_s, d), new_c, new_n.reshape(b_p, depth, 2, heads, dh),
            new_m.reshape(b_p, depth, 2, heads))
```

```python
import functools
import math

import numpy as np
import jax
import jax.numpy as jnp
from jax import lax
from jax.experimental import pallas as pl
from jax.experimental.pallas import tpu as pltpu

F32 = jnp.float32
BF16 = jnp.bfloat16

GRID_W = 64
POOL_WINDOWS = (2, 4, 8, 16)
N_POOL_GROUPS = len(POOL_WINDOWS)
MLSTM_HEADS = 4
LN_EPS = 1e-5

LANES = 128
SUBLANES = 8
BF16_ROWS = 16
VMEM_LIMIT_BYTES = 56 * 1024 * 1024

COND_ROWS = SUBLANES
ADA_COLS = 3072
MLSTM_CHUNK = LANES
MLSTM_UNROLL_CHUNKS = 2
MLSTM_STEP_BYTES = 2 * 1024 * 1024
POOL_TILE = 256
TOKEN_TILE = 1024
TOKEN_SUB = 256
XBUF_SLOTS = 3
FFN_TOKENS = 2048
FFN_BLOCK = 512
FFN_COLS = 256
FFN_CONV_ROWS = 256
LN_ROWS = 256


def _params(*sem):
    return pltpu.CompilerParams(dimension_semantics=sem, vmem_limit_bytes=VMEM_LIMIT_BYTES)


def _layer_norm(x):
    mu = jnp.mean(x, axis=-1, keepdims=True)
    xc = x - mu
    var = jnp.mean(xc * xc, axis=-1, keepdims=True)
    return xc * lax.rsqrt(var + LN_EPS)


def _dot(a, b):
    return jnp.dot(a, b, preferred_element_type=F32)


def _dot_nt(a, b):
    return lax.dot_general(a, b, (((1,), (1,)), ((), ())), preferred_element_type=F32)


def _ada_kernel(cond_ref, w_ref, b_ref, out_ref):
    cnd = cond_ref[...]
    act = (cnd * jax.nn.sigmoid(cnd)).astype(BF16)
    out_ref[...] = _dot(act, w_ref[...].astype(BF16)) + b_ref[...]


def _ada(cond, w_ada, b_ada):
    depth, d, n = w_ada.shape
    tn = ADA_COLS
    return pl.pallas_call(
        _ada_kernel,
        out_shape=jax.ShapeDtypeStruct((depth, COND_ROWS, n), F32),
        grid=(depth, n // tn),
        in_specs=[
            pl.BlockSpec((COND_ROWS, d), lambda l, j: (0, 0)),
            pl.BlockSpec((None, d, tn), lambda l, j: (l, 0, j)),
            pl.BlockSpec((None, 1, tn), lambda l, j: (l, 0, j)),
        ],
        out_specs=pl.BlockSpec((None, COND_ROWS, tn), lambda l, j: (l, 0, j)),
        compiler_params=_params("parallel", "parallel"),
        name="ada",
    )(cond, w_ada, b_ada.reshape(depth, 1, n))


def _inproj_kernel(x_ref, mod_ref, w_ref, wvt_ref, wgt_ref, bg_ref, proj_ref, vt_ref, gates_ref):
    sub = TOKEN_SUB
    for s in range(x_ref.shape[0] // sub):
        rs = slice(s * sub, (s + 1) * sub)
        h = _layer_norm(x_ref[rs, :]) * (1.0 + mod_ref[1:2, :]) + mod_ref[0:1, :]
        h = h.astype(BF16)
        proj_ref[rs, :] = _dot(h, w_ref[...])
        v_t = _dot_nt(wvt_ref[...], h).astype(BF16)
        gates_t = _dot_nt(wgt_ref[...], h) + bg_ref[...]
        for k in range(sub // LANES):
            vt_ref[s * (sub // LANES) + k] = v_t[:, k * LANES:(k + 1) * LANES]
            gates_ref[s * (sub // LANES) + k] = gates_t[:, k * LANES:(k + 1) * LANES]


def _inproj(x, mod, cond_row, layer, w_main, w_v_t, w_gate_t, b_gate):
    tokens, d = x.shape
    n = w_main.shape[2]
    nv = w_v_t.shape[1]
    ng = w_gate_t.shape[1]
    tm = TOKEN_TILE
    return pl.pallas_call(
        _inproj_kernel,
        out_shape=(jax.ShapeDtypeStruct((tokens, n), F32), jax.ShapeDtypeStruct((tokens // LANES, nv, LANES), BF16),
                   jax.ShapeDtypeStruct((tokens // LANES, ng, LANES), F32)),
        grid=(tokens // tm,),
        in_specs=[
            pl.BlockSpec((tm, d), lambda i: (i, 0)),
            pl.BlockSpec((None, None, 6, d), lambda i: (layer, cond_row(i * tm), 0, 0)),
            pl.BlockSpec((None, d, n), lambda i: (layer, 0, 0)),
            pl.BlockSpec((None, nv, d), lambda i: (layer, 0, 0)),
            pl.BlockSpec((None, ng, d), lambda i: (layer, 0, 0)),
            pl.BlockSpec((None, ng, 1), lambda i: (layer, 0, 0)),
        ],
        out_specs=(pl.BlockSpec((tm, n), lambda i: (i, 0)),
                   pl.BlockSpec((tm // LANES, nv, LANES), lambda i: (i, 0, 0)),
                   pl.BlockSpec((tm // LANES, ng, LANES), lambda i: (i, 0, 0))),
        compiler_params=_params("parallel"),
        name="inproj",
    )(x, mod, w_main, w_v_t, w_gate_t, b_gate)


def _pool_col_matrices(cols):
    t = np.arange(POOL_TILE)
    row, col = t // cols, t % cols
    mats = []
    for win in POOL_WINDOWS:
        hw = win // 2
        lo = np.maximum(col - hw, 0)
        hi = np.minimum(col + hw, cols)
        same_row = row[:, None] == row[None, :]
        inside = (col[None, :] >= lo[:, None]) & (col[None, :] < hi[:, None])
        mats.append((same_row & inside).astype(np.float32))
    return jnp.asarray(np.stack(mats), dtype=BF16)


def _pool_kernel(xp_ref, pmat_ref, wp_ref, ps_ref, out_ref, pad_ref, hl_s, y_s, *, rows, cols, tokens):
    gdim = LANES
    tile = POOL_TILE
    halo = (max(POOL_WINDOWS) // 2) * cols if rows > 1 else 0
    t = lax.broadcasted_iota(jnp.int32, (tile, 1), 0)
    col = t & (cols - 1)
    if rows > 1:
        zeros = jnp.zeros((halo, gdim), F32)
        pad_ref[0:halo, :] = zeros
        pad_ref[halo + tokens:halo + tokens + halo, :] = zeros
    for gi, win in enumerate(POOL_WINDOWS):
        hw = win // 2
        gs = slice(gi * gdim, (gi + 1) * gdim)
        if rows > 1:
            pad_ref[halo:halo + tokens, :] = xp_ref[:, gs]
        cnt_c = (jnp.minimum(col + hw, cols) - jnp.maximum(col - hw, 0)).astype(F32)
        pm = pmat_ref[gi]
        wp = wp_ref[gi]
        scale = ps_ref[:, gs]

        nt = tokens // tile
        for tt in range(nt):
            off = tt * tile
            if rows > 1:
                acc = None
                for dr in range(-hw, hw):
                    term = pad_ref[off + halo + dr * cols:off + halo + dr * cols + tile, :]
                    acc = term if acc is None else acc + term
                rw = (off + t) >> int(math.log2(cols))
                cnt_r = (jnp.minimum(rw + hw, rows) - jnp.maximum(rw - hw, 0)).astype(F32)
                m1 = acc / cnt_r
            else:
                m1 = xp_ref[off:off + tile, gs]
            hi = m1.astype(BF16)
            hl_s[gi * nt + tt] = jnp.concatenate([hi, (m1 - hi.astype(F32)).astype(BF16)], axis=1)
        for tt in range(nt):
            off = tt * tile
            both = _dot(pm, hl_s[gi * nt + tt])
            m2 = (both[:, :gdim] + both[:, gdim:]) / cnt_c
            y_s[gi, off:off + tile, :] = (m2 - xp_ref[off:off + tile, gs]).astype(BF16)
        out_ref[:, gs] = (_dot(y_s[gi], wp) * scale).astype(BF16)


def _pool(proj, pmat, layer, w_pool, pool_scale, rows, cols, step_tokens):
    total = proj.shape[0]
    pool_w = N_POOL_GROUPS * LANES
    halo = (max(POOL_WINDOWS) // 2) * cols if rows > 1 else 0
    kern = functools.partial(_pool_kernel, rows=rows, cols=cols, tokens=step_tokens)
    return pl.pallas_call(
        kern,
        out_shape=jax.ShapeDtypeStruct((total, pool_w), BF16),
        grid=(total // step_tokens,),
        in_specs=[
            pl.BlockSpec((step_tokens, pool_w), lambda i: (i, 0)),
            pl.BlockSpec(pmat.shape, lambda i: (0, 0, 0)),
            pl.BlockSpec((None,) + w_pool.shape[1:], lambda i: (layer, 0, 0, 0)),
            pl.BlockSpec((None, 1, pool_w), lambda i: (layer, 0, 0)),
        ],
        out_specs=pl.BlockSpec((step_tokens, pool_w), lambda i: (i, 0)),
        scratch_shapes=[
            pltpu.VMEM((step_tokens + 2 * halo if rows > 1 else SUBLANES, LANES), F32),
            pltpu.VMEM((N_POOL_GROUPS * (step_tokens // POOL_TILE), POOL_TILE, 2 * LANES), BF16),
            pltpu.VMEM((N_POOL_GROUPS, step_tokens, LANES), BF16),
        ],
        compiler_params=_params("parallel"),
        name="pool",
    )(proj, pmat, w_pool, pool_scale)


def _log_sigmoid(x):
    return jnp.minimum(x, 0.0) - jnp.log(1.0 + jnp.exp(-jnp.abs(x)))


def _split3(x):
    hi = x.astype(BF16)
    r1 = x - hi.astype(F32)
    mid = r1.astype(BF16)
    lo = (r1 - mid.astype(F32)).astype(BF16)
    return lo, mid, hi


def _mlstm_kernel(*refs, seq, chunk, hp, zero_init, emit_state, n_unused, state_slot):
    refs = list(refs)
    q_ref, k_ref, vt_ref, o_ref, g_ref, wq_ref, wk_ref, hng_ref = refs[:8]
    pos = 8
    if not zero_init:
        c0_ref, n0_ref, m0_ref = refs[pos:pos + 3]
        pos += 3
    pos += n_unused
    out_ref = refs[pos]
    pos += 1
    if emit_state:
        c_ref, n_ref, m_ref = refs[pos:pos + 3]
        pos += 3
    qpad, kpad, qs, ks, ht_f, ht_b, rcb_s, br_s, rc_s, ct_s, dt_s, mrow_s, z1_s, rhs_s, z2_s = refs[pos:]
    dh = q_ref.shape[1] // hp
    nc = seq // chunk
    npair = 2 * hp
    aug = dh + BF16_ROWS
    margin = SUBLANES
    unroll = nc <= MLSTM_UNROLL_CHUNKS

    for pad_ref, src_ref in ((qpad, q_ref), (kpad, k_ref)):
        pad_ref[0:margin, :] = jnp.zeros((margin, hp * dh), F32)
        pad_ref[margin + seq:2 * margin + seq, :] = jnp.zeros((margin, hp * dh), F32)
        pad_ref[margin:margin + seq, :] = src_ref[...]

    def conv_silu(pad_ref, w_ref, c, hs):
        w = w_ref[:, hs]
        start = c * chunk if unroll else pl.multiple_of(c * chunk, chunk)
        win = pad_ref[pl.ds(start, chunk + 2 * margin), hs]
        y = (pltpu.roll(win, 1, 0) * w[0:1, :] + win * w[1:2, :]
             + pltpu.roll(win, chunk + 2 * margin - 1, 0) * w[2:3, :])[margin:margin + chunk]
        return y * jax.nn.sigmoid(y)

    ri = lax.broadcasted_iota(jnp.int32, (chunk, chunk), 0)
    ci = lax.broadcasted_iota(jnp.int32, (chunk, chunk), 1)
    lower = ci <= ri
    upper = ci >= ri
    upper_b = jnp.where(upper, 1.0, 0.0).astype(BF16)
    row_is_fwd = lax.broadcasted_iota(jnp.int32, (npair, chunk), 0) < hp
    ones_row = jnp.where(lax.broadcasted_iota(jnp.int32, (BF16_ROWS, chunk), 0) == 0, 1.0, 0.0).astype(BF16)

    gates = g_ref[...]
    lf = _log_sigmoid(gates[:, npair:, :])
    lf3 = _split3(lf.reshape(nc * npair, chunk))
    prefix = (_dot(lf3[0], upper_b) + _dot(lf3[1], upper_b) + _dot(lf3[2], upper_b)).reshape(nc, npair, chunk)
    suffix = prefix[:, :, chunk - 1:chunk] - prefix + lf
    br = jnp.where(row_is_fwd, prefix, suffix)
    br_s[...] = br
    rc_s[...] = gates[:, :npair, :] - br

    def rows_of(c):
        return slice(c * chunk, (c + 1) * chunk) if unroll else pl.ds(pl.multiple_of(c * chunk, chunk), chunk)

    def prep(c, carry):
        sl = rows_of(c)
        rc = rc_s[c]
        for p in range(npair):
            rcb_s[sl, p * LANES:(p + 1) * LANES] = jnp.transpose(jnp.broadcast_to(rc[p:p + 1, :], (LANES, chunk)))
        for hh in range(hp):
            hs = slice(hh * dh, (hh + 1) * dh)
            qs[sl, hs] = conv_silu(qpad, wq_ref, c, hs).astype(BF16)
            ks[sl, hs] = (conv_silu(kpad, wk_ref, c, hs) * (dh ** -0.5)).astype(BF16)
        return carry

    n_rows = lax.broadcasted_iota(jnp.int32, (BF16_ROWS, dh), 0) == 0
    ms = []
    for p in range(npair):
        di, hh = divmod(p, hp)
        if zero_init:
            ct_s[p] = jnp.zeros((aug, dh), F32)
            ms.append(jnp.zeros((1, 1), F32))
        else:
            ct_s[p, 0:dh, :] = jnp.transpose(c0_ref[di, hh])
            ct_s[p, dh:aug, :] = jnp.where(n_rows, n0_ref[di, hh], 0.0)
            ms.append(m0_ref[di, hh])

    def pair_args(p, i):
        di, hh = divmod(p, hp)
        fwd = di == 0
        c = i if fwd else nc - 1 - i
        last = chunk - 1 if fwd else 0
        sl = rows_of(c)
        return fwd, hh, c, last, sl, slice(hh * dh, (hh + 1) * dh)

    def body(i, ms):
        for p in range(npair):
            fwd, hh, c, last, sl, hs = pair_args(p, i)
            rcb = rcb_s[sl, p * LANES:(p + 1) * LANES]
            xt = jnp.where(upper if fwd else lower, rcb, -jnp.inf)
            mrow = jnp.maximum(jnp.max(xt, axis=0, keepdims=True), ms[p])
            dt_s[p] = jnp.exp(xt - mrow)
            mrow_s[p] = jnp.broadcast_to(mrow, (SUBLANES, chunk))
            w = jnp.exp(rcb - mrow[:, last:last + 1])
            rhs_s[p, :, chunk:] = (w * ks[sl, hs].astype(F32)).astype(BF16)
        zeros_q = jnp.zeros((chunk, dh), BF16)
        for pk in range(npair // 2):
            _, _, _, _, sl_a, hs_a = pair_args(2 * pk, i)
            _, _, _, _, sl_b, hs_b = pair_args(2 * pk + 1, i)
            lhs = jnp.concatenate([
                jnp.concatenate([ks[sl_a, hs_a], ks[sl_b, hs_b]], axis=1),
                jnp.concatenate([ct_s[2 * pk].astype(BF16), ct_s[2 * pk + 1].astype(BF16)], axis=1)], axis=0)
            q_diag = jnp.concatenate([
                jnp.concatenate([qs[sl_a, hs_a], zeros_q], axis=1),
                jnp.concatenate([zeros_q, qs[sl_b, hs_b]], axis=1)], axis=0)
            z1_s[pk] = _dot_nt(lhs, q_diag)
        for p in range(npair):
            lanes = slice((p % 2) * chunk, (p % 2 + 1) * chunk)
            rhs_s[p, :, 0:chunk] = (z1_s[p // 2, 0:chunk, lanes] * dt_s[p]).astype(BF16)
        for p in range(npair):
            fwd, hh, c, last, sl, hs = pair_args(p, i)
            z2_s[p] = _dot(jnp.concatenate([vt_ref[c, hs, :], ones_row], axis=0), rhs_s[p])
        new_ms = []
        for p in range(npair):
            fwd, hh, c, last, sl, hs = pair_args(p, i)
            m = ms[p]
            mrow = mrow_s[p, 0:1, :]
            lanes = slice((p % 2) * chunk, (p % 2 + 1) * chunk)
            num = z1_s[p // 2, chunk:, lanes] * jnp.exp(m - mrow) + z2_s[p, :, 0:chunk]
            b_r = br_s[c, p:p + 1, :]
            scale = 1.0 / jnp.maximum(jnp.abs(num[dh:dh + 1, :]), jnp.exp(-(b_r + mrow)))
            h_t = num[0:dh, :] * scale
            if fwd:
                ht_f[hh * nc + c] = h_t
            else:
                ht_b[hh * nc + c] = h_t
            mlast = mrow[:, last:last + 1]
            ct_s[p] = jnp.exp(m - mlast) * ct_s[p] + z2_s[p, :, chunk:]
            new_ms.append(b_r[:, last:last + 1] + mlast)
        return tuple(new_ms)

    def finish(c, carry):
        sl = rows_of(c)
        for hh in range(hp):
            hs = slice(hh * dh, (hh + 1) * dh)
            h_t = ht_f[hh * nc + c] + ht_b[hh * nc + c]
            mu = jnp.mean(h_t, axis=0, keepdims=True)
            hc = h_t - mu
            var = jnp.mean(hc * hc, axis=0, keepdims=True)
            hn = jnp.transpose(hc * lax.rsqrt(var + LN_EPS)) * hng_ref[:, hs]
            out_ref[sl, hs] = (jax.nn.sigmoid(o_ref[sl, hs]) * hn).astype(BF16)
        return carry

    if unroll:
        for c in range(nc):
            prep(c, 0)
        ms = tuple(ms)
        for i in range(nc):
            ms = body(i, ms)
        for c in range(nc):
            finish(c, 0)
    else:
        half = nc // 2
        prep(0, 0)
        prep(nc - 1, 0)

        def first_half(i, ms):
            ms = body(i, ms)
            prep(i + 1, 0)
            prep(nc - 2 - i, 0)
            return ms

        def second_half(i, ms):
            ms = body(i, ms)
            finish(i, 0)
            finish(nc - 1 - i, 0)
            return ms

        ms = lax.fori_loop(0, half, first_half, tuple(ms), unroll=True)
        ms = lax.fori_loop(half, nc, second_half, ms, unroll=True)

    if emit_state:
        for slot in range(c_ref.shape[0]):
            if slot != state_slot:
                c_ref[slot] = jnp.zeros(c_ref.shape[1:], F32)
                n_ref[slot] = jnp.zeros(n_ref.shape[1:], F32)
                m_ref[slot] = jnp.zeros(m_ref.shape[1:], F32)
        for p in range(npair):
            di, hh = divmod(p, hp)
            c_ref[state_slot, di, hh] = jnp.transpose(ct_s[p, 0:dh, :])
            n_ref[state_slot, di, hh] = ct_s[p, dh:dh + 1, :]
            m_ref[state_slot, di, hh] = ms[p]


def _mlstm_heads_per_step(seq, dh):
    return max(1, min(MLSTM_HEADS, MLSTM_STEP_BYTES // (seq * dh * 4)))


def _gate_row_order(hp):
    heads = MLSTM_HEADS
    order = []
    for g in range(heads // hp):
        for kind in range(2):
            for di in range(2):
                for hh in range(hp):
                    order.append((di * 2 + kind) * heads + g * hp + hh)
    return np.asarray(order)


def _mlstm(proj, v_t, gates_t, wl, w_qk_conv, hn_g, batch, seq, col0, dh, init_state=None, layer=None, depth=None,
           prev=None):
    heads = MLSTM_HEADS
    chunk = MLSTM_CHUNK
    assert chunk == LANES and seq % chunk == 0 and (seq // chunk <= MLSTM_UNROLL_CHUNKS or seq // chunk % 2 == 0)
    nc = seq // chunk
    hp = _mlstm_heads_per_step(seq, dh)
    hg = heads // hp
    npair = 2 * hp
    aug = dh + BF16_ROWS
    width = hp * dh
    cb = col0 // width
    gb = heads * dh // width
    zero_init = init_state is None
    emit_state = layer is not None
    prev = () if prev is None else tuple(prev)
    slots, first_slot = (1, layer) if prev else (depth, 0)
    kern = functools.partial(_mlstm_kernel, seq=seq, chunk=chunk, hp=hp, zero_init=zero_init, emit_state=emit_state,
                             n_unused=len(prev), state_slot=(layer - first_slot) if emit_state else None)

    def tok_spec(group):
        return pl.BlockSpec((seq, width), lambda b, g: (b, cb + group * gb + g))

    in_specs = [
        tok_spec(0), tok_spec(1), pl.BlockSpec((nc, width, chunk), lambda b, g: (b, g, 0)), tok_spec(2),
        pl.BlockSpec((nc, 2 * npair, chunk), lambda b, g: (b, g, 0)),
        pl.BlockSpec((None, 3, width), lambda b, g: (wl, 0, g)),
        pl.BlockSpec((None, 3, width), lambda b, g: (wl, 0, gb + g)),
        pl.BlockSpec((None, 1, width), lambda b, g: (wl, 0, g)),
    ]
    args = [proj, proj, v_t, proj, gates_t, w_qk_conv, w_qk_conv, hn_g]
    if not zero_init:
        in_specs += [pl.BlockSpec((None, None, 2, hp, dh, dh), lambda b, g: (b, wl, 0, g, 0, 0)),
                     pl.BlockSpec((None, None, 2, hp, 1, dh), lambda b, g: (b, wl, 0, g, 0, 0)),
                     pl.BlockSpec((None, None, 2, hp, 1, 1), lambda b, g: (b, wl, 0, g, 0, 0))]
        args += list(init_state)
    aliases = {}
    for k, arr in enumerate(prev):
        aliases[len(args)] = 1 + k
        in_specs.append(pl.BlockSpec(memory_space=pl.ANY))
        args.append(arr)
    out_shape = [jax.ShapeDtypeStruct((batch * seq, heads * dh), BF16)]
    out_specs = [pl.BlockSpec((seq, width), lambda b, g: (b, g))]
    if emit_state:
        out_shape += [jax.ShapeDtypeStruct((batch, depth, 2, heads, dh, dh), F32),
                      jax.ShapeDtypeStruct((batch, depth, 2, heads, 1, dh), F32),
                      jax.ShapeDtypeStruct((batch, depth, 2, heads, 1, 1), F32)]
        out_specs += [pl.BlockSpec((None, slots, 2, hp, dh, dh), lambda b, g: (b, first_slot, 0, g, 0, 0)),
                      pl.BlockSpec((None, slots, 2, hp, 1, dh), lambda b, g: (b, first_slot, 0, g, 0, 0)),
                      pl.BlockSpec((None, slots, 2, hp, 1, 1), lambda b, g: (b, first_slot, 0, g, 0, 0))]
    res = pl.pallas_call(
        kern,
        out_shape=tuple(out_shape),
        grid=(batch, hg),
        in_specs=in_specs,
        out_specs=tuple(out_specs),
        scratch_shapes=[
            pltpu.VMEM((seq + 2 * SUBLANES, width), F32), pltpu.VMEM((seq + 2 * SUBLANES, width), F32),
            pltpu.VMEM((seq, width), BF16), pltpu.VMEM((seq, width), BF16),
            pltpu.VMEM((hp * nc, dh, chunk), F32), pltpu.VMEM((hp * nc, dh, chunk), F32),
            pltpu.VMEM((seq, npair * LANES), F32), pltpu.VMEM((nc, npair, chunk), F32),
            pltpu.VMEM((nc, npair, chunk), F32),
            pltpu.VMEM((npair, aug, dh), F32),
            pltpu.VMEM((npair, chunk, chunk), F32), pltpu.VMEM((npair, SUBLANES, chunk), F32),
            pltpu.VMEM((npair // 2, chunk + aug, 2 * chunk), F32), pltpu.VMEM((npair, chunk, chunk + dh), BF16),
            pltpu.VMEM((npair, aug, chunk + dh), F32),
        ],
        input_output_aliases=aliases,
        compiler_params=_params("parallel", "parallel"),
        name="mlstm",
    )(*args)
    return res[0], tuple(res[1:])


def _outproj_kernel(pool_ref, ml_ref, x_hbm, mod_ref, wp_ref, wm_ref, g_ref, b_ref, x1_ref, h2_ref, xbuf, xsem, *,
                    alpha, steps):
    sub = TOKEN_SUB
    tm = x1_ref.shape[0]
    i = pl.program_id(0)

    def x_copy(step):
        slot = step % XBUF_SLOTS
        start = step * tm if isinstance(step, int) else pl.multiple_of(step * tm, tm)
        return pltpu.make_async_copy(x_hbm.at[pl.ds(start, tm)], xbuf.at[slot], xsem.at[slot])

    @pl.when(i == 0)
    def _():
        x_copy(0).start()
        if steps > 1:
            x_copy(1).start()

    @pl.when(i + 2 < steps)
    def _():
        x_copy(i + 2).start()

    x_copy(i).wait()
    x_ref = xbuf.at[i % XBUF_SLOTS]
    for s in range(tm // sub):
        rs = slice(s * sub, (s + 1) * sub)
        mix = _dot(pool_ref[rs, :], wp_ref[...]) + _dot(ml_ref[rs, :], wm_ref[...])
        x1 = _layer_norm(alpha * x_ref[rs, :] + mod_ref[2:3, :] * mix) * g_ref[...] + b_ref[...]
        x1_ref[rs, :] = x1
        h2 = _layer_norm(x1) * (1.0 + mod_ref[4:5, :]) + mod_ref[3:4, :]
        h2_ref[rs, :] = h2.astype(BF16)


def _outproj(pool_out, ml_out, x, mod, cond_row, layer, w_out, ln_g, ln_b, alpha):
    tokens, d = x.shape
    half = pool_out.shape[1]
    tm = TOKEN_TILE
    kern = functools.partial(_outproj_kernel, alpha=alpha, steps=tokens // tm)
    return pl.pallas_call(
        kern,
        out_shape=(jax.ShapeDtypeStruct((tokens, d), F32), jax.ShapeDtypeStruct((tokens, d), BF16)),
        grid=(tokens // tm,),
        in_specs=[
            pl.BlockSpec((tm, half), lambda i: (i, 0)),
            pl.BlockSpec((tm, half), lambda i: (i, 0)),
            pl.BlockSpec(memory_space=pl.ANY),
            pl.BlockSpec((None, None, 6, d), lambda i: (layer, cond_row(i * tm), 0, 0)),
            pl.BlockSpec((None, half, d), lambda i: (layer, 0, 0)),
            pl.BlockSpec((None, half, d), lambda i: (layer, 1, 0)),
            pl.BlockSpec((None, 1, d), lambda i: (layer, 0, 0)),
            pl.BlockSpec((None, 1, d), lambda i: (layer, 0, 0)),
        ],
        out_specs=(pl.BlockSpec((tm, d), lambda i: (i, 0)), pl.BlockSpec((tm, d), lambda i: (i, 0))),
        scratch_shapes=[pltpu.VMEM((XBUF_SLOTS, tm, d), F32), pltpu.SemaphoreType.DMA((XBUF_SLOTS,))],
        compiler_params=_params("arbitrary"),
        name="outproj",
    )(pool_out, ml_out, x, mod, w_out, w_out, ln_g, ln_b)


def _gelu_tanh(x):
    return 0.5 * x * (1.0 + jnp.tanh(math.sqrt(2.0 / math.pi) * (x + 0.044715 * (x * x * x))))


def _ffn_kernel(h_ref, x1_ref, mod_ref, wa_ref, wg_ref, wdw_ref, bdw_ref, wd_ref, lng_ref, lnb_ref, out_ref,
                gpad, a_s, u_s, wa_b, wg_b, wd_b, *, rows, cols, tokens, alpha):
    j = pl.program_id(1)
    tf = wa_ref.shape[1]
    blk = FFN_BLOCK
    nb = tokens // blk
    base = (cols if rows > 1 else 0) + SUBLANES

    @pl.when(j == 0)
    def _():
        out_ref[...] = jnp.zeros_like(out_ref)

    wa_b[...] = wa_ref[...].astype(BF16)
    wg_b[...] = wg_ref[...].astype(BF16)
    wd_b[...] = wd_ref[...].astype(BF16)
    gpad[0:base, :] = jnp.zeros((base, tf), F32)
    gpad[base + tokens:base + tokens + base, :] = jnp.zeros((base, tf), F32)

    wdw = wdw_ref[...]
    bdw = bdw_ref[...]
    sub = max(cols, FFN_CONV_ROWS)
    ci = lax.broadcasted_iota(jnp.int32, (sub, 1), 0) & (cols - 1)
    first = ci == 0
    last = ci == cols - 1
    taps_y = (-1, 0, 1) if rows > 1 else (0,)

    def up(b):
        hb = h_ref[b * blk:(b + 1) * blk, :]
        a_s[b * blk:(b + 1) * blk, :] = _dot(hb, wa_b[...])
        gpad[base + b * blk:base + (b + 1) * blk, :] = _dot(hb, wg_b[...])

    def conv_gate(b):
        for r in range(b * blk // sub, (b + 1) * blk // sub):
            off = r * sub
            for lb in range(tf // LANES):
                ls = slice(lb * LANES, (lb + 1) * LANES)
                left = mid = right = None
                for dy in taps_y:
                    r0 = base + off + dy * cols
                    kk = (dy + 1) * 3
                    tl = gpad[r0 - 1:r0 - 1 + sub, ls] * wdw[kk:kk + 1, ls]
                    tc = gpad[r0:r0 + sub, ls] * wdw[kk + 1:kk + 2, ls]
                    tr = gpad[r0 + 1:r0 + 1 + sub, ls] * wdw[kk + 2:kk + 3, ls]
                    left = tl if left is None else left + tl
                    mid = tc if mid is None else mid + tc
                    right = tr if right is None else right + tr
                y = mid + jnp.where(first, 0.0, left) + jnp.where(last, 0.0, right) + bdw[:, ls]
                u_s[off:off + sub, ls] = (_gelu_tanh(y) * a_s[off:off + sub, ls]).astype(BF16)

    def down(b):
        out_ref[b * blk:(b + 1) * blk, :] += _dot(u_s[b * blk:(b + 1) * blk, :], wd_b[...])

    for step in range(nb + 2):
        if step < nb:
            up(step)
        if 1 <= step <= nb:
            conv_gate(step - 1)
        if step >= 2:
            down(step - 2)

    @pl.when(j == pl.num_programs(1) - 1)
    def _():
        gate = mod_ref[5:6, :]
        lng = lng_ref[...]
        lnb = lnb_ref[...]

        def ln_body(r, carry):
            sl = pl.ds(pl.multiple_of(r * LN_ROWS, LN_ROWS), LN_ROWS)
            z = alpha * x1_ref[sl, :] + gate * out_ref[sl, :]
            out_ref[sl, :] = _layer_norm(z) * lng + lnb
            return carry

        lax.fori_loop(0, tokens // LN_ROWS, ln_body, 0)


def _ffn(h2, x1, mod, cond_row, layer, w_up, w_dw, b_dw, w_down, ln_g, ln_b, rows, cols, alpha):
    total, d = x1.shape
    d_ff = w_down.shape[1]
    tm = FFN_TOKENS
    tf = FFN_COLS
    nj = d_ff // tf
    kern = functools.partial(_ffn_kernel, rows=rows, cols=cols, tokens=tm, alpha=alpha)
    return pl.pallas_call(
        kern,
        out_shape=jax.ShapeDtypeStruct((total, d), F32),
        grid=(total // tm, nj),
        in_specs=[
            pl.BlockSpec((tm, d), lambda i, j: (i, 0)),
            pl.BlockSpec((tm, d), lambda i, j: (i, 0)),
            pl.BlockSpec((None, None, 6, d), lambda i, j: (layer, cond_row(i * tm), 0, 0)),
            pl.BlockSpec((None, d, tf), lambda i, j: (layer, 0, j)),
            pl.BlockSpec((None, d, tf), lambda i, j: (layer, 0, nj + j)),
            pl.BlockSpec((None, 9, tf), lambda i, j: (layer, 0, j)),
            pl.BlockSpec((None, 1, tf), lambda i, j: (layer, 0, j)),
            pl.BlockSpec((None, tf, d), lambda i, j: (layer, j, 0)),
            pl.BlockSpec((None, 1, d), lambda i, j: (layer, 0, 0)),
            pl.BlockSpec((None, 1, d), lambda i, j: (layer, 0, 0)),
        ],
        out_specs=pl.BlockSpec((tm, d), lambda i, j: (i, 0)),
        scratch_shapes=[
            pltpu.VMEM((tm + 2 * ((cols if rows > 1 else 0) + SUBLANES), tf), F32),
            pltpu.VMEM((tm, tf), F32),
            pltpu.VMEM((tm, tf), BF16),
            pltpu.VMEM((d, tf), BF16), pltpu.VMEM((d, tf), BF16), pltpu.VMEM((tf, d), BF16),
        ],
        compiler_params=_params("parallel", "arbitrary"),
        name="ffn",
    )(h2, x1, mod, w_up, w_up, w_dw, b_dw, w_down, ln_g, ln_b)


def _trunk_layer(x, batch, rows, cols, cond_row, layer, p, pmat, alpha, dh, state_kw):
    seq = rows * cols
    hp = _mlstm_heads_per_step(seq, dh)
    mod = p["ada"]
    proj, v_t, gates_t = _inproj(x, mod, cond_row, layer, p["w_in_main"], p["w_v_t"], p["w_gate_t"][hp],
                                 p["b_gate"][hp])
    pool_out = _pool(proj, pmat, layer, p["w_pool"], p["pool_scale"], rows, cols, min(FFN_TOKENS, batch * seq))
    ml_out, states = _mlstm(proj, v_t, gates_t, layer, p["w_qk_conv"], p["hn_g"], batch, seq, N_POOL_GROUPS * LANES,
                            dh, **state_kw)
    x1, h2 = _outproj(pool_out, ml_out, x, mod, cond_row, layer, p["w_out"], p["ln1_g"], p["ln1_b"], alpha)
    x2 = _ffn(h2, x1, mod, cond_row, layer, p["w_up"], p["w_dw"], p["b_dw"], p["w_down"], p["ln2_g"], p["ln2_b"],
              rows, cols, alpha)
    return x2, states


def kernel(x_prompt, x_sample, c, state_C, state_n, state_m, c_ctx, w_ada, b_ada, w_in, b_gates, w_qk_conv, hn_g,
           w_pool, pool_scale, w_out, ln1_g, ln1_b, w_up, w_dw, b_dw, w_down, ln2_g, ln2_b):
    b_p, l_p, d = x_prompt.shape
    b_s, l_s, _ = x_sample.shape
    depth = w_ada.shape[0]
    heads = MLSTM_HEADS
    dh = state_C.shape[-1]
    pool_w = N_POOL_GROUPS * LANES
    mw = heads * dh
    n_main = pool_w + 4 * mw
    v0 = pool_w + 2 * mw
    d_ff = w_down.shape[1]
    alpha = (2.0 * depth) ** 0.25
    rows_s = l_s // GRID_W
    assert b_s + 1 <= COND_ROWS and l_p % POOL_TILE == 0 and POOL_TILE % GRID_W == 0
    assert l_p & (l_p - 1) == 0 and GRID_W & (GRID_W - 1) == 0

    cond = jnp.concatenate([c_ctx[None, :], c, jnp.zeros((COND_ROWS - 1 - b_s, d), F32)], axis=0)

    pmat_p = _pool_col_matrices(l_p)
    pmat_s = _pool_col_matrices(GRID_W)
    assert min(FFN_TOKENS, b_s * l_s) == l_s
    hps = sorted({_mlstm_heads_per_step(l_p, dh), _mlstm_heads_per_step(l_s, dh)})

    w_gate = w_in[:, :, n_main:]
    p = dict(
        ada=_ada(cond, w_ada, b_ada).reshape(depth, COND_ROWS, 6, d),
        w_in_main=jnp.concatenate([w_in[:, :, :v0], w_in[:, :, v0 + mw:n_main]], axis=2).astype(BF16),
        w_v_t=w_in[:, :, v0:v0 + mw].transpose(0, 2, 1).astype(BF16),
        w_gate_t={hp: w_gate[:, :, _gate_row_order(hp)].transpose(0, 2, 1).astype(BF16) for hp in hps},
        b_gate={hp: b_gates[:, _gate_row_order(hp)][:, :, None] for hp in hps},
        w_qk_conv=w_qk_conv, hn_g=hn_g[:, None, :],
        w_pool=w_pool.astype(BF16), pool_scale=pool_scale[:, None, :],
        w_out=w_out.astype(BF16), ln1_g=ln1_g[:, None, :], ln1_b=ln1_b[:, None, :],
        w_up=w_up, w_dw=w_dw.reshape(depth, 9, d_ff), b_dw=b_dw[:, None, :],
        w_down=w_down, ln2_g=ln2_g[:, None, :], ln2_b=ln2_b[:, None, :])
    init = (state_C, state_n.reshape(b_s, depth, 2, heads, 1, dh), state_m.reshape(b_s, depth, 2, heads, 1, 1))

    y_p = x_prompt.reshape(b_p * l_p, d)
    y_s = x_sample.reshape(b_s * l_s, d)
    states = None
    for l in range(depth):
        y_p, states = _trunk_layer(y_p, b_p, 1, l_p, lambda t: 0, l, p, pmat_p, alpha, dh,
                                   dict(layer=l, depth=depth, prev=states))
        y_s, _ = _trunk_layer(y_s, b_s, rows_s, GRID_W, lambda t: 1 + t // l_s, l, p, pmat_s, alpha, dh,
                              dict(init_state=init))
    new_c, new_n, new_m = states
    return (y_p.reshape(b_p, l_p, d), y_s.reshape(b_s, l_s, d), new_c, new_n.reshape(b_p, depth, 2, heads, dh),
            new_m.reshape(b_p, depth, 2, heads))
```
